```python
import math
import jax, jax.numpy as jnp
from jax import lax
import numpy as np

D_MODEL = 1024
BATCH = 32
SEQ = 2048
DEPTH = 2
DEC_BATCH = 32
DEC_SEQ = 64
PAST_LEN = 4096

CHUNK = 64
QBLOCK = 128
EPS = 1e-6
H_A = 4
DH_A = 64
DV_A = 128
H_B = 8
DH_B = 64
H_IDX = 4
D_IDX = 64
TOPK_MAX = 256
H_C = 4
DK_C = 64
DV_C = 128
W_BR = 512
N_BRANCH = 3
NUM_BUCKETS = 32
MAX_DISTANCE = 128
ROPE_BASE = 10000.0
IN_SIZES = (H_A * 2 * DH_A, H_A * 2 * DH_A, H_A * DV_A, W_BR,
            H_B * DH_B, DH_B, DH_B, H_IDX * D_IDX, D_IDX, H_IDX, W_BR,
            H_C * DK_C, H_C * DK_C, H_C * DV_C, W_BR,
            N_BRANCH * D_MODEL)
D_IN = sum(IN_SIZES)

kernel_name = 'hybrid_streaming_encoder_step'


def rmsnorm(x, g):
    xf = x.astype(jnp.float32)
    xf = xf * lax.rsqrt(jnp.mean(xf * xf, axis=-1, keepdims=True) + EPS)
    return xf.astype(x.dtype) * g


def rotary(x, pos):
    half = x.shape[-1] // 2
    inv = ROPE_BASE ** (-jnp.arange(half, dtype=jnp.float32) / half)
    ang = pos.astype(jnp.float32)[:, None] * inv[None, :]
    cos = jnp.cos(ang)[:, None, :]
    sin = jnp.sin(ang)[:, None, :]
    xf = x.astype(jnp.float32)
    x1, x2 = xf[..., :half], xf[..., half:]
    return jnp.concatenate([x1 * cos - x2 * sin, x1 * sin + x2 * cos], axis=-1).astype(x.dtype)


def t5_bucket(rel):
    nb = NUM_BUCKETS // 2
    max_exact = nb // 2
    ret = jnp.where(rel > 0, nb, 0)
    n = jnp.abs(rel)
    large = max_exact + (jnp.log(jnp.maximum(n, 1).astype(jnp.float32) / max_exact)
                         / math.log(MAX_DISTANCE / max_exact) * (nb - max_exact)).astype(jnp.int32)
    large = jnp.minimum(large, nb - 1)
    return ret + jnp.where(n < max_exact, n, large)


def visible(q_pos, k_pos):
    return k_pos < (q_pos // CHUNK + 1) * CHUNK


def block_map(fn, qs, q_pos):
    T = q_pos.shape[0]
    nb = T // QBLOCK
    blocks = tuple(jnp.moveaxis(a.reshape(a.shape[0], nb, QBLOCK, *a.shape[2:]), 1, 0) for a in qs)
    pos_b = q_pos.reshape(nb, QBLOCK)
    out = lax.map(lambda args: fn(*args), (blocks, pos_b))
    out = jnp.moveaxis(out, 0, 1)
    return out.reshape(out.shape[0], T, *out.shape[3:])


def project(x, pos, norm_g, w_in, a_qk_g, b_qk_g):
    B, T, _ = x.shape
    z = rmsnorm(x, norm_g) @ w_in
    points, acc = [], 0
    for s in IN_SIZES[:-1]:
        acc += s
        points.append(acc)
    aq, ak, av, ag, bq, bk, bv, bqi, bki, bwi, bg, cq, ck, cv, cg, mg = jnp.split(z, points, axis=-1)
    return {
        'aq': rmsnorm(aq.reshape(B, T, H_A, 2, DH_A), a_qk_g[0]),
        'ak': rmsnorm(ak.reshape(B, T, H_A, 2, DH_A), a_qk_g[1]),
        'av': av.reshape(B, T, H_A, DV_A),
        'ag': ag,
        'bq': rmsnorm(bq.reshape(B, T, H_B, DH_B), b_qk_g[0]),
        'bk': rmsnorm(bk, b_qk_g[1]),
        'bv': bv,
        'bqi': bqi.reshape(B, T, H_IDX, D_IDX),
        'bki': bki,
        'bwi': bwi,
        'bg': bg,
        'cq': rotary(cq.reshape(B, T, H_C, DK_C), pos),
        'ck': rotary(ck.reshape(B, T, H_C, DK_C), pos) * DK_C ** -0.5,
        'cv': cv.reshape(B, T, H_C, DV_C),
        'cg': cg,
        'mg': mg,
    }


def diff_attention(q, k, v, q_pos, k_pos, bias_a, lam, lam_init, subln_g):
    B, Tq = q.shape[0], q.shape[1]
    bias = bias_a[t5_bucket(k_pos[None, :] - q_pos[:, None])]
    logits = jnp.einsum('bqhcd,bshcd->bhcqs', q, k).astype(jnp.float32) * DH_A ** -0.5
    logits = logits + jnp.transpose(bias, (2, 0, 1)).astype(jnp.float32)[None, :, None]
    mask = visible(q_pos[:, None], k_pos[None, :])
    logits = jnp.where(mask, logits, -1e30)
    p = jax.nn.softmax(logits, axis=-1)
    a = p[:, :, 0] - lam * p[:, :, 1]
    o = jnp.einsum('bhqs,bshv->bqhv', a.astype(v.dtype), v)
    o = rmsnorm(o, subln_g) * (1.0 - lam_init)
    return o.reshape(B, Tq, H_A * DV_A)


def dsa_attention(q, qi, wi, q_pos, k, v, ki, k_pos, bias_b, topk):
    B, Tq = q.shape[0], q.shape[1]
    mask = visible(q_pos[:, None], k_pos[None, :])
    rel = jax.nn.relu(jnp.einsum('bqhd,bsd->bqhs', qi, ki).astype(jnp.float32))
    score = jnp.einsum('bqh,bqhs->bqs', wi.astype(jnp.float32), rel)
    score = jnp.where(mask[None], score, -jnp.inf)
    _, idx = lax.top_k(score, topk)
    gather = jax.vmap(lambda a, i: a[i])
    k_sel = gather(k, idx)
    v_sel = gather(v, idx)
    sel_pos = k_pos[idx]
    valid = visible(q_pos[None, :, None], sel_pos)
    bias = bias_b[t5_bucket(sel_pos - q_pos[None, :, None])]
    logits = jnp.einsum('bqhd,bqkd->bqhk', q, k_sel).astype(jnp.float32) * DH_B ** -0.5
    logits = logits + jnp.moveaxis(bias, -1, 2).astype(jnp.float32)
    logits = jnp.where(valid[:, :, None, :], logits, -1e30)
    p = jax.nn.softmax(logits, axis=-1)
    o = jnp.einsum('bqhk,bqkd->bqhd', p.astype(v.dtype), v_sel)
    return o.reshape(B, Tq, H_B * DH_B)


def retention_chunk(q, k, v, state, log_gamma):
    C = q.shape[1]
    n = jnp.arange(C, dtype=jnp.float32)
    diff = n[:, None] - n[None, :]
    dmat = jnp.where(diff >= 0, jnp.exp(log_gamma[:, None, None] * jnp.maximum(diff, 0.0)[None]), 0.0)
    inner = jnp.einsum('bqhd,bshd->bhqs', q, k) * dmat[None]
    decay_q = jnp.exp((n[:, None] + 1.0) * log_gamma[None, :])
    o = jnp.einsum('bhqs,bshv->bqhv', inner, v) + jnp.einsum('bqhd,bhdv->bqhv', q * decay_q[None, :, :, None], state)
    decay_k = jnp.exp((C - 1.0 - n)[:, None] * log_gamma[None, :])
    new_state = jnp.exp(C * log_gamma)[None, :, None, None] * state + jnp.einsum('bshd,bshv->bhdv', k * decay_k[None, :, :, None], v)
    return o, new_state


def retention_prompt(q, k, v, log_gamma):
    B, T = q.shape[0], q.shape[1]
    nc = T // CHUNK

    def to_chunks(a):
        return jnp.moveaxis(a.astype(jnp.float32).reshape(B, nc, CHUNK, *a.shape[2:]), 1, 0)

    def step(s, inp):
        o, s = retention_chunk(inp[0], inp[1], inp[2], s, log_gamma)
        return s, o

    s0 = jnp.zeros((B, H_C, DK_C, DV_C), jnp.float32)
    s, o = lax.scan(step, s0, (to_chunks(q), to_chunks(k), to_chunks(v)))
    o = jnp.moveaxis(o, 0, 1).reshape(B, T, H_C, DV_C)
    return o, s


def merge(oa, ob, oc, P, w_branch, w_out):
    B, T, _ = oa.shape
    br = jnp.stack([oa * jax.nn.silu(P['ag']), ob * jax.nn.silu(P['bg']), oc * jax.nn.silu(P['cg'])], axis=2)
    proj = jnp.einsum('btnw,nwd->btnd', br, w_branch)
    gates = jax.nn.sigmoid(P['mg'].reshape(B, T, N_BRANCH, D_MODEL))
    return jnp.sum(gates * proj, axis=2) @ w_out


def setup_inputs(seed: int = 0) -> dict:
    key = jax.random.key(seed)
    ks = jax.random.split(key, 18)
    f32 = jnp.float32

    def nrm(k, shape, s):
        return jax.random.normal(k, shape, f32) * s

    return {
        'x_prompt': nrm(ks[0], (BATCH, SEQ, D_MODEL), 1.0),
        'x_sample': nrm(ks[1], (DEC_BATCH, DEC_SEQ, D_MODEL), 1.0),
        'cache_a_k': nrm(ks[2], (DEPTH, DEC_BATCH, PAST_LEN, H_A, 2, DH_A), 1.0),
        'cache_a_v': nrm(ks[3], (DEPTH, DEC_BATCH, PAST_LEN, H_A, DV_A), 1.0),
        'cache_b_k': nrm(ks[4], (DEPTH, DEC_BATCH, PAST_LEN, DH_B), 1.0),
        'cache_b_v': nrm(ks[5], (DEPTH, DEC_BATCH, PAST_LEN, DH_B), 1.0),
        'cache_b_kidx': nrm(ks[6], (DEPTH, DEC_BATCH, PAST_LEN, D_IDX), 1.0),
        'state_c': nrm(ks[7], (DEPTH, DEC_BATCH, H_C, DK_C, DV_C), 0.5),
        'rel_bias': nrm(ks[8], (NUM_BUCKETS, H_A + H_B), 0.5),
        'norm_g': 1.0 + nrm(ks[9], (DEPTH, D_MODEL), 0.02),
        'w_in': nrm(ks[10], (DEPTH, D_MODEL, D_IN), D_MODEL ** -0.5),
        'a_qk_g': 1.0 + nrm(ks[11], (DEPTH, 2, DH_A), 0.02),
        'a_lambda': nrm(ks[12], (DEPTH, 4, DH_A), 0.1),
        'a_subln_g': 1.0 + nrm(ks[13], (DEPTH, DV_A), 0.02),
        'b_qk_g': 1.0 + nrm(ks[14], (DEPTH, 2, DH_B), 0.02),
        'c_gn_g': 1.0 + nrm(ks[15], (DEPTH, DV_C), 0.02),
        'w_branch': nrm(ks[16], (DEPTH, N_BRANCH, W_BR, D_MODEL), W_BR ** -0.5),
        'w_out': nrm(ks[17], (DEPTH, D_MODEL, D_MODEL), D_MODEL ** -0.5),
    }


def reference(x_prompt, x_sample, cache_a_k, cache_a_v, cache_b_k, cache_b_v, cache_b_kidx, state_c,
              rel_bias, norm_g, w_in, a_qk_g, a_lambda, a_subln_g, b_qk_g, c_gn_g, w_branch, w_out):
    Bp, Tp, _ = x_prompt.shape
    Bs, Ts, _ = x_sample.shape
    past = cache_a_k.shape[2]
    pos_p = jnp.arange(Tp, dtype=jnp.int32)
    pos_s = past + jnp.arange(Ts, dtype=jnp.int32)
    kpos_s = jnp.arange(past + Ts, dtype=jnp.int32)
    topk_p = min(TOPK_MAX, Tp // 4)
    topk_s = min(TOPK_MAX, (past + Ts) // 4)
    bias_a = rel_bias[:, :H_A]
    bias_b = rel_bias[:, H_A:]
    log_gamma = jnp.log(1.0 - 2.0 ** (-5.0 - jnp.arange(H_C, dtype=jnp.float32)))

    yp, ys = x_prompt, x_sample
    pak, pav, pbk, pbv, pbi, psc = [], [], [], [], [], []
    sak, sav, sbk, sbv, sbi, ssc = [], [], [], [], [], []
    for l in range(DEPTH):
        lam_init = 0.8 - 0.6 * math.exp(-0.3 * l)
        lp = a_lambda[l].astype(jnp.float32)
        lam = jnp.exp(jnp.sum(lp[0] * lp[1])) - jnp.exp(jnp.sum(lp[2] * lp[3])) + lam_init

        P = project(yp, pos_p, norm_g[l], w_in[l], a_qk_g[l], b_qk_g[l])
        oa = block_map(lambda qs, qp: diff_attention(qs[0], P['ak'], P['av'], qp, pos_p, bias_a, lam, lam_init, a_subln_g[l]),
                       (P['aq'],), pos_p)
        ob = block_map(lambda qs, qp: dsa_attention(qs[0], qs[1], qs[2], qp, P['bk'], P['bv'], P['bki'], pos_p, bias_b, topk_p),
                       (P['bq'], P['bqi'], P['bwi']), pos_p)
        oc, sc = retention_prompt(P['cq'], P['ck'], P['cv'], log_gamma)
        oc = rmsnorm(oc.astype(yp.dtype), c_gn_g[l]).reshape(Bp, Tp, W_BR)
        yp = yp + merge(oa, ob, oc, P, w_branch[l], w_out[l])
        pak.append(P['ak'])
        pav.append(P['av'])
        pbk.append(P['bk'])
        pbv.append(P['bv'])
        pbi.append(P['bki'])
        psc.append(sc.astype(x_prompt.dtype))

        S = project(ys, pos_s, norm_g[l], w_in[l], a_qk_g[l], b_qk_g[l])
        ka = jnp.concatenate([cache_a_k[l], S['ak']], axis=1)
        va = jnp.concatenate([cache_a_v[l], S['av']], axis=1)
        kb = jnp.concatenate([cache_b_k[l], S['bk']], axis=1)
        vb = jnp.concatenate([cache_b_v[l], S['bv']], axis=1)
        kib = jnp.concatenate([cache_b_kidx[l], S['bki']], axis=1)
        oa = diff_attention(S['aq'], ka, va, pos_s, kpos_s, bias_a, lam, lam_init, a_subln_g[l])
        ob = dsa_attention(S['bq'], S['bqi'], S['bwi'], pos_s, kb, vb, kib, kpos_s, bias_b, topk_s)
        oc, ss = retention_chunk(S['cq'].astype(jnp.float32), S['ck'].astype(jnp.float32), S['cv'].astype(jnp.float32),
                                 state_c[l].astype(jnp.float32), log_gamma)
        oc = rmsnorm(oc.astype(ys.dtype), c_gn_g[l]).reshape(Bs, Ts, W_BR)
        ys = ys + merge(oa, ob, oc, S, w_branch[l], w_out[l])
        sak.append(S['ak'])
        sav.append(S['av'])
        sbk.append(S['bk'])
        sbv.append(S['bv'])
        sbi.append(S['bki'])
        ssc.append(ss.astype(state_c.dtype))

    return (yp, ys,
            jnp.stack(pak), jnp.stack(pav), jnp.stack(pbk), jnp.stack(pbv), jnp.stack(pbi), jnp.stack(psc),
            jnp.stack(sak), jnp.stack(sav), jnp.stack(sbk), jnp.stack(sbv), jnp.stack(sbi), jnp.stack(ssc))
```

```python
import functools
import math

import jax
import jax.numpy as jnp
from jax import lax
from jax.experimental import pallas as pl
from jax.experimental.pallas import tpu as pltpu

D_MODEL = 1024
CHUNK = 64
EPS = 1e-6
H_A = 4
DH_A = 64
DV_A = 128
H_B = 8
DH_B = 64
H_IDX = 4
D_IDX = 64
TOPK_MAX = 256
H_C = 4
DK_C = 64
DV_C = 128
W_BR = 512
N_BRANCH = 3
NUM_BUCKETS = 32
MAX_DISTANCE = 128
ROPE_BASE = 10000.0

LANE = 128
TQ = 128
NEG = -1e30
INT_MIN = -(2 ** 31)
F32 = jnp.float32
BF16 = jnp.bfloat16
VMEM_LIMIT = 56 * 1024 * 1024

_SEC = {}
_off = 0
for _name, _n in (("aq", 512), ("ak", 512), ("av", 512), ("ag", 512), ("bq", 512), ("bqi", 256),
                  ("kk", 128), ("vv", 128), ("kiki", 128), ("wi", 128), ("bg", 512),
                  ("cq", 512), ("ck", 512), ("cv", 512), ("cg", 512), ("mg", 3072)):
    _SEC[_name] = (_off, _n)
    _off += _n
D_IN_PAD = _off


def _dot(a, b):
    return jnp.dot(a, b, preferred_element_type=F32)


def _dot_nt(a, b):
    return lax.dot_general(a, b, (((1,), (1,)), ((), ())), preferred_element_type=F32)


def _dot_tn(a, b):
    return lax.dot_general(a, b, (((0,), (0,)), ((), ())), preferred_element_type=F32)


def _proj_kernel(x_ref, g_ref, w_ref, cq_cos_ref, cq_sin_ref, ck_cos_ref, ck_sin_ref,
                 gaq_ref, gak_ref, gbq_ref, gbk_ref, g64_ref,
                 aq_ref, ak_ref, akb_ref, av_ref, avb_ref, sag_ref, bq_ref, bqi_ref,
                 bk_ref, kkb_ref, bv_ref, vvb_ref, bki_ref, kib_ref, wif_ref, sbg_ref,
                 cq_ref, ck_ref, cv_ref, scg_ref, smg_ref):
    x = x_ref[...]
    ms = jnp.mean(x * x, axis=-1, keepdims=True)
    xn = ((x * lax.rsqrt(ms + EPS)) * g_ref[...]).astype(BF16)

    def mm(name, sub=None):
        c0, n = _SEC[name]
        if sub is not None:
            c0, n = c0 + sub[0], sub[1]
        return _dot(xn, w_ref[:, c0:c0 + n])

    def group_norm64(z, gain):
        zz = (z * z).astype(BF16)
        parts = [_dot(zz[:, j * LANE:(j + 1) * LANE], g64_ref[...]) for j in range(z.shape[1] // LANE)]
        ss = parts[0] if len(parts) == 1 else jnp.concatenate(parts, axis=1)
        return (z * lax.rsqrt(ss * (1.0 / 64.0) + EPS)) * gain

    aq_ref[...] = group_norm64(mm("aq"), gaq_ref[...]).astype(BF16)
    ak = group_norm64(mm("ak"), gak_ref[...])
    ak_ref[...] = ak
    akb_ref[...] = ak.astype(BF16)
    av = mm("av")
    av_ref[...] = av
    avb_ref[...] = av.astype(BF16)
    z = mm("ag")
    sag_ref[...] = (z * jax.nn.sigmoid(z)).astype(BF16)
    bq_ref[...] = group_norm64(mm("bq"), gbq_ref[...]).astype(BF16)
    bqi_ref[...] = mm("bqi").astype(BF16)
    kk = group_norm64(mm("kk"), gbk_ref[...])
    bk_ref[...] = kk[:, :DH_B]
    kkb_ref[...] = kk.astype(BF16)
    vv = mm("vv")
    bv_ref[...] = vv[:, :DH_B]
    vvb_ref[...] = vv.astype(BF16)
    kiki = mm("kiki")
    bki_ref[...] = kiki[:, :D_IDX]
    kib_ref[...] = kiki.astype(BF16)
    wif_ref[...] = mm("wi")
    z = mm("bg")
    sbg_ref[...] = (z * jax.nn.sigmoid(z)).astype(BF16)
    z = mm("cq")
    cq_ref[...] = (z[:, :256] * cq_cos_ref[...] + z[:, 256:] * cq_sin_ref[...]).astype(BF16)
    z = mm("ck")
    ck_ref[...] = (z[:, :256] * ck_cos_ref[...] + z[:, 256:] * ck_sin_ref[...]).astype(BF16)
    cv_ref[...] = mm("cv").astype(BF16)
    z = mm("cg")
    scg_ref[...] = (z * jax.nn.sigmoid(z)).astype(BF16)
    for j in range(N_BRANCH * 2):
        z = mm("mg", (j * 512, 512))
        smg_ref[:, j * 512:(j + 1) * 512] = jax.nn.sigmoid(z).astype(BF16)


def _rot_half_cols(w):
    k = w.shape[0]
    w4 = w.reshape(k, -1, DK_C)
    half = DK_C // 2
    return jnp.concatenate([-w4[..., half:], w4[..., :half]], axis=-1).reshape(k, -1)


def _rearrange_w_in(w):
    s = [0]
    for n in (512, 512, 512, 512, 512, 64, 64, 256, 64, 4, 512, 256, 256, 512, 512, 3072):
        s.append(s[-1] + n)
    (aq, ak, av, ag, bq, bk, bv, bqi, bki, bwi, bg, cq, ck, cv, cg, mg) = [w[:, s[i]:s[i + 1]] for i in range(16)]
    wi_pad = jnp.zeros((w.shape[0], LANE - H_IDX), w.dtype)
    cols = [aq, ak, av, ag, bq, bqi, bk, bk, bv, bv, bki, bki, bwi, wi_pad, bg,
            cq, _rot_half_cols(cq), ck, _rot_half_cols(ck), cv, cg, mg]
    out = jnp.concatenate(cols, axis=1).astype(BF16)
    assert out.shape[1] == D_IN_PAD
    return out


def _rope_tables(pos, rows):
    half = DK_C // 2
    inv = ROPE_BASE ** (-jnp.arange(half, dtype=F32) / half)
    ang = pos.astype(F32)[:, None] * inv[None, :]
    cos = jnp.tile(jnp.cos(ang), (1, 2 * H_C))
    sin = jnp.tile(jnp.sin(ang), (1, 2 * H_C))
    reps = max(1, rows // cos.shape[0])
    cos = jnp.tile(cos, (reps, 1))
    sin = jnp.tile(sin, (reps, 1))
    kscale = DK_C ** -0.5
    return cos, sin, cos * kscale, sin * kscale


def _project(x2d, pos, tm, norm_g, w_pad, a_qk_g, b_qk_g, g64):
    n = x2d.shape[0]
    cq_cos, cq_sin, ck_cos, ck_sin = _rope_tables(pos, tm)
    ntab = cq_cos.shape[0] // tm
    gaq = jnp.tile(a_qk_g[0], 8)[None, :]
    gak = jnp.tile(a_qk_g[1], 8)[None, :]
    gbq = jnp.tile(b_qk_g[0], 8)[None, :]
    gbk = jnp.tile(b_qk_g[1], 2)[None, :]

    def row(width):
        return pl.BlockSpec((tm, width), lambda i: (i, 0))

    def const(shape, single=False):
        if single:
            return pl.BlockSpec(shape, lambda i: (0, 0), pipeline_mode=pl.Buffered(1))
        return pl.BlockSpec(shape, lambda i: (0, 0))

    tab = pl.BlockSpec((tm, 256), lambda i: (i % ntab, 0))
    outs = [("aq", 512, BF16), ("ak", 512, F32), ("akb", 512, BF16), ("av", 512, F32), ("avb", 512, BF16),
            ("sag", 512, BF16), ("bq", 512, BF16), ("bqi", 256, BF16), ("bk", 64, F32), ("kkb", 128, BF16),
            ("bv", 64, F32), ("vvb", 128, BF16), ("bki", 64, F32), ("kib", 128, BF16), ("wif", 128, F32),
            ("sbg", 512, BF16), ("cq", 256, BF16), ("ck", 256, BF16), ("cv", 512, BF16), ("scg", 512, BF16),
            ("smg", 3072, BF16)]
    res = pl.pallas_call(
        _proj_kernel,
        grid=(n // tm,),
        in_specs=[row(D_MODEL), const((1, D_MODEL)), const((D_MODEL, D_IN_PAD), single=True),
                  tab, tab, tab, tab,
                  const((1, 512)), const((1, 512)), const((1, 512)), const((1, 128)), const((LANE, LANE))],
        out_specs=[row(w) for _, w, _ in outs],
        out_shape=[jax.ShapeDtypeStruct((n, w), dt) for _, w, dt in outs],
        compiler_params=pltpu.CompilerParams(dimension_semantics=("arbitrary",), vmem_limit_bytes=VMEM_LIMIT),
        name="proj",
    )(x2d, norm_g[None, :], w_pad, cq_cos, cq_sin, ck_cos, ck_sin, gaq, gak, gbq, gbk, g64)
    return {name: r for (name, _, _), r in zip(outs, res)}


def _t5_bucket(rel):
    nb = NUM_BUCKETS // 2
    max_exact = nb // 2
    ret = jnp.where(rel > 0, nb, 0)
    n = jnp.abs(rel)
    large = max_exact + (jnp.log(jnp.maximum(n, 1).astype(F32) / max_exact)
                         / math.log(MAX_DISTANCE / max_exact) * (nb - max_exact)).astype(jnp.int32)
    large = jnp.minimum(large, nb - 1)
    return ret + jnp.where(n < max_exact, n, large)


def _bias_slabs(bias_cols, tq):
    i = jnp.arange(tq, dtype=jnp.int32)[:, None]
    j = jnp.arange(LANE, dtype=jnp.int32)[None, :]
    vis0 = j < (i // CHUNK + 1) * CHUNK
    tiles, masks = [], []
    for d in (-2, -1, 0):
        rel = d * LANE + j - i
        b = jnp.transpose(bias_cols[_t5_bucket(rel)], (2, 0, 1)).astype(F32)
        if d == 0:
            b = jnp.where(vis0[None], b, NEG)
            masks.append(jnp.where(vis0, 0.0, -jnp.inf).astype(F32))
        else:
            masks.append(jnp.zeros((tq, LANE), F32))
        tiles.append(b)
    tiles.append(jnp.full_like(tiles[0], NEG))
    masks.append(jnp.full((tq, LANE), -jnp.inf, F32))
    return jnp.stack(tiles, axis=1), jnp.stack(masks, axis=0)


def _tile_types(qb, nkt):
    return [jnp.clip(kt - qb, -2, 1) + 2 for kt in range(nkt)]


def _variants(nqb, qb0, nkt_total):
    if nqb == 1:
        return [(0, 1, nkt_total)]
    nvar = min(4, nqb)
    grp = nqb // nvar
    return [(v * grp, (v + 1) * grp, min(nkt_total, qb0 + (v + 1) * grp)) for v in range(nvar)]


def _attn_a_kernel(aq_ref, k_ref, v_ref, sag_ref, slab_ref, alam_ref, subg_ref, out_ref, *,
                   tq, qb0, variants, lam_init):
    qi = pl.program_id(1)
    qb = qi + qb0
    lp = alam_ref[...]
    lam = (jnp.exp(jnp.sum(lp[0:1] * lp[1:2], axis=1, keepdims=True))
           - jnp.exp(jnp.sum(lp[2:3] * lp[3:4], axis=1, keepdims=True)) + lam_init)
    lane = lax.broadcasted_iota(jnp.int32, (tq, LANE), 1)
    lo = lane < DH_A

    def body(nkt):
        sv = nkt * LANE
        ty = _tile_types(qb, nkt)
        for h in range(H_A):
            qp = aq_ref[0, :, h * LANE:(h + 1) * LANE] * jnp.asarray(DH_A ** -0.5, BF16)
            zero = jnp.zeros_like(qp)
            q2 = jnp.concatenate([jnp.where(lo, qp, zero), jnp.where(lo, zero, qp)], axis=0)
            s = _dot_nt(q2, k_ref[0, :sv, h * LANE:(h + 1) * LANE])
            bias = jnp.concatenate([slab_ref[h, ty[kt]] for kt in range(nkt)], axis=1)
            s = s + jnp.concatenate([bias, bias], axis=0)
            m = jnp.max(s, axis=1, keepdims=True)
            p = jnp.exp(s - m)
            l = jnp.sum(p, axis=1, keepdims=True)
            o2 = _dot(p.astype(BF16), v_ref[0, :sv, h * LANE:(h + 1) * LANE]) * (1.0 / l)
            o = o2[:tq] - lam * o2[tq:]
            o = o * lax.rsqrt(jnp.mean(o * o, axis=1, keepdims=True) + EPS) * subg_ref[...] * (1.0 - lam_init)
            gate = sag_ref[0, :, h * LANE:(h + 1) * LANE].astype(F32)
            out_ref[0, :, h * LANE:(h + 1) * LANE] = (o * gate).astype(BF16)

    if len(variants) == 1:
        body(variants[0][2])
    else:
        for lo_q, hi_q, nkt in variants:
            pl.when((qi >= lo_q) & (qi < hi_q))(functools.partial(body, nkt))


def _attn_a(aq, k, v, sag, slab, a_lambda, subln_g, *, tq, qb0, lam_init):
    b, t, _ = aq.shape
    s = k.shape[1]
    nqb = t // tq
    variants = _variants(nqb, qb0, s // LANE)
    kern = functools.partial(_attn_a_kernel, tq=tq, qb0=qb0, variants=variants, lam_init=lam_init)
    qspec = pl.BlockSpec((1, tq, 512), lambda bi, qi: (bi, qi, 0))
    kspec = pl.BlockSpec((1, s, 512), lambda bi, qi: (bi, 0, 0))
    return pl.pallas_call(
        kern,
        grid=(b, nqb),
        in_specs=[qspec, kspec, kspec, qspec,
                  pl.BlockSpec(slab.shape, lambda bi, qi: (0, 0, 0, 0)),
                  pl.BlockSpec((4, DH_A), lambda bi, qi: (0, 0)),
                  pl.BlockSpec((1, DV_A), lambda bi, qi: (0, 0))],
        out_specs=qspec,
        out_shape=jax.ShapeDtypeStruct((b, t, 512), BF16),
        compiler_params=pltpu.CompilerParams(dimension_semantics=("arbitrary", "arbitrary"),
                                             vmem_limit_bytes=VMEM_LIMIT),
        name="attn_a",
    )(aq, k, v, sag, slab, a_lambda, subln_g[None, :])


def _dsa_kernel(bq_ref, bqi_ref, wif_ref, kk_ref, vv_ref, ki_ref, sbg_ref, slab_ref, mslab_ref, tri_ref,
                out_ref, keys_ref, *, tq, qb0, variants, topk):
    qi = pl.program_id(1)
    qb = qi + qb0
    lane = lax.broadcasted_iota(jnp.int32, (tq, LANE), 1)
    lo = lane < DH_B
    kf = float(topk)

    def body(nkt):
        sv = nkt * LANE
        ty = _tile_types(qb, nkt)

        qis = []
        for hp in range(H_IDX // 2):
            qp = bqi_ref[0, :, hp * LANE:(hp + 1) * LANE]
            zero = jnp.zeros_like(qp)
            qis += [jnp.where(lo, qp, zero), jnp.where(lo, zero, qp)]
        r = jnp.maximum(_dot_nt(jnp.concatenate(qis, axis=0), ki_ref[0, :sv, :]), 0.0)
        wi = wif_ref[0]
        score = wi[:, 0:1] * r[0:tq]
        for h in range(1, H_IDX):
            score = score + wi[:, h:h + 1] * r[h * tq:(h + 1) * tq]
        score = score + jnp.concatenate([mslab_ref[ty[kt]] for kt in range(nkt)], axis=1)
        bits = pltpu.bitcast(score, jnp.int32)
        keys_ref[:, :sv] = bits ^ ((bits >> 31) & jnp.int32(0x7FFFFFFF))

        def search(i, t):
            cand = t + lax.shift_left(jnp.int32(1), 31 - i)
            acc = jnp.zeros((tq, LANE), F32)
            for kt in range(nkt):
                acc = acc + jnp.where(keys_ref[:, kt * LANE:(kt + 1) * LANE] >= cand, 1.0, 0.0)
            cnt = jnp.sum(acc, axis=1, keepdims=True)
            return jnp.where(cnt >= kf, cand, t)

        t = lax.fori_loop(0, 32, search, jnp.full((tq, 1), INT_MIN, jnp.int32))

        acc = jnp.zeros((tq, LANE), F32)
        for kt in range(nkt):
            acc = acc + jnp.where(keys_ref[:, kt * LANE:(kt + 1) * LANE] > t, 1.0, 0.0)
        need = kf - jnp.sum(acc, axis=1, keepdims=True)
        run = jnp.zeros((tq, 1), F32)
        sel_tiles = []
        for kt in range(nkt):
            kt_keys = keys_ref[:, kt * LANE:(kt + 1) * LANE]
            eq = jnp.where(kt_keys == t, 1.0, 0.0)
            pos = _dot(eq.astype(BF16), tri_ref[...]) + run
            sel_tiles.append(jnp.where(kt_keys > t, 1.0, jnp.where(pos <= need, eq, 0.0)))
            run = run + jnp.sum(eq, axis=1, keepdims=True)
        sel = jnp.concatenate(sel_tiles, axis=1) > 0.5

        for hp in range(H_B // 2):
            qp = bq_ref[0, :, hp * LANE:(hp + 1) * LANE] * jnp.asarray(DH_B ** -0.5, BF16)
            zero = jnp.zeros_like(qp)
            q2 = jnp.concatenate([jnp.where(lo, qp, zero), jnp.where(lo, zero, qp)], axis=0)
            s = _dot_nt(q2, kk_ref[0, :sv, :])
            ps = []
            for j in range(2):
                h = hp * 2 + j
                bias = jnp.concatenate([slab_ref[h, ty[kt]] for kt in range(nkt)], axis=1)
                sj = jnp.where(sel, s[j * tq:(j + 1) * tq] + bias, NEG)
                ps.append(sj)
            s2 = jnp.concatenate(ps, axis=0)
            m = jnp.max(s2, axis=1, keepdims=True)
            p = jnp.exp(s2 - m)
            l = jnp.sum(p, axis=1, keepdims=True)
            o2 = _dot(p.astype(BF16), vv_ref[0, :sv, :]) * (1.0 / l)
            o = jnp.where(lo, o2[:tq], o2[tq:])
            gate = sbg_ref[0, :, hp * LANE:(hp + 1) * LANE].astype(F32)
            out_ref[0, :, hp * LANE:(hp + 1) * LANE] = (o * gate).astype(BF16)

    if len(variants) == 1:
        body(variants[0][2])
    else:
        for lo_q, hi_q, nkt in variants:
            pl.when((qi >= lo_q) & (qi < hi_q))(functools.partial(body, nkt))


def _dsa(bq, bqi, wif, kk, vv, ki, sbg, slab, mslab, tri, *, tq, qb0, topk):
    b, t, _ = bq.shape
    s = kk.shape[1]
    nqb = t // tq
    variants = _variants(nqb, qb0, s // LANE)
    kern = functools.partial(_dsa_kernel, tq=tq, qb0=qb0, variants=variants, topk=topk)

    def qspec(w):
        return pl.BlockSpec((1, tq, w), lambda bi, qi: (bi, qi, 0))

    kspec = pl.BlockSpec((1, s, LANE), lambda bi, qi: (bi, 0, 0))
    return pl.pallas_call(
        kern,
        grid=(b, nqb),
        in_specs=[qspec(512), qspec(256), qspec(LANE), kspec, kspec, kspec, qspec(512),
                  pl.BlockSpec(slab.shape, lambda bi, qi: (0, 0, 0, 0)),
                  pl.BlockSpec(mslab.shape, lambda bi, qi: (0, 0, 0)),
                  pl.BlockSpec((LANE, LANE), lambda bi, qi: (0, 0))],
        out_specs=qspec(512),
        out_shape=jax.ShapeDtypeStruct((b, t, 512), BF16),
        scratch_shapes=[pltpu.VMEM((tq, s), jnp.int32)],
        compiler_params=pltpu.CompilerParams(dimension_semantics=("arbitrary", "arbitrary"),
                                             vmem_limit_bytes=VMEM_LIMIT),
        name="dsa",
    )(bq, bqi, wif, kk, vv, ki, sbg, slab, mslab, tri)


def _ret_kernel(cq_ref, ck_ref, cv_ref, scg_ref, st_ref, dmat_ref, dq_ref, dk_ref, gc_ref, bd_ref, gn_ref,
                out_ref, sto_ref, *, c, nchunks):
    lane = lax.broadcasted_iota(jnp.int32, (c, LANE), 1)
    lo = lane < DK_C
    for p in range(H_C // 2):
        st = st_ref[0, p]
        for ci in range(nchunks):
            rows = slice(ci * c, (ci + 1) * c)
            q = cq_ref[0, rows, p * LANE:(p + 1) * LANE]
            k = ck_ref[0, rows, p * LANE:(p + 1) * LANE]
            v = cv_ref[0, rows, p * 2 * DV_C:(p + 1) * 2 * DV_C]
            qd = (q.astype(F32) * dq_ref[p]).astype(BF16)
            cross = _dot(qd, st.astype(BF16))
            zero = jnp.zeros_like(q)
            for j in range(2):
                h = 2 * p + j
                qm = jnp.where(lo, q, zero) if j == 0 else jnp.where(lo, zero, q)
                a = (_dot_nt(qm, k) * dmat_ref[h]).astype(BF16)
                o = cross[:, j * DV_C:(j + 1) * DV_C] + _dot(a, v[:, j * DV_C:(j + 1) * DV_C])
                o = o * lax.rsqrt(jnp.mean(o * o, axis=1, keepdims=True) + EPS) * gn_ref[...]
                gate = scg_ref[0, rows, h * DV_C:(h + 1) * DV_C].astype(F32)
                out_ref[0, rows, h * DV_C:(h + 1) * DV_C] = (o * gate).astype(BF16)
            kd = (k.astype(F32) * dk_ref[p]).astype(BF16)
            st = (gc_ref[p] * st + _dot_tn(kd, v)) * bd_ref[...]
        sto_ref[0, p] = st


def _retention(cq, ck, cv, scg, state_pairs, log_gamma, gn_g, *, c):
    b, t, _ = cq.shape
    nchunks = t // c
    n = jnp.arange(c, dtype=F32)
    diff = n[:, None] - n[None, :]
    dmat = jnp.where(diff >= 0, jnp.exp(log_gamma[:, None, None] * jnp.maximum(diff, 0.0)[None]), 0.0)
    decay_q = jnp.exp((n[:, None] + 1.0) * log_gamma[None, :])
    decay_k = jnp.exp((c - 1.0 - n)[:, None] * log_gamma[None, :])
    decay_c = jnp.exp(c * log_gamma)

    def lanes(tab):
        return jnp.transpose(jnp.repeat(tab, DK_C, axis=1).reshape(c, H_C // 2, LANE), (1, 0, 2))

    gc = jnp.broadcast_to(jnp.repeat(decay_c, DK_C).reshape(H_C // 2, LANE, 1), (H_C // 2, LANE, 2 * DV_C))
    bd = (jnp.arange(LANE)[:, None] // DK_C == jnp.arange(2 * DV_C)[None, :] // DV_C).astype(F32)

    def full(a):
        return pl.BlockSpec(a.shape, lambda bi: (0,) * a.ndim)

    def tok(w):
        return pl.BlockSpec((1, t, w), lambda bi: (bi, 0, 0))

    stspec = pl.BlockSpec((1, H_C // 2, LANE, 2 * DV_C), lambda bi: (bi, 0, 0, 0))
    dq, dk, gnv = lanes(decay_q), lanes(decay_k), gn_g[None, :]
    return pl.pallas_call(
        functools.partial(_ret_kernel, c=c, nchunks=nchunks),
        grid=(b,),
        in_specs=[tok(256), tok(256), tok(512), tok(512), stspec,
                  full(dmat), full(dq), full(dk), full(gc), full(bd), full(gnv)],
        out_specs=[tok(512), stspec],
        out_shape=[jax.ShapeDtypeStruct((b, t, 512), BF16),
                   jax.ShapeDtypeStruct((b, H_C // 2, LANE, 2 * DV_C), F32)],
        compiler_params=pltpu.CompilerParams(dimension_semantics=("arbitrary",), vmem_limit_bytes=VMEM_LIMIT),
        name="retention",
    )(cq, ck, cv, scg, state_pairs, dmat, dq, dk, gc, bd, gnv)


def _state_to_pairs(st):
    b = st.shape[0]
    s4 = st.reshape(b, H_C // 2, 2, DK_C, DV_C)
    z = jnp.zeros_like(s4[:, :, 0])
    top = jnp.concatenate([s4[:, :, 0], z], axis=-1)
    bot = jnp.concatenate([z, s4[:, :, 1]], axis=-1)
    return jnp.concatenate([top, bot], axis=-2)


def _pairs_to_state(sp):
    b = sp.shape[0]
    s0 = sp[:, :, :DK_C, :DV_C]
    s1 = sp[:, :, DK_C:, DV_C:]
    return jnp.stack([s0, s1], axis=2).reshape(b, H_C, DK_C, DV_C)


def _merge_kernel(x_ref, ba_ref, bb_ref, bc_ref, smg_ref, wb_ref, wo_ref, y_ref):
    m = None
    for n, br in enumerate((ba_ref, bb_ref, bc_ref)):
        proj = _dot(br[...], wb_ref[n])
        term = smg_ref[:, n * D_MODEL:(n + 1) * D_MODEL].astype(F32) * proj
        m = term if m is None else m + term
    y_ref[...] = x_ref[...] + _dot(m.astype(BF16), wo_ref[...])


def _merge(x2d, br_a, br_b, br_c, smg, wb, wo, tm):
    n = x2d.shape[0]

    def row(w):
        return pl.BlockSpec((tm, w), lambda i: (i, 0))

    return pl.pallas_call(
        _merge_kernel,
        grid=(n // tm,),
        in_specs=[row(D_MODEL), row(W_BR), row(W_BR), row(W_BR), row(N_BRANCH * D_MODEL),
                  pl.BlockSpec((N_BRANCH, W_BR, D_MODEL), lambda i: (0, 0, 0)),
                  pl.BlockSpec((D_MODEL, D_MODEL), lambda i: (0, 0))],
        out_specs=row(D_MODEL),
        out_shape=jax.ShapeDtypeStruct((n, D_MODEL), F32),
        compiler_params=pltpu.CompilerParams(dimension_semantics=("arbitrary",), vmem_limit_bytes=VMEM_LIMIT),
        name="merge",
    )(x2d, br_a, br_b, br_c, smg, wb, wo)


def _with_past(cache2d, new, dup):
    c = cache2d.astype(BF16)
    if dup:
        c = jnp.concatenate([c, c], axis=-1)
    pad = (-(c.shape[1] + new.shape[1])) % LANE
    parts = [c, new]
    if pad:
        parts.append(jnp.zeros((new.shape[0], pad, new.shape[2]), BF16))
    return jnp.concatenate(parts, axis=1)


def kernel(x_prompt, x_sample, cache_a_k, cache_a_v, cache_b_k, cache_b_v, cache_b_kidx, state_c, rel_bias,
           norm_g, w_in, a_qk_g, a_lambda, a_subln_g, b_qk_g, c_gn_g, w_branch, w_out):
    bp, tp, _ = x_prompt.shape
    bs, ts, _ = x_sample.shape
    depth = w_in.shape[0]
    past = cache_a_k.shape[2]
    assert tp % TQ == 0 and ts == CHUNK and past % LANE == 0
    pos_p = jnp.arange(tp, dtype=jnp.int32)
    pos_s = past + jnp.arange(ts, dtype=jnp.int32)
    topk_p = min(TOPK_MAX, tp // 4)
    topk_s = min(TOPK_MAX, (past + ts) // 4)
    log_gamma = jnp.log(1.0 - 2.0 ** (-5.0 - jnp.arange(H_C, dtype=F32)))

    slab_a_p, mslab_p = _bias_slabs(rel_bias[:, :H_A], TQ)
    slab_b_p, _ = _bias_slabs(rel_bias[:, H_A:], TQ)
    slab_a_s, mslab_s = slab_a_p[:, :, :ts], mslab_p[:, :ts]
    slab_b_s = slab_b_p[:, :, :ts]
    lane_i = jnp.arange(LANE)
    g64 = (lane_i[:, None] // 64 == lane_i[None, :] // 64).astype(BF16)
    tri = (lane_i[:, None] <= lane_i[None, :]).astype(BF16)
    c_prompt = 256 if tp % 256 == 0 else CHUNK
    tm_p = 256
    tm_s = 256 if (bs * ts) % 256 == 0 else ts

    yp = x_prompt.reshape(bp * tp, D_MODEL)
    ys = x_sample.reshape(bs * ts, D_MODEL)
    outs_p = {k: [] for k in ("ak", "av", "bk", "bv", "bki", "sc")}
    outs_s = {k: [] for k in ("ak", "av", "bk", "bv", "bki", "sc")}
    for l in range(depth):
        lam_init = 0.8 - 0.6 * math.exp(-0.3 * l)
        w_pad = _rearrange_w_in(w_in[l])
        wb = w_branch[l].astype(BF16)
        wo = w_out[l].astype(BF16)

        P = _project(yp, pos_p, tm_p, norm_g[l], w_pad, a_qk_g[l], b_qk_g[l], g64)
        r3 = lambda a, b=bp, t=tp: a.reshape(b, t, a.shape[-1])
        br_a = _attn_a(r3(P["aq"]), r3(P["akb"]), r3(P["avb"]), r3(P["sag"]), slab_a_p, a_lambda[l],
                       a_subln_g[l], tq=TQ, qb0=0, lam_init=lam_init)
        br_b = _dsa(r3(P["bq"]), r3(P["bqi"]), r3(P["wif"]), r3(P["kkb"]), r3(P["vvb"]), r3(P["kib"]),
                    r3(P["sbg"]), slab_b_p, mslab_p, tri, tq=TQ, qb0=0, topk=topk_p)
        st0 = jnp.zeros((bp, H_C // 2, LANE, 2 * DV_C), F32)
        br_c, stp = _retention(r3(P["cq"]), r3(P["ck"]), r3(P["cv"]), r3(P["scg"]), st0, log_gamma, c_gn_g[l],
                               c=c_prompt)
        yp = _merge(yp, br_a.reshape(bp * tp, W_BR), br_b.reshape(bp * tp, W_BR), br_c.reshape(bp * tp, W_BR),
                    P["smg"], wb, wo, tm_p)
        outs_p["ak"].append(P["ak"].reshape(bp, tp, H_A, 2, DH_A))
        outs_p["av"].append(P["av"].reshape(bp, tp, H_A, DV_A))
        outs_p["bk"].append(P["bk"].reshape(bp, tp, DH_B))
        outs_p["bv"].append(P["bv"].reshape(bp, tp, DH_B))
        outs_p["bki"].append(P["bki"].reshape(bp, tp, D_IDX))
        outs_p["sc"].append(_pairs_to_state(stp))

        S = _project(ys, pos_s, tm_s, norm_g[l], w_pad, a_qk_g[l], b_qk_g[l], g64)
        r3s = lambda a: a.reshape(bs, ts, a.shape[-1])
        ka = _with_past(cache_a_k[l].reshape(bs, past, 512), r3s(S["akb"]), False)
        va = _with_past(cache_a_v[l].reshape(bs, past, 512), r3s(S["avb"]), False)
        kb = _with_past(cache_b_k[l], r3s(S["kkb"]), True)
        vb = _with_past(cache_b_v[l], r3s(S["vvb"]), True)
        kib = _with_past(cache_b_kidx[l], r3s(S["kib"]), True)
        qb_s = past // LANE
        br_a = _attn_a(r3s(S["aq"]), ka, va, r3s(S["sag"]), slab_a_s, a_lambda[l], a_subln_g[l],
                       tq=ts, qb0=qb_s, lam_init=lam_init)
        br_b = _dsa(r3s(S["bq"]), r3s(S["bqi"]), r3s(S["wif"]), kb, vb, kib, r3s(S["sbg"]), slab_b_s, mslab_s, tri,
                    tq=ts, qb0=qb_s, topk=topk_s)
        br_c, sts = _retention(r3s(S["cq"]), r3s(S["ck"]), r3s(S["cv"]), r3s(S["scg"]),
                               _state_to_pairs(state_c[l].astype(F32)), log_gamma, c_gn_g[l], c=ts)
        ys = _merge(ys, br_a.reshape(bs * ts, W_BR), br_b.reshape(bs * ts, W_BR), br_c.reshape(bs * ts, W_BR),
                    S["smg"], wb, wo, tm_s)
        outs_s["ak"].append(S["ak"].reshape(bs, ts, H_A, 2, DH_A))
        outs_s["av"].append(S["av"].reshape(bs, ts, H_A, DV_A))
        outs_s["bk"].append(S["bk"].reshape(bs, ts, DH_B))
        outs_s["bv"].append(S["bv"].reshape(bs, ts, DH_B))
        outs_s["bki"].append(S["bki"].reshape(bs, ts, D_IDX))
        outs_s["sc"].append(_pairs_to_state(sts).astype(state_c.dtype))

    order = ("ak", "av", "bk", "bv", "bki", "sc")
    return ((yp.reshape(bp, tp, D_MODEL), ys.reshape(bs, ts, D_MODEL))
            + tuple(jnp.stack(outs_p[k]) for k in order)
            + tuple(jnp.stack(outs_s[k]) for k in order))
```

```python
import functools
import math

import jax
import jax.numpy as jnp
from jax import lax
from jax.experimental import pallas as pl
from jax.experimental.pallas import tpu as pltpu

D_MODEL = 1024
CHUNK = 64
EPS = 1e-6
H_A = 4
DH_A = 64
DV_A = 128
H_B = 8
DH_B = 64
H_IDX = 4
D_IDX = 64
TOPK_MAX = 256
H_C = 4
DK_C = 64
DV_C = 128
W_BR = 512
N_BRANCH = 3
NUM_BUCKETS = 32
MAX_DISTANCE = 128
ROPE_BASE = 10000.0

LANE = 128
TQ = 128
NEG = -1e30
LOG2E = math.log2(math.e)
SEARCH_UNROLL = 8
INT_MIN = -(2 ** 31)
F32 = jnp.float32
BF16 = jnp.bfloat16
VMEM_LIMIT = 56 * 1024 * 1024

_SEC = {}
_off = 0
for _name, _n in (("aq", 512), ("ak", 512), ("av", 512), ("ag", 512), ("bq", 512), ("bqi", 256),
                  ("kk", 128), ("vv", 128), ("kiki", 128), ("wi", 128), ("bg", 512),
                  ("cq", 512), ("ck", 512), ("cv", 512), ("cg", 512), ("mg", 3072)):
    _SEC[_name] = (_off, _n)
    _off += _n
D_IN_PAD = _off


def _dot(a, b):
    return jnp.dot(a, b, preferred_element_type=F32)


def _dot_nt(a, b):
    return lax.dot_general(a, b, (((1,), (1,)), ((), ())), preferred_element_type=F32)


def _dot_tn(a, b):
    return lax.dot_general(a, b, (((0,), (0,)), ((), ())), preferred_element_type=F32)


def _proj_kernel(x_ref, g_ref, w_ref, cq_cos_ref, cq_sin_ref, ck_cos_ref, ck_sin_ref,
                 gaq_ref, gak_ref, gbq_ref, gbk_ref, g64_ref,
                 aq_ref, ak_ref, akb_ref, av_ref, avb_ref, sag_ref, bq_ref, bqi_ref,
                 bk_ref, kkb_ref, bv_ref, vvb_ref, bki_ref, kib_ref, wif_ref, sbg_ref,
                 cq_ref, ck_ref, cv_ref, scg_ref, smg_ref):
    x = x_ref[...]
    ms = jnp.mean(x * x, axis=-1, keepdims=True)
    xn = ((x * lax.rsqrt(ms + EPS)) * g_ref[...]).astype(BF16)

    def mm(name, sub=None):
        c0, n = _SEC[name]
        if sub is not None:
            c0, n = c0 + sub[0], sub[1]
        return _dot(xn, w_ref[:, c0:c0 + n])

    def group_norm64(z, gain):
        zz = (z * z).astype(BF16)
        parts = [_dot(zz[:, j * LANE:(j + 1) * LANE], g64_ref[...]) for j in range(z.shape[1] // LANE)]
        ss = parts[0] if len(parts) == 1 else jnp.concatenate(parts, axis=1)
        return (z * lax.rsqrt(ss * (1.0 / 64.0) + EPS)) * gain

    aq_ref[...] = group_norm64(mm("aq"), gaq_ref[...]).astype(BF16)
    ak = group_norm64(mm("ak"), gak_ref[...])
    ak_ref[...] = ak
    akb_ref[...] = ak.astype(BF16)
    av = mm("av")
    av_ref[...] = av
    avb_ref[...] = av.astype(BF16)
    z = mm("ag")
    sag_ref[...] = (z * jax.nn.sigmoid(z)).astype(BF16)
    bq_ref[...] = group_norm64(mm("bq"), gbq_ref[...]).astype(BF16)
    bqi_ref[...] = mm("bqi").astype(BF16)
    kk = group_norm64(mm("kk"), gbk_ref[...])
    bk_ref[...] = kk[:, :DH_B]
    kkb_ref[...] = kk.astype(BF16)
    vv = mm("vv")
    bv_ref[...] = vv[:, :DH_B]
    vvb_ref[...] = vv.astype(BF16)
    kiki = mm("kiki")
    bki_ref[...] = kiki[:, :D_IDX]
    kib_ref[...] = kiki.astype(BF16)
    wif_ref[...] = mm("wi")
    z = mm("bg")
    sbg_ref[...] = (z * jax.nn.sigmoid(z)).astype(BF16)
    z = mm("cq")
    cq_ref[...] = (z[:, :256] * cq_cos_ref[...] + z[:, 256:] * cq_sin_ref[...]).astype(BF16)
    z = mm("ck")
    ck_ref[...] = (z[:, :256] * ck_cos_ref[...] + z[:, 256:] * ck_sin_ref[...]).astype(BF16)
    cv_ref[...] = mm("cv").astype(BF16)
    z = mm("cg")
    scg_ref[...] = (z * jax.nn.sigmoid(z)).astype(BF16)
    for j in range(N_BRANCH * 2):
        z = mm("mg", (j * 512, 512))
        smg_ref[:, j * 512:(j + 1) * 512] = jax.nn.sigmoid(z).astype(BF16)


def _rot_half_cols(w):
    k = w.shape[0]
    w4 = w.reshape(k, -1, DK_C)
    half = DK_C // 2
    return jnp.concatenate([-w4[..., half:], w4[..., :half]], axis=-1).reshape(k, -1)


def _rearrange_w_in(w):
    s = [0]
    for n in (512, 512, 512, 512, 512, 64, 64, 256, 64, 4, 512, 256, 256, 512, 512, 3072):
        s.append(s[-1] + n)
    (aq, ak, av, ag, bq, bk, bv, bqi, bki, bwi, bg, cq, ck, cv, cg, mg) = [w[:, s[i]:s[i + 1]] for i in range(16)]
    wi_pad = jnp.zeros((w.shape[0], LANE - H_IDX), w.dtype)
    cols = [aq, ak, av, ag, bq, bqi, bk, bk, bv, bv, bki, bki, bwi, wi_pad, bg,
            cq, _rot_half_cols(cq), ck, _rot_half_cols(ck), cv, cg, mg]
    out = jnp.concatenate(cols, axis=1).astype(BF16)
    assert out.shape[1] == D_IN_PAD
    return out


def _rope_tables(pos, rows):
    half = DK_C // 2
    inv = ROPE_BASE ** (-jnp.arange(half, dtype=F32) / half)
    ang = pos.astype(F32)[:, None] * inv[None, :]
    cos = jnp.tile(jnp.cos(ang), (1, 2 * H_C))
    sin = jnp.tile(jnp.sin(ang), (1, 2 * H_C))
    reps = max(1, rows // cos.shape[0])
    cos = jnp.tile(cos, (reps, 1))
    sin = jnp.tile(sin, (reps, 1))
    kscale = DK_C ** -0.5
    return cos, sin, cos * kscale, sin * kscale


def _project(x2d, pos, tm, norm_g, w_pad, a_qk_g, b_qk_g, g64):
    n = x2d.shape[0]
    cq_cos, cq_sin, ck_cos, ck_sin = _rope_tables(pos, tm)
    ntab = cq_cos.shape[0] // tm
    gaq = jnp.tile(a_qk_g[0], 8)[None, :] * (LOG2E * DH_A ** -0.5)
    gak = jnp.tile(a_qk_g[1], 8)[None, :]
    gbq = jnp.tile(b_qk_g[0], 8)[None, :] * (LOG2E * DH_B ** -0.5)
    gbk = jnp.tile(b_qk_g[1], 2)[None, :]

    def row(width):
        return pl.BlockSpec((tm, width), lambda i: (i, 0))

    def const(shape, single=False):
        if single:
            return pl.BlockSpec(shape, lambda i: (0, 0), pipeline_mode=pl.Buffered(1))
        return pl.BlockSpec(shape, lambda i: (0, 0))

    tab = pl.BlockSpec((tm, 256), lambda i: (i % ntab, 0))
    outs = [("aq", 512, BF16), ("ak", 512, F32), ("akb", 512, BF16), ("av", 512, F32), ("avb", 512, BF16),
            ("sag", 512, BF16), ("bq", 512, BF16), ("bqi", 256, BF16), ("bk", 64, F32), ("kkb", 128, BF16),
            ("bv", 64, F32), ("vvb", 128, BF16), ("bki", 64, F32), ("kib", 128, BF16), ("wif", 128, F32),
            ("sbg", 512, BF16), ("cq", 256, BF16), ("ck", 256, BF16), ("cv", 512, BF16), ("scg", 512, BF16),
            ("smg", 3072, BF16)]
    res = pl.pallas_call(
        _proj_kernel,
        grid=(n // tm,),
        in_specs=[row(D_MODEL), const((1, D_MODEL)), const((D_MODEL, D_IN_PAD), single=True),
                  tab, tab, tab, tab,
                  const((1, 512)), const((1, 512)), const((1, 512)), const((1, 128)), const((LANE, LANE))],
        out_specs=[row(w) for _, w, _ in outs],
        out_shape=[jax.ShapeDtypeStruct((n, w), dt) for _, w, dt in outs],
        compiler_params=pltpu.CompilerParams(dimension_semantics=("arbitrary",), vmem_limit_bytes=VMEM_LIMIT),
        name="proj",
    )(x2d, norm_g[None, :], w_pad, cq_cos, cq_sin, ck_cos, ck_sin, gaq, gak, gbq, gbk, g64)
    return {name: r for (name, _, _), r in zip(outs, res)}


def _t5_bucket(rel):
    nb = NUM_BUCKETS // 2
    max_exact = nb // 2
    ret = jnp.where(rel > 0, nb, 0)
    n = jnp.abs(rel)
    large = max_exact + (jnp.log(jnp.maximum(n, 1).astype(F32) / max_exact)
                         / math.log(MAX_DISTANCE / max_exact) * (nb - max_exact)).astype(jnp.int32)
    large = jnp.minimum(large, nb - 1)
    return ret + jnp.where(n < max_exact, n, large)


def _bias_slabs(bias_cols):
    i = jnp.arange(TQ, dtype=jnp.int32)[:, None]
    j = jnp.arange(LANE, dtype=jnp.int32)[None, :]
    vis0 = j < (i // CHUNK + 1) * CHUNK

    def lookup(rel):
        bucket = _t5_bucket(rel)
        out = jnp.zeros((bias_cols.shape[1],) + rel.shape, F32)
        for b in range(NUM_BUCKETS):
            out = jnp.where(bucket[None] == b, bias_cols[b].astype(F32)[:, None, None], out)
        return out

    far = lookup(-2 * LANE + j - i)
    c = far[:, :1, :1]
    tiles = [(far - c) * LOG2E, (lookup(-LANE + j - i) - c) * LOG2E,
             jnp.where(vis0[None], (lookup(j - i) - c) * LOG2E, NEG), jnp.full_like(far, NEG)]
    zero = jnp.zeros((TQ, LANE), F32)
    masks = [zero, zero, jnp.where(vis0, 0.0, -jnp.inf).astype(F32), jnp.full((TQ, LANE), -jnp.inf, F32)]
    return jnp.stack(tiles, axis=1), jnp.stack(masks, axis=0)


def _tile_types(qb, nkt):
    return [jnp.clip(kt - qb, -2, 1) + 2 for kt in range(nkt)]


def _variants(nqb, qb0, seg_rows):
    if len(seg_rows) > 1 or nqb == 1:
        assert nqb == 1
        return [(0, 1, tuple(seg_rows), max(0, qb0 - 1))]
    nvar = min(4, nqb)
    assert nqb % nvar == 0
    grp = nqb // nvar
    nkt_total = seg_rows[0] // LANE
    return [(v * grp, (v + 1) * grp, (min(nkt_total, qb0 + (v + 1) * grp) * LANE,), max(0, qb0 + v * grp - 1))
            for v in range(nvar)]


def _run_variants(qi, variants, body):
    if len(variants) == 1:
        body(variants[0][2], variants[0][3])
    else:
        for lo_q, hi_q, rows, n_far in variants:
            pl.when((qi >= lo_q) & (qi < hi_q))(functools.partial(body, rows, n_far))


def _head_stack(qp, lo, kw):
    if kw == LANE:
        zero = jnp.zeros_like(qp)
        return jnp.concatenate([jnp.where(lo, qp, zero), jnp.where(lo, zero, qp)], axis=0)
    return jnp.concatenate([qp[:, :kw], qp[:, kw:]], axis=0)


def _stack2(a, b):
    return jnp.concatenate([a, b], axis=0)


def _softmax_pv(parts, v_blocks):
    s2 = parts[0] if len(parts) == 1 else jnp.concatenate(parts, axis=1)
    m = jnp.max(s2, axis=1, keepdims=True)
    p = jnp.exp2(s2 - m)
    l = jnp.sum(p, axis=1, keepdims=True)
    pb = p.astype(BF16)
    dv = min(v.shape[1] for v in v_blocks)
    o, c0 = None, 0
    for v in v_blocks:
        term = _dot(pb[:, c0:c0 + v.shape[0]], v)[:, :dv]
        o = term if o is None else o + term
        c0 += v.shape[0]
    return o * (1.0 / l)


def _attn_a_kernel(*refs, qb0, variants, lam_init, nseg):
    aq_ref, sag_ref, slab_ref, alam_ref, subg_ref = refs[:5]
    segs = [refs[5 + 2 * i:7 + 2 * i] for i in range(nseg)]
    out_ref = refs[5 + 2 * nseg]
    qi = pl.program_id(1)
    qb = qi + qb0
    lp = alam_ref[...]
    lam = (jnp.exp(jnp.sum(lp[0:1] * lp[1:2], axis=1, keepdims=True))
           - jnp.exp(jnp.sum(lp[2:3] * lp[3:4], axis=1, keepdims=True)) + lam_init)
    lo = lax.broadcasted_iota(jnp.int32, (TQ, LANE), 1) < DH_A

    def body(seg_rows, n_far):
        nkt = sum(r // LANE for r in seg_rows)
        ty = _tile_types(qb, nkt)
        for h in range(H_A):
            hs = slice(h * LANE, (h + 1) * LANE)
            q2 = _head_stack(aq_ref[0, :, hs], lo, LANE)
            parts, kt = [], 0
            for (k_ref, _), rows in zip(segs, seg_rows):
                s = _dot_nt(q2, k_ref[0, :rows, hs])
                n_plain = min(max(n_far - kt, 0), rows // LANE)
                if n_plain:
                    parts.append(s[:, :n_plain * LANE])
                for j in range(n_plain, rows // LANE):
                    b = slab_ref[h, ty[kt + j]]
                    parts.append(s[:, j * LANE:(j + 1) * LANE] + _stack2(b, b))
                kt += rows // LANE
            o2 = _softmax_pv(parts, [v_ref[0, :rows, hs] for (_, v_ref), rows in zip(segs, seg_rows)])
            o = o2[:TQ] - lam * o2[TQ:]
            o = o * lax.rsqrt(jnp.mean(o * o, axis=1, keepdims=True) + EPS) * subg_ref[...] * (1.0 - lam_init)
            out_ref[0, :, hs] = (o * sag_ref[0, :, hs].astype(F32)).astype(BF16)

    _run_variants(qi, variants, body)


def _attn_a(aq, sag, slab, a_lambda, subln_g, segs, *, qb0, lam_init):
    b, t, _ = aq.shape
    nqb = t // TQ
    seg_rows = [k.shape[1] for k, _ in segs]
    variants = _variants(nqb, qb0, seg_rows)
    kern = functools.partial(_attn_a_kernel, qb0=qb0, variants=variants, lam_init=lam_init, nseg=len(segs))
    qspec = pl.BlockSpec((1, TQ, 512), lambda bi, qi: (bi, qi, 0))
    seg_specs, seg_args = [], []
    for k, v in segs:
        spec = pl.BlockSpec((1, k.shape[1], 512), lambda bi, qi: (bi, 0, 0))
        seg_specs += [spec, spec]
        seg_args += [k, v]
    return pl.pallas_call(
        kern,
        grid=(b, nqb),
        in_specs=[qspec, qspec,
                  pl.BlockSpec(slab.shape, lambda bi, qi: (0, 0, 0, 0)),
                  pl.BlockSpec((4, DH_A), lambda bi, qi: (0, 0)),
                  pl.BlockSpec((1, DV_A), lambda bi, qi: (0, 0))] + seg_specs,
        out_specs=qspec,
        out_shape=jax.ShapeDtypeStruct((b, t, 512), BF16),
        compiler_params=pltpu.CompilerParams(dimension_semantics=("arbitrary", "arbitrary"),
                                             vmem_limit_bytes=VMEM_LIMIT),
        name="attn_a",
    )(aq, sag, slab, a_lambda, subln_g[None, :], *seg_args)


def _dsa_kernel(*refs, qb0, variants, topk, nseg):
    bq_ref, bqi_ref, wif_ref, sbg_ref, slab_ref, mslab_ref, tri_ref = refs[:7]
    segs = [refs[7 + 3 * i:10 + 3 * i] for i in range(nseg)]
    out_ref, keys_ref = refs[7 + 3 * nseg:]
    qi = pl.program_id(1)
    qb = qi + qb0
    lo = lax.broadcasted_iota(jnp.int32, (TQ, LANE), 1) < DH_B
    kf = float(topk)

    def body(seg_rows, n_far):
        nkt = sum(r // LANE for r in seg_rows)
        ty = _tile_types(qb, nkt)

        wi = wif_ref[0]
        kt0 = 0
        for (_, _, ki_ref), rows in zip(segs, seg_rows):
            ki = ki_ref[0, :rows, :].astype(BF16)
            qis = jnp.concatenate([_head_stack(bqi_ref[0, :, hp * LANE:(hp + 1) * LANE], lo, ki.shape[1])
                                   for hp in range(H_IDX // 2)], axis=0)
            r = jnp.maximum(_dot_nt(qis, ki), 0.0)
            score = wi[:, 0:1] * r[0:TQ]
            for h in range(1, H_IDX):
                score = score + wi[:, h:h + 1] * r[h * TQ:(h + 1) * TQ]
            score = score + jnp.concatenate([mslab_ref[ty[kt0 + j]] for j in range(rows // LANE)], axis=1)
            bits = pltpu.bitcast(score, jnp.int32)
            keys_ref[:, kt0 * LANE:kt0 * LANE + rows] = bits ^ ((bits >> 31) & jnp.int32(0x7FFFFFFF))
            kt0 += rows // LANE

        def count(pred):
            acc = None
            for kt in range(nkt):
                ind = jnp.where(pred(keys_ref[:, kt * LANE:(kt + 1) * LANE]), 1.0, 0.0)
                acc = ind if acc is None else acc + ind
            return jnp.sum(acc, axis=1, keepdims=True)

        def search(i, t):
            cand = t + lax.shift_left(jnp.int32(1), 31 - i)
            return jnp.where(count(lambda x: x >= cand) >= kf, cand, t)

        t = lax.fori_loop(0, 32, search, jnp.full((TQ, 1), INT_MIN, jnp.int32), unroll=SEARCH_UNROLL)

        need = kf - count(lambda x: x > t)
        run = jnp.zeros((TQ, 1), F32)
        negm = []
        for kt in range(nkt):
            kt_keys = keys_ref[:, kt * LANE:(kt + 1) * LANE]
            eq = jnp.where(kt_keys == t, 1.0, 0.0)
            rank = _dot(eq.astype(BF16), tri_ref[...]) + run
            tie = jnp.where(rank <= need, eq, 0.0)
            negm.append(jnp.where(kt_keys > t, 0.0, (1.0 - tie) * NEG))
            run = run + jnp.sum(eq, axis=1, keepdims=True)
        far2 = [_stack2(negm[kt], negm[kt]) for kt in range(n_far)]

        for hp in range(H_B // 2):
            hs = slice(hp * LANE, (hp + 1) * LANE)
            qp = bq_ref[0, :, hs]
            parts, vs, kt = [], [], 0
            for (k_ref, v_ref, _), rows in zip(segs, seg_rows):
                k = k_ref[0, :rows, :].astype(BF16)
                s = _dot_nt(_head_stack(qp, lo, k.shape[1]), k)
                for j in range(rows // LANE):
                    if kt < n_far:
                        nb2 = far2[kt]
                    else:
                        nb2 = _stack2(negm[kt] + slab_ref[2 * hp, ty[kt]], negm[kt] + slab_ref[2 * hp + 1, ty[kt]])
                    parts.append(s[:, j * LANE:(j + 1) * LANE] + nb2)
                    kt += 1
                vs.append(v_ref[0, :rows, :].astype(BF16))
            o2 = _softmax_pv(parts, vs)
            if o2.shape[1] == LANE:
                o = jnp.where(lo, o2[:TQ], o2[TQ:])
            else:
                o = jnp.concatenate([o2[:TQ], o2[TQ:]], axis=1)
            out_ref[0, :, hs] = (o * sbg_ref[0, :, hs].astype(F32)).astype(BF16)

    _run_variants(qi, variants, body)


def _dsa(bq, bqi, wif, sbg, slab, mslab, tri, segs, *, qb0, topk):
    b, t, _ = bq.shape
    nqb = t // TQ
    seg_rows = [s[0].shape[-2] for s in segs]
    variants = _variants(nqb, qb0, seg_rows)
    kern = functools.partial(_dsa_kernel, qb0=qb0, variants=variants, topk=topk, nseg=len(segs))

    def qspec(w):
        return pl.BlockSpec((1, TQ, w), lambda bi, qi: (bi, qi, 0))

    seg_specs, seg_args = [], []
    for k, v, ki, layer in segs:
        for a in (k, v, ki):
            if layer is None:
                seg_specs.append(pl.BlockSpec((1,) + a.shape[1:], lambda bi, qi: (bi, 0, 0)))
            else:
                seg_specs.append(pl.BlockSpec((None, 1) + a.shape[2:], lambda bi, qi, layer=layer: (layer, bi, 0, 0)))
            seg_args.append(a)
    return pl.pallas_call(
        kern,
        grid=(b, nqb),
        in_specs=[qspec(512), qspec(256), qspec(LANE), qspec(512),
                  pl.BlockSpec(slab.shape, lambda bi, qi: (0, 0, 0, 0)),
                  pl.BlockSpec(mslab.shape, lambda bi, qi: (0, 0, 0)),
                  pl.BlockSpec((LANE, LANE), lambda bi, qi: (0, 0))] + seg_specs,
        out_specs=qspec(512),
        out_shape=jax.ShapeDtypeStruct((b, t, 512), BF16),
        scratch_shapes=[pltpu.VMEM((TQ, sum(seg_rows)), jnp.int32)],
        compiler_params=pltpu.CompilerParams(dimension_semantics=("arbitrary", "arbitrary"),
                                             vmem_limit_bytes=VMEM_LIMIT),
        name="dsa",
    )(bq, bqi, wif, sbg, slab, mslab, tri, *seg_args)


def _ret_kernel(cq_ref, ck_ref, cv_ref, scg_ref, st_ref, dmat_ref, dq_ref, dk_ref, gc_ref, bd_ref, gn_ref,
                out_ref, sto_ref, *, c, nchunks):
    lane = lax.broadcasted_iota(jnp.int32, (c, LANE), 1)
    lo = lane < DK_C
    for p in range(H_C // 2):
        st = st_ref[0, p]
        for ci in range(nchunks):
            rows = slice(ci * c, (ci + 1) * c)
            q = cq_ref[0, rows, p * LANE:(p + 1) * LANE]
            k = ck_ref[0, rows, p * LANE:(p + 1) * LANE]
            v = cv_ref[0, rows, p * 2 * DV_C:(p + 1) * 2 * DV_C]
            qd = (q.astype(F32) * dq_ref[p]).astype(BF16)
            cross = _dot(qd, st.astype(BF16))
            zero = jnp.zeros_like(q)
            for j in range(2):
                h = 2 * p + j
                qm = jnp.where(lo, q, zero) if j == 0 else jnp.where(lo, zero, q)
                a = (_dot_nt(qm, k) * dmat_ref[h]).astype(BF16)
                o = cross[:, j * DV_C:(j + 1) * DV_C] + _dot(a, v[:, j * DV_C:(j + 1) * DV_C])
                o = o * lax.rsqrt(jnp.mean(o * o, axis=1, keepdims=True) + EPS) * gn_ref[...]
                gate = scg_ref[0, rows, h * DV_C:(h + 1) * DV_C].astype(F32)
                out_ref[0, rows, h * DV_C:(h + 1) * DV_C] = (o * gate).astype(BF16)
            kd = (k.astype(F32) * dk_ref[p]).astype(BF16)
            st = (gc_ref[p] * st + _dot_tn(kd, v)) * bd_ref[...]
        sto_ref[0, p] = st


def _retention(cq, ck, cv, scg, state_pairs, log_gamma, gn_g, *, c):
    b, t, _ = cq.shape
    nchunks = t // c
    n = jnp.arange(c, dtype=F32)
    diff = n[:, None] - n[None, :]
    dmat = jnp.where(diff >= 0, jnp.exp(log_gamma[:, None, None] * jnp.maximum(diff, 0.0)[None]), 0.0)
    decay_q = jnp.exp((n[:, None] + 1.0) * log_gamma[None, :])
    decay_k = jnp.exp((c - 1.0 - n)[:, None] * log_gamma[None, :])
    decay_c = jnp.exp(c * log_gamma)

    def lanes(tab):
        return jnp.transpose(jnp.repeat(tab, DK_C, axis=1).reshape(c, H_C // 2, LANE), (1, 0, 2))

    gc = jnp.broadcast_to(jnp.repeat(decay_c, DK_C).reshape(H_C // 2, LANE, 1), (H_C // 2, LANE, 2 * DV_C))
    bd = (jnp.arange(LANE)[:, None] // DK_C == jnp.arange(2 * DV_C)[None, :] // DV_C).astype(F32)

    def full(a):
        return pl.BlockSpec(a.shape, lambda bi: (0,) * a.ndim)

    def tok(w):
        return pl.BlockSpec((1, t, w), lambda bi: (bi, 0, 0))

    stspec = pl.BlockSpec((1, H_C // 2, LANE, 2 * DV_C), lambda bi: (bi, 0, 0, 0))
    dq, dk, gnv = lanes(decay_q), lanes(decay_k), gn_g[None, :]
    return pl.pallas_call(
        functools.partial(_ret_kernel, c=c, nchunks=nchunks),
        grid=(b,),
        in_specs=[tok(256), tok(256), tok(512), tok(512), stspec,
                  full(dmat), full(dq), full(dk), full(gc), full(bd), full(gnv)],
        out_specs=[tok(512), stspec],
        out_shape=[jax.ShapeDtypeStruct((b, t, 512), BF16),
                   jax.ShapeDtypeStruct((b, H_C // 2, LANE, 2 * DV_C), F32)],
        compiler_params=pltpu.CompilerParams(dimension_semantics=("arbitrary",), vmem_limit_bytes=VMEM_LIMIT),
        name="retention",
    )(cq, ck, cv, scg, state_pairs, dmat, dq, dk, gc, bd, gnv)


def _state_to_pairs(st):
    b = st.shape[0]
    s4 = st.reshape(b, H_C // 2, 2, DK_C, DV_C)
    z = jnp.zeros_like(s4[:, :, 0])
    top = jnp.concatenate([s4[:, :, 0], z], axis=-1)
    bot = jnp.concatenate([z, s4[:, :, 1]], axis=-1)
    return jnp.concatenate([top, bot], axis=-2)


def _pairs_to_state(sp):
    b = sp.shape[0]
    s0 = sp[:, :, :DK_C, :DV_C]
    s1 = sp[:, :, DK_C:, DV_C:]
    return jnp.stack([s0, s1], axis=2).reshape(b, H_C, DK_C, DV_C)


def _merge_kernel(x_ref, ba_ref, bb_ref, bc_ref, smg_ref, wb_ref, wo_ref, y_ref):
    m = None
    for n, br in enumerate((ba_ref, bb_ref, bc_ref)):
        proj = _dot(br[...], wb_ref[n])
        term = smg_ref[:, n * D_MODEL:(n + 1) * D_MODEL].astype(F32) * proj
        m = term if m is None else m + term
    y_ref[...] = x_ref[...] + _dot(m.astype(BF16), wo_ref[...])


def _merge(x2d, br_a, br_b, br_c, smg, wb, wo, tm):
    n = x2d.shape[0]

    def row(w):
        return pl.BlockSpec((tm, w), lambda i: (i, 0))

    return pl.pallas_call(
        _merge_kernel,
        grid=(n // tm,),
        in_specs=[row(D_MODEL), row(W_BR), row(W_BR), row(W_BR), row(N_BRANCH * D_MODEL),
                  pl.BlockSpec((N_BRANCH, W_BR, D_MODEL), lambda i: (0, 0, 0)),
                  pl.BlockSpec((D_MODEL, D_MODEL), lambda i: (0, 0))],
        out_specs=row(D_MODEL),
        out_shape=jax.ShapeDtypeStruct((n, D_MODEL), F32),
        compiler_params=pltpu.CompilerParams(dimension_semantics=("arbitrary",), vmem_limit_bytes=VMEM_LIMIT),
        name="merge",
    )(x2d, br_a, br_b, br_c, smg, wb, wo)


def _pad_rows(a, rows):
    return jnp.pad(a, ((0, 0), (0, rows - a.shape[1]), (0, 0)))


def kernel(x_prompt, x_sample, cache_a_k, cache_a_v, cache_b_k, cache_b_v, cache_b_kidx, state_c, rel_bias,
           norm_g, w_in, a_qk_g, a_lambda, a_subln_g, b_qk_g, c_gn_g, w_branch, w_out):
    bp, tp, _ = x_prompt.shape
    bs, ts, _ = x_sample.shape
    depth = w_in.shape[0]
    past = cache_a_k.shape[2]
    assert tp % TQ == 0 and ts == CHUNK and past % LANE == 0
    pos_p = jnp.arange(tp, dtype=jnp.int32)
    pos_s = past + jnp.arange(ts, dtype=jnp.int32)
    topk_p = min(TOPK_MAX, tp // 4)
    topk_s = min(TOPK_MAX, (past + ts) // 4)
    log_gamma = jnp.log(1.0 - 2.0 ** (-5.0 - jnp.arange(H_C, dtype=F32)))

    slab_a, mslab = _bias_slabs(rel_bias[:, :H_A])
    slab_b, _ = _bias_slabs(rel_bias[:, H_A:])
    lane_i = jnp.arange(LANE)
    g64 = (lane_i[:, None] // 64 == lane_i[None, :] // 64).astype(BF16)
    tri = (lane_i[:, None] <= lane_i[None, :]).astype(BF16)
    c_prompt = 256 if tp % 256 == 0 else CHUNK
    tm_p = 256
    tm_s = 256 if (bs * ts) % 256 == 0 else ts
    qb_s = past // LANE

    yp = x_prompt.reshape(bp * tp, D_MODEL)
    ys = x_sample.reshape(bs * ts, D_MODEL)
    outs_p = {k: [] for k in ("ak", "av", "bk", "bv", "bki", "sc")}
    outs_s = {k: [] for k in ("ak", "av", "bk", "bv", "bki", "sc")}
    for l in range(depth):
        lam_init = 0.8 - 0.6 * math.exp(-0.3 * l)
        w_pad = _rearrange_w_in(w_in[l])
        wb = w_branch[l].astype(BF16)
        wo = w_out[l].astype(BF16)

        P = _project(yp, pos_p, tm_p, norm_g[l], w_pad, a_qk_g[l], b_qk_g[l], g64)
        r3 = lambda a: a.reshape(bp, tp, a.shape[-1])
        br_a = _attn_a(r3(P["aq"]), r3(P["sag"]), slab_a, a_lambda[l], a_subln_g[l],
                       [(r3(P["akb"]), r3(P["avb"]))], qb0=0, lam_init=lam_init)
        br_b = _dsa(r3(P["bq"]), r3(P["bqi"]), r3(P["wif"]), r3(P["sbg"]), slab_b, mslab, tri,
                    [(r3(P["kkb"]), r3(P["vvb"]), r3(P["kib"]), None)], qb0=0, topk=topk_p)
        st0 = jnp.zeros((bp, H_C // 2, LANE, 2 * DV_C), F32)
        br_c, stp = _retention(r3(P["cq"]), r3(P["ck"]), r3(P["cv"]), r3(P["scg"]), st0, log_gamma, c_gn_g[l],
                               c=c_prompt)
        yp = _merge(yp, br_a.reshape(bp * tp, W_BR), br_b.reshape(bp * tp, W_BR), br_c.reshape(bp * tp, W_BR),
                    P["smg"], wb, wo, tm_p)
        outs_p["ak"].append(P["ak"].reshape(bp, tp, H_A, 2, DH_A))
        outs_p["av"].append(P["av"].reshape(bp, tp, H_A, DV_A))
        outs_p["bk"].append(P["bk"].reshape(bp, tp, DH_B))
        outs_p["bv"].append(P["bv"].reshape(bp, tp, DH_B))
        outs_p["bki"].append(P["bki"].reshape(bp, tp, D_IDX))
        outs_p["sc"].append(_pairs_to_state(stp))

        S = _project(ys, pos_s, tm_s, norm_g[l], w_pad, a_qk_g[l], b_qk_g[l], g64)
        r3s = lambda a: a.reshape(bs, ts, a.shape[-1])
        rq = lambda a: _pad_rows(r3s(a), TQ)
        ka = cache_a_k[l].reshape(bs, past, 512).astype(BF16)
        va = cache_a_v[l].reshape(bs, past, 512).astype(BF16)
        br_a = _attn_a(rq(S["aq"]), rq(S["sag"]), slab_a, a_lambda[l], a_subln_g[l],
                       [(ka, va), (rq(S["akb"]), rq(S["avb"]))], qb0=qb_s, lam_init=lam_init)[:, :ts]
        br_b = _dsa(rq(S["bq"]), rq(S["bqi"]), rq(S["wif"]), rq(S["sbg"]), slab_b, mslab, tri,
                    [(cache_b_k, cache_b_v, cache_b_kidx, l), (rq(S["kkb"]), rq(S["vvb"]), rq(S["kib"]), None)],
                    qb0=qb_s, topk=topk_s)[:, :ts]
        br_c, sts = _retention(r3s(S["cq"]), r3s(S["ck"]), r3s(S["cv"]), r3s(S["scg"]),
                               _state_to_pairs(state_c[l].astype(F32)), log_gamma, c_gn_g[l], c=ts)
        ys = _merge(ys, br_a.reshape(bs * ts, W_BR), br_b.reshape(bs * ts, W_BR), br_c.reshape(bs * ts, W_BR),
                    S["smg"], wb, wo, tm_s)
        outs_s["ak"].append(S["ak"].reshape(bs, ts, H_A, 2, DH_A))
        outs_s["av"].append(S["av"].reshape(bs, ts, H_A, DV_A))
        outs_s["bk"].append(S["bk"].reshape(bs, ts, DH_B))
        outs_s["bv"].append(S["bv"].reshape(bs, ts, DH_B))
        outs_s["bki"].append(S["bki"].reshape(bs, ts, D_IDX))
        outs_s["sc"].append(_pairs_to_state(sts).astype(state_c.dtype))

    order = ("ak", "av", "bk", "bv", "bki", "sc")
    return ((yp.reshape(bp, tp, D_MODEL), ys.reshape(bs, ts, D_MODEL))
            + tuple(jnp.stack(outs_p[k]) for k in order)
            + tuple(jnp.stack(outs_s[k]) for k in order))
```

```python
import functools
import math

import jax
import jax.numpy as jnp
from jax import lax
from jax.experimental import pallas as pl
from jax.experimental.pallas import tpu as pltpu

D_MODEL = 1024
CHUNK = 64
EPS = 1e-6
H_A = 4
DH_A = 64
DV_A = 128
H_B = 8
DH_B = 64
H_IDX = 4
D_IDX = 64
TOPK_MAX = 256
H_C = 4
DK_C = 64
DV_C = 128
W_BR = 512
N_BRANCH = 3
NUM_BUCKETS = 32
MAX_DISTANCE = 128
ROPE_BASE = 10000.0

LANE = 128
TQ = 128
NEG = -1e30
LOG2E = math.log2(math.e)
F32 = jnp.float32
BF16 = jnp.bfloat16
VMEM_LIMIT = 56 * 1024 * 1024

_SEC = {}
_off = 0
for _name, _n in (("aq", 512), ("ak", 512), ("av", 512), ("ag", 512), ("bq", 512), ("bqi", 256),
                  ("kk", 128), ("vv", 128), ("kiki", 128), ("wi", 128), ("bg", 512),
                  ("cq", 512), ("ck", 512), ("cv", 512), ("cg", 512), ("mg", 3072)):
    _SEC[_name] = (_off, _n)
    _off += _n
D_IN_PAD = _off


def _dot(a, b):
    return jnp.dot(a, b, preferred_element_type=F32)


def _dot_nt(a, b):
    return lax.dot_general(a, b, (((1,), (1,)), ((), ())), preferred_element_type=F32)


def _dot_tn(a, b):
    return lax.dot_general(a, b, (((0,), (0,)), ((), ())), preferred_element_type=F32)


def _proj_kernel(*refs, n_alias):
    (x_ref, g_ref, w_ref, cq_cos_ref, cq_sin_ref, ck_cos_ref, ck_sin_ref,
     gaq_ref, gak_ref, gbq_ref, gbk_ref, g64_ref) = refs[:12]
    (aq_ref, ak_ref, akb_ref, av_ref, avb_ref, sag_ref, bq_ref, bqi_ref,
     bk_ref, kkb_ref, bv_ref, vvb_ref, bki_ref, kib_ref, wif_ref, sbg_ref,
     cq_ref, ck_ref, cv_ref, scg_ref, smg_ref) = refs[12 + n_alias:]
    x = x_ref[...]
    ms = jnp.mean(x * x, axis=-1, keepdims=True)
    xn = ((x * lax.rsqrt(ms + EPS)) * g_ref[...]).astype(BF16)

    def mm(name, sub=None):
        c0, n = _SEC[name]
        if sub is not None:
            c0, n = c0 + sub[0], sub[1]
        return _dot(xn, w_ref[:, c0:c0 + n])

    def group_norm64(z, gain):
        zz = (z * z).astype(BF16)
        parts = [_dot(zz[:, j * LANE:(j + 1) * LANE], g64_ref[...]) for j in range(z.shape[1] // LANE)]
        ss = parts[0] if len(parts) == 1 else jnp.concatenate(parts, axis=1)
        return (z * lax.rsqrt(ss * (1.0 / 64.0) + EPS)) * gain

    aq_ref[...] = group_norm64(mm("aq"), gaq_ref[...]).astype(BF16)
    ak = group_norm64(mm("ak"), gak_ref[...])
    ak_ref[...] = ak
    akb_ref[...] = ak.astype(BF16)
    av = mm("av")
    av_ref[...] = av
    avb_ref[...] = av.astype(BF16)
    z = mm("ag")
    sag_ref[...] = (z * jax.nn.sigmoid(z)).astype(BF16)
    bq_ref[...] = group_norm64(mm("bq"), gbq_ref[...]).astype(BF16)
    bqi_ref[...] = mm("bqi").astype(BF16)
    kk = group_norm64(mm("kk"), gbk_ref[...])
    bk_ref[...] = kk[:, :DH_B]
    kkb_ref[...] = kk.astype(BF16)
    vv = mm("vv")
    bv_ref[...] = vv[:, :DH_B]
    vvb_ref[...] = vv.astype(BF16)
    kiki = mm("kiki")
    bki_ref[...] = kiki[:, :D_IDX]
    kib_ref[...] = kiki.astype(BF16)
    wif_ref[...] = mm("wi")
    z = mm("bg")
    sbg_ref[...] = (z * jax.nn.sigmoid(z)).astype(BF16)
    z = mm("cq")
    cq_ref[...] = (z[:, :256] * cq_cos_ref[...] + z[:, 256:] * cq_sin_ref[...]).astype(BF16)
    z = mm("ck")
    ck_ref[...] = (z[:, :256] * ck_cos_ref[...] + z[:, 256:] * ck_sin_ref[...]).astype(BF16)
    cv_ref[...] = mm("cv").astype(BF16)
    z = mm("cg")
    scg_ref[...] = (z * jax.nn.sigmoid(z)).astype(BF16)
    for j in range(N_BRANCH * 2):
        z = mm("mg", (j * 512, 512))
        smg_ref[:, j * 512:(j + 1) * 512] = jax.nn.sigmoid(z).astype(BF16)


def _rot_half_cols(w):
    k = w.shape[0]
    w4 = w.reshape(k, -1, DK_C)
    half = DK_C // 2
    return jnp.concatenate([-w4[..., half:], w4[..., :half]], axis=-1).reshape(k, -1)


def _rearrange_w_in(w):
    wb = w.astype(BF16)
    head = wb[:, :3008]
    tail = wb[:, 3012:]
    bk, bv, bqi, bki = head[:, 2560:2624], head[:, 2624:2688], head[:, 2688:2944], head[:, 2944:3008]
    wi_pad = jnp.zeros((w.shape[0], LANE - H_IDX), BF16)
    small = jnp.concatenate([bqi, bk, bk, bv, bv, bki, bki, wb[:, 3008:3012], wi_pad], axis=1)
    cq, ck = tail[:, 512:768], tail[:, 768:1024]
    mid = jnp.concatenate([tail[:, :512], cq, _rot_half_cols(cq), ck, _rot_half_cols(ck)], axis=1)
    out = jnp.concatenate([head[:, :2560], small, mid, tail[:, 1024:]], axis=1)
    assert out.shape[1] == D_IN_PAD
    return out


def _rope_tables(pos, rows):
    half = DK_C // 2
    inv = ROPE_BASE ** (-jnp.arange(half, dtype=F32) / half)
    ang = pos.astype(F32)[:, None] * inv[None, :]
    cos = jnp.tile(jnp.cos(ang), (1, 2 * H_C))
    sin = jnp.tile(jnp.sin(ang), (1, 2 * H_C))
    reps = max(1, rows // cos.shape[0])
    cos = jnp.tile(cos, (reps, 1))
    sin = jnp.tile(sin, (reps, 1))
    kscale = DK_C ** -0.5
    return cos, sin, cos * kscale, sin * kscale


_STACKED = ("ak", "av", "bk", "bv", "bki")


def _project(x2d, pos, tm, norm_g, w_pad, a_qk_g, b_qk_g, g64, *, layer, depth, prev):
    n = x2d.shape[0]
    nblk = n // tm
    cq_cos, cq_sin, ck_cos, ck_sin = _rope_tables(pos, tm)
    ntab = cq_cos.shape[0] // tm
    gaq = jnp.tile(a_qk_g[0], 8)[None, :] * (LOG2E * DH_A ** -0.5)
    gak = jnp.tile(a_qk_g[1], 8)[None, :]
    gbq = jnp.tile(b_qk_g[0], 8)[None, :] * (LOG2E * DH_B ** -0.5)
    gbk = jnp.tile(b_qk_g[1], 2)[None, :]

    def row(width):
        return pl.BlockSpec((tm, width), lambda i: (i, 0))

    def stacked_row(width):
        return pl.BlockSpec((tm, width), lambda i: (i + layer * nblk, 0))

    def const(shape, single=False):
        if single:
            return pl.BlockSpec(shape, lambda i: (0, 0), pipeline_mode=pl.Buffered(1))
        return pl.BlockSpec(shape, lambda i: (0, 0))

    tab = pl.BlockSpec((tm, 256), lambda i: (i % ntab, 0))
    outs = [("aq", 512, BF16), ("ak", 512, F32), ("akb", 512, BF16), ("av", 512, F32), ("avb", 512, BF16),
            ("sag", 512, BF16), ("bq", 512, BF16), ("bqi", 256, BF16), ("bk", 64, F32), ("kkb", 128, BF16),
            ("bv", 64, F32), ("vvb", 128, BF16), ("bki", 64, F32), ("kib", 128, BF16), ("wif", 128, F32),
            ("sbg", 512, BF16), ("cq", 256, BF16), ("ck", 256, BF16), ("cv", 512, BF16), ("scg", 512, BF16),
            ("smg", 3072, BF16)]
    names = [name for name, _, _ in outs]
    alias_args = [] if prev is None else [prev[name] for name in _STACKED]
    n_in = 12
    res = pl.pallas_call(
        functools.partial(_proj_kernel, n_alias=len(alias_args)),
        grid=(nblk,),
        in_specs=[row(D_MODEL), const((1, D_MODEL)), const((D_MODEL, D_IN_PAD), single=True),
                  tab, tab, tab, tab,
                  const((1, 512)), const((1, 512)), const((1, 512)), const((1, 128)), const((LANE, LANE))]
                 + [pl.BlockSpec(memory_space=pl.ANY)] * len(alias_args),
        out_specs=[stacked_row(w) if name in _STACKED else row(w) for name, w, _ in outs],
        out_shape=[jax.ShapeDtypeStruct((depth * n if name in _STACKED else n, w), dt) for name, w, dt in outs],
        input_output_aliases={n_in + j: names.index(name) for j, name in enumerate(_STACKED)} if alias_args else {},
        compiler_params=pltpu.CompilerParams(dimension_semantics=("arbitrary",), vmem_limit_bytes=VMEM_LIMIT),
        name="proj",
    )(x2d, norm_g[None, :], w_pad, cq_cos, cq_sin, ck_cos, ck_sin, gaq, gak, gbq, gbk, g64, *alias_args)
    return dict(zip(names, res))


def _t5_bucket(rel):
    nb = NUM_BUCKETS // 2
    max_exact = nb // 2
    ret = jnp.where(rel > 0, nb, 0)
    n = jnp.abs(rel)
    large = max_exact + (jnp.log(jnp.maximum(n, 1).astype(F32) / max_exact)
                         / math.log(MAX_DISTANCE / max_exact) * (nb - max_exact)).astype(jnp.int32)
    large = jnp.minimum(large, nb - 1)
    return ret + jnp.where(n < max_exact, n, large)


def _bias_slabs(bias_cols):
    i = jnp.arange(TQ, dtype=jnp.int32)[:, None]
    j = jnp.arange(LANE, dtype=jnp.int32)[None, :]
    vis0 = j < (i // CHUNK + 1) * CHUNK

    def lookup(rel):
        bucket = _t5_bucket(rel)
        out = jnp.zeros((bias_cols.shape[1],) + rel.shape, F32)
        for b in range(NUM_BUCKETS):
            out = jnp.where(bucket[None] == b, bias_cols[b].astype(F32)[:, None, None], out)
        return out

    far = lookup(-2 * LANE + j - i)
    c = far[:, :1, :1]
    tiles = [(far - c) * LOG2E, (lookup(-LANE + j - i) - c) * LOG2E,
             jnp.where(vis0[None], (lookup(j - i) - c) * LOG2E, NEG), jnp.full_like(far, NEG)]
    zero = jnp.zeros((TQ, LANE), F32)
    masks = [zero, zero, jnp.where(vis0, 0.0, -jnp.inf).astype(F32), jnp.full((TQ, LANE), -jnp.inf, F32)]
    return jnp.stack(tiles, axis=1), jnp.stack(masks, axis=0)


def _tile_types(qb, nkt):
    return [jnp.clip(kt - qb, -2, 1) + 2 for kt in range(nkt)]


def _variants(nqb, qb0, seg_rows):
    if len(seg_rows) > 1 or nqb == 1:
        assert nqb == 1
        return [(0, 1, tuple(seg_rows), max(0, qb0 - 1))]
    nvar = min(4, nqb)
    assert nqb % nvar == 0
    grp = nqb // nvar
    nkt_total = seg_rows[0] // LANE
    return [(v * grp, (v + 1) * grp, (min(nkt_total, qb0 + (v + 1) * grp) * LANE,), max(0, qb0 + v * grp - 1))
            for v in range(nvar)]


def _run_variants(qi, variants, body):
    if len(variants) == 1:
        body(variants[0][2], variants[0][3])
    else:
        for lo_q, hi_q, rows, n_far in variants:
            pl.when((qi >= lo_q) & (qi < hi_q))(functools.partial(body, rows, n_far))


def _head_stack(qp, lo, kw):
    if kw == LANE:
        zero = jnp.zeros_like(qp)
        return jnp.concatenate([jnp.where(lo, qp, zero), jnp.where(lo, zero, qp)], axis=0)
    return jnp.concatenate([qp[:, :kw], qp[:, kw:]], axis=0)


def _stack2(a, b):
    return jnp.concatenate([a, b], axis=0)


def _softmax_pv(parts, v_blocks):
    s2 = parts[0] if len(parts) == 1 else jnp.concatenate(parts, axis=1)
    m = jnp.max(s2, axis=1, keepdims=True)
    p = jnp.exp2(s2 - m)
    l = jnp.sum(p, axis=1, keepdims=True)
    pb = p.astype(BF16)
    dv = min(v.shape[1] for v in v_blocks)
    o, c0 = None, 0
    for v in v_blocks:
        term = _dot(pb[:, c0:c0 + v.shape[0]], v)[:, :dv]
        o = term if o is None else o + term
        c0 += v.shape[0]
    return o * (1.0 / l)


def _attn_a_kernel(*refs, tq, qb0, variants, lam_init, nseg):
    aq_ref, sag_ref, slab_ref, alam_ref, subg_ref = refs[:5]
    segs = [refs[5 + 2 * i:7 + 2 * i] for i in range(nseg)]
    out_ref = refs[5 + 2 * nseg]
    qi = pl.program_id(1)
    qb = qi + qb0
    lp = alam_ref[...]
    lam = (jnp.exp(jnp.sum(lp[0:1] * lp[1:2], axis=1, keepdims=True))
           - jnp.exp(jnp.sum(lp[2:3] * lp[3:4], axis=1, keepdims=True)) + lam_init)
    lo = lax.broadcasted_iota(jnp.int32, (tq, LANE), 1) < DH_A

    def body(seg_rows, n_far):
        nkt = sum(r // LANE for r in seg_rows)
        ty = _tile_types(qb, nkt)
        for h in range(H_A):
            hs = slice(h * LANE, (h + 1) * LANE)
            q2 = _head_stack(aq_ref[0, :, hs], lo, LANE)
            parts, kt = [], 0
            for (k_ref, _), rows in zip(segs, seg_rows):
                s = _dot_nt(q2, k_ref[0, :rows, hs])
                n_plain = min(max(n_far - kt, 0), rows // LANE)
                if n_plain:
                    parts.append(s[:, :n_plain * LANE])
                for j in range(n_plain, rows // LANE):
                    b = slab_ref[h, ty[kt + j], :tq]
                    parts.append(s[:, j * LANE:(j + 1) * LANE] + _stack2(b, b))
                kt += rows // LANE
            o2 = _softmax_pv(parts, [v_ref[0, :rows, hs] for (_, v_ref), rows in zip(segs, seg_rows)])
            o = o2[:tq] - lam * o2[tq:]
            o = o * lax.rsqrt(jnp.mean(o * o, axis=1, keepdims=True) + EPS) * subg_ref[...] * (1.0 - lam_init)
            out_ref[0, :, hs] = (o * sag_ref[0, :, hs].astype(F32)).astype(BF16)

    _run_variants(qi, variants, body)


def _attn_a(aq, sag, slab, a_lambda, subln_g, segs, *, tq, qb0, lam_init):
    b, t, _ = aq.shape
    nqb = t // tq
    seg_rows = [k.shape[1] for k, _ in segs]
    variants = _variants(nqb, qb0, seg_rows)
    kern = functools.partial(_attn_a_kernel, tq=tq, qb0=qb0, variants=variants, lam_init=lam_init, nseg=len(segs))
    qspec = pl.BlockSpec((1, tq, 512), lambda bi, qi: (bi, qi, 0))
    seg_specs, seg_args = [], []
    for k, v in segs:
        spec = pl.BlockSpec((1, k.shape[1], 512), lambda bi, qi: (bi, 0, 0))
        seg_specs += [spec, spec]
        seg_args += [k, v]
    return pl.pallas_call(
        kern,
        grid=(b, nqb),
        in_specs=[qspec, qspec,
                  pl.BlockSpec(slab.shape, lambda bi, qi: (0, 0, 0, 0)),
                  pl.BlockSpec((4, DH_A), lambda bi, qi: (0, 0)),
                  pl.BlockSpec((1, DV_A), lambda bi, qi: (0, 0))] + seg_specs,
        out_specs=qspec,
        out_shape=jax.ShapeDtypeStruct((b, t, 512), BF16),
        compiler_params=pltpu.CompilerParams(dimension_semantics=("arbitrary", "arbitrary"),
                                             vmem_limit_bytes=VMEM_LIMIT),
        name="attn_a",
    )(aq, sag, slab, a_lambda, subln_g[None, :], *seg_args)


def _kth_largest_key(hi_ref, lo_ref, nkt, kf):
    rows = hi_ref.shape[0]
    one, zero = jnp.int16(1), jnp.int16(0)
    lowest = jnp.int32(-32768)

    def count(ref, pred):
        acc = None
        for kt in range(nkt):
            ind = jnp.where(pred(ref[:, kt * LANE:(kt + 1) * LANE]), one, zero)
            acc = ind if acc is None else acc + ind
        return jnp.sum(acc.astype(F32), axis=1, keepdims=True)

    def tile16(x):
        return jnp.broadcast_to(x, (rows, LANE)).astype(jnp.int16)

    def search(ref, kth):
        def step(i, t):
            d = lax.shift_left(jnp.int32(1), 14 - 2 * i)
            for c in (t + 3 * d, t + 2 * d, t + d):
                c16 = tile16(c)
                t = jnp.where((count(ref, lambda x: x >= c16) >= kth) & (c > t), c, t)
            return t
        return lax.fori_loop(0, 8, step, jnp.full((rows, 1), lowest, jnp.int32), unroll=True)

    t_hi = search(hi_ref, kf)
    th16 = tile16(t_hi)
    kth_lo = kf - count(hi_ref, lambda x: x > th16)
    for kt in range(nkt):
        cols = slice(kt * LANE, (kt + 1) * LANE)
        lo_ref[:, cols] = jnp.where(hi_ref[:, cols] == th16, lo_ref[:, cols], jnp.int16(-32768))
    t_lo = search(lo_ref, kth_lo)
    return lax.shift_left(t_hi, 16) | ((t_lo + 32768) & jnp.int32(0xFFFF))


def _dsa_kernel(*refs, tq, qb0, variants, topk, nseg):
    bq_ref, bqi_ref, wif_ref, sbg_ref, slab_ref, mslab_ref, tri_ref = refs[:7]
    segs = [refs[7 + 3 * i:10 + 3 * i] for i in range(nseg)]
    out_ref, keys_ref, hi_ref, lo_ref = refs[7 + 3 * nseg:]
    qi = pl.program_id(1)
    qb = qi + qb0
    lo = lax.broadcasted_iota(jnp.int32, (tq, LANE), 1) < DH_B
    kf = float(topk)

    def body(seg_rows, n_far):
        nkt = sum(r // LANE for r in seg_rows)
        ty = _tile_types(qb, nkt)

        wi = wif_ref[0]
        kt0 = 0
        for (_, _, ki_ref), rows in zip(segs, seg_rows):
            ki = ki_ref[0, :rows, :].astype(BF16)
            qis = jnp.concatenate([_head_stack(bqi_ref[0, :, hp * LANE:(hp + 1) * LANE], lo, ki.shape[1])
                                   for hp in range(H_IDX // 2)], axis=0)
            r = jnp.maximum(_dot_nt(qis, ki), 0.0)
            score = wi[:, 0:1] * r[0:tq]
            for h in range(1, H_IDX):
                score = score + wi[:, h:h + 1] * r[h * tq:(h + 1) * tq]
            score = score + jnp.concatenate([mslab_ref[ty[kt0 + j], :tq] for j in range(rows // LANE)], axis=1)
            bits = pltpu.bitcast(score, jnp.int32)
            keys = bits ^ ((bits >> 31) & jnp.int32(0x7FFFFFFF))
            cols = slice(kt0 * LANE, kt0 * LANE + rows)
            keys_ref[:, cols] = keys
            hi_ref[:, cols] = (keys >> 16).astype(jnp.int16)
            lo_ref[:, cols] = ((keys & jnp.int32(0xFFFF)) - 32768).astype(jnp.int16)
            kt0 += rows // LANE

        t = _kth_largest_key(hi_ref, lo_ref, nkt, kf)

        def count(pred):
            acc = None
            for kt in range(nkt):
                ind = jnp.where(pred(keys_ref[:, kt * LANE:(kt + 1) * LANE]), 1.0, 0.0)
                acc = ind if acc is None else acc + ind
            return jnp.sum(acc, axis=1, keepdims=True)

        need = kf - count(lambda x: x > t)
        run = jnp.zeros((tq, 1), F32)
        negm = []
        for kt in range(nkt):
            kt_keys = keys_ref[:, kt * LANE:(kt + 1) * LANE]
            eq = jnp.where(kt_keys == t, 1.0, 0.0)
            rank = _dot(eq.astype(BF16), tri_ref[...]) + run
            tie = jnp.where(rank <= need, eq, 0.0)
            negm.append(jnp.where(kt_keys > t, 0.0, (1.0 - tie) * NEG))
            run = run + jnp.sum(eq, axis=1, keepdims=True)
        far2 = [_stack2(negm[kt], negm[kt]) for kt in range(n_far)]

        for hp in range(H_B // 2):
            hs = slice(hp * LANE, (hp + 1) * LANE)
            qp = bq_ref[0, :, hs]
            parts, vs, kt = [], [], 0
            for (k_ref, v_ref, _), rows in zip(segs, seg_rows):
                k = k_ref[0, :rows, :].astype(BF16)
                s = _dot_nt(_head_stack(qp, lo, k.shape[1]), k)
                for j in range(rows // LANE):
                    if kt < n_far:
                        nb2 = far2[kt]
                    else:
                        nb2 = _stack2(negm[kt] + slab_ref[2 * hp, ty[kt], :tq],
                                      negm[kt] + slab_ref[2 * hp + 1, ty[kt], :tq])
                    parts.append(s[:, j * LANE:(j + 1) * LANE] + nb2)
                    kt += 1
                vs.append(v_ref[0, :rows, :].astype(BF16))
            o2 = _softmax_pv(parts, vs)
            if o2.shape[1] == LANE:
                o = jnp.where(lo, o2[:tq], o2[tq:])
            else:
                o = jnp.concatenate([o2[:tq], o2[tq:]], axis=1)
            out_ref[0, :, hs] = (o * sbg_ref[0, :, hs].astype(F32)).astype(BF16)

    _run_variants(qi, variants, body)


def _dsa(bq, bqi, wif, sbg, slab, mslab, tri, segs, *, tq, qb0, topk):
    b, t, _ = bq.shape
    nqb = t // tq
    seg_rows = [s[0].shape[-2] for s in segs]
    variants = _variants(nqb, qb0, seg_rows)
    kern = functools.partial(_dsa_kernel, tq=tq, qb0=qb0, variants=variants, topk=topk, nseg=len(segs))

    def qspec(w):
        return pl.BlockSpec((1, tq, w), lambda bi, qi: (bi, qi, 0))

    seg_specs, seg_args = [], []
    for k, v, ki, layer in segs:
        for a in (k, v, ki):
            if layer is None:
                seg_specs.append(pl.BlockSpec((1,) + a.shape[1:], lambda bi, qi: (bi, 0, 0)))
            else:
                seg_specs.append(pl.BlockSpec((None, 1) + a.shape[2:], lambda bi, qi, layer=layer: (layer, bi, 0, 0)))
            seg_args.append(a)
    return pl.pallas_call(
        kern,
        grid=(b, nqb),
        in_specs=[qspec(512), qspec(256), qspec(LANE), qspec(512),
                  pl.BlockSpec(slab.shape, lambda bi, qi: (0, 0, 0, 0)),
                  pl.BlockSpec(mslab.shape, lambda bi, qi: (0, 0, 0)),
                  pl.BlockSpec((LANE, LANE), lambda bi, qi: (0, 0))] + seg_specs,
        out_specs=qspec(512),
        out_shape=jax.ShapeDtypeStruct((b, t, 512), BF16),
        scratch_shapes=[pltpu.VMEM((tq, sum(seg_rows)), jnp.int32), pltpu.VMEM((tq, sum(seg_rows)), jnp.int16),
                        pltpu.VMEM((tq, sum(seg_rows)), jnp.int16)],
        compiler_params=pltpu.CompilerParams(dimension_semantics=("arbitrary", "arbitrary"),
                                             vmem_limit_bytes=VMEM_LIMIT),
        name="dsa",
    )(bq, bqi, wif, sbg, slab, mslab, tri, *seg_args)


def _ret_kernel(cq_ref, ck_ref, cv_ref, scg_ref, st_ref, dmat_ref, dq_ref, dk_ref, gc_ref, bd_ref, gn_ref,
                out_ref, sto_ref, *, c, nchunks):
    lane = lax.broadcasted_iota(jnp.int32, (c, LANE), 1)
    lo = lane < DK_C
    for p in range(H_C // 2):
        st = st_ref[0, p]
        for ci in range(nchunks):
            rows = slice(ci * c, (ci + 1) * c)
            q = cq_ref[0, rows, p * LANE:(p + 1) * LANE]
            k = ck_ref[0, rows, p * LANE:(p + 1) * LANE]
            v = cv_ref[0, rows, p * 2 * DV_C:(p + 1) * 2 * DV_C]
            qd = (q.astype(F32) * dq_ref[p]).astype(BF16)
            cross = _dot(qd, st.astype(BF16))
            zero = jnp.zeros_like(q)
            for j in range(2):
                h = 2 * p + j
                qm = jnp.where(lo, q, zero) if j == 0 else jnp.where(lo, zero, q)
                a = (_dot_nt(qm, k) * dmat_ref[h]).astype(BF16)
                o = cross[:, j * DV_C:(j + 1) * DV_C] + _dot(a, v[:, j * DV_C:(j + 1) * DV_C])
                o = o * lax.rsqrt(jnp.mean(o * o, axis=1, keepdims=True) + EPS) * gn_ref[...]
                gate = scg_ref[0, rows, h * DV_C:(h + 1) * DV_C].astype(F32)
                out_ref[0, rows, h * DV_C:(h + 1) * DV_C] = (o * gate).astype(BF16)
            kd = (k.astype(F32) * dk_ref[p]).astype(BF16)
            st = (gc_ref[p] * st + _dot_tn(kd, v)) * bd_ref[...]
        sto_ref[0, p] = st


def _retention(cq, ck, cv, scg, state_pairs, log_gamma, gn_g, *, c):
    b, t, _ = cq.shape
    nchunks = t // c
    n = jnp.arange(c, dtype=F32)
    diff = n[:, None] - n[None, :]
    dmat = jnp.where(diff >= 0, jnp.exp(log_gamma[:, None, None] * jnp.maximum(diff, 0.0)[None]), 0.0)
    decay_q = jnp.exp((n[:, None] + 1.0) * log_gamma[None, :])
    decay_k = jnp.exp((c - 1.0 - n)[:, None] * log_gamma[None, :])
    decay_c = jnp.exp(c * log_gamma)

    def lanes(tab):
        return jnp.transpose(jnp.repeat(tab, DK_C, axis=1).reshape(c, H_C // 2, LANE), (1, 0, 2))

    gc = jnp.broadcast_to(jnp.repeat(decay_c, DK_C).reshape(H_C // 2, LANE, 1), (H_C // 2, LANE, 2 * DV_C))
    bd = (jnp.arange(LANE)[:, None] // DK_C == jnp.arange(2 * DV_C)[None, :] // DV_C).astype(F32)

    def full(a):
        return pl.BlockSpec(a.shape, lambda bi: (0,) * a.ndim)

    def tok(w):
        return pl.BlockSpec((1, t, w), lambda bi: (bi, 0, 0))

    stspec = pl.BlockSpec((1, H_C // 2, LANE, 2 * DV_C), lambda bi: (bi, 0, 0, 0))
    dq, dk, gnv = lanes(decay_q), lanes(decay_k), gn_g[None, :]
    return pl.pallas_call(
        functools.partial(_ret_kernel, c=c, nchunks=nchunks),
        grid=(b,),
        in_specs=[tok(256), tok(256), tok(512), tok(512), stspec,
                  full(dmat), full(dq), full(dk), full(gc), full(bd), full(gnv)],
        out_specs=[tok(512), stspec],
        out_shape=[jax.ShapeDtypeStruct((b, t, 512), BF16),
                   jax.ShapeDtypeStruct((b, H_C // 2, LANE, 2 * DV_C), F32)],
        compiler_params=pltpu.CompilerParams(dimension_semantics=("arbitrary",), vmem_limit_bytes=VMEM_LIMIT),
        name="retention",
    )(cq, ck, cv, scg, state_pairs, dmat, dq, dk, gc, bd, gnv)


def _state_to_pairs(st):
    b = st.shape[0]
    s4 = st.reshape(b, H_C // 2, 2, DK_C, DV_C)
    z = jnp.zeros_like(s4[:, :, 0])
    top = jnp.concatenate([s4[:, :, 0], z], axis=-1)
    bot = jnp.concatenate([z, s4[:, :, 1]], axis=-1)
    return jnp.concatenate([top, bot], axis=-2)


def _pairs_to_state(sp):
    b = sp.shape[0]
    s0 = sp[:, :, :DK_C, :DV_C]
    s1 = sp[:, :, DK_C:, DV_C:]
    return jnp.stack([s0, s1], axis=2).reshape(b, H_C, DK_C, DV_C)


def _merge_kernel(x_ref, ba_ref, bb_ref, bc_ref, smg_ref, wb_ref, wo_ref, y_ref):
    m = None
    for n, br in enumerate((ba_ref, bb_ref, bc_ref)):
        proj = _dot(br[...], wb_ref[n])
        term = smg_ref[:, n * D_MODEL:(n + 1) * D_MODEL].astype(F32) * proj
        m = term if m is None else m + term
    y_ref[...] = x_ref[...] + _dot(m.astype(BF16), wo_ref[...])


def _merge(x2d, br_a, br_b, br_c, smg, wb, wo, tm):
    n = x2d.shape[0]

    def row(w):
        return pl.BlockSpec((tm, w), lambda i: (i, 0))

    return pl.pallas_call(
        _merge_kernel,
        grid=(n // tm,),
        in_specs=[row(D_MODEL), row(W_BR), row(W_BR), row(W_BR), row(N_BRANCH * D_MODEL),
                  pl.BlockSpec((N_BRANCH, W_BR, D_MODEL), lambda i: (0, 0, 0)),
                  pl.BlockSpec((D_MODEL, D_MODEL), lambda i: (0, 0))],
        out_specs=row(D_MODEL),
        out_shape=jax.ShapeDtypeStruct((n, D_MODEL), F32),
        compiler_params=pltpu.CompilerParams(dimension_semantics=("arbitrary",), vmem_limit_bytes=VMEM_LIMIT),
        name="merge",
    )(x2d, br_a, br_b, br_c, smg, wb, wo)


def _pad_rows(a, rows):
    return jnp.pad(a, ((0, 0), (0, rows - a.shape[1]), (0, 0)))


def kernel(x_prompt, x_sample, cache_a_k, cache_a_v, cache_b_k, cache_b_v, cache_b_kidx, state_c, rel_bias,
           norm_g, w_in, a_qk_g, a_lambda, a_subln_g, b_qk_g, c_gn_g, w_branch, w_out):
    bp, tp, _ = x_prompt.shape
    bs, ts, _ = x_sample.shape
    depth = w_in.shape[0]
    past = cache_a_k.shape[2]
    assert tp % TQ == 0 and ts == CHUNK and past % LANE == 0
    pos_p = jnp.arange(tp, dtype=jnp.int32)
    pos_s = past + jnp.arange(ts, dtype=jnp.int32)
    topk_p = min(TOPK_MAX, tp // 4)
    topk_s = min(TOPK_MAX, (past + ts) // 4)
    log_gamma = jnp.log(1.0 - 2.0 ** (-5.0 - jnp.arange(H_C, dtype=F32)))

    slab_a, mslab = _bias_slabs(rel_bias[:, :H_A])
    slab_b, _ = _bias_slabs(rel_bias[:, H_A:])
    lane_i = jnp.arange(LANE)
    g64 = (lane_i[:, None] // 64 == lane_i[None, :] // 64).astype(BF16)
    tri = (lane_i[:, None] <= lane_i[None, :]).astype(BF16)
    c_prompt = 256 if tp % 256 == 0 else CHUNK
    tm_p = 256
    tm_s = 256 if (bs * ts) % 256 == 0 else ts
    qb_s = past // LANE

    yp = x_prompt.reshape(bp * tp, D_MODEL)
    ys = x_sample.reshape(bs * ts, D_MODEL)
    P = S = None
    sc_p, sc_s = [], []
    for l in range(depth):
        lam_init = 0.8 - 0.6 * math.exp(-0.3 * l)
        w_pad = _rearrange_w_in(w_in[l])
        wb = w_branch[l].astype(BF16)
        wo = w_out[l].astype(BF16)

        P = _project(yp, pos_p, tm_p, norm_g[l], w_pad, a_qk_g[l], b_qk_g[l], g64, layer=l, depth=depth, prev=P)
        r3 = lambda a: a.reshape(bp, tp, a.shape[-1])
        br_a = _attn_a(r3(P["aq"]), r3(P["sag"]), slab_a, a_lambda[l], a_subln_g[l],
                       [(r3(P["akb"]), r3(P["avb"]))], tq=TQ, qb0=0, lam_init=lam_init)
        br_b = _dsa(r3(P["bq"]), r3(P["bqi"]), r3(P["wif"]), r3(P["sbg"]), slab_b, mslab, tri,
                    [(r3(P["kkb"]), r3(P["vvb"]), r3(P["kib"]), None)], tq=TQ, qb0=0, topk=topk_p)
        st0 = jnp.zeros((bp, H_C // 2, LANE, 2 * DV_C), F32)
        br_c, stp = _retention(r3(P["cq"]), r3(P["ck"]), r3(P["cv"]), r3(P["scg"]), st0, log_gamma, c_gn_g[l],
                               c=c_prompt)
        yp = _merge(yp, br_a.reshape(bp * tp, W_BR), br_b.reshape(bp * tp, W_BR), br_c.reshape(bp * tp, W_BR),
                    P["smg"], wb, wo, tm_p)
        sc_p.append(_pairs_to_state(stp))

        S = _project(ys, pos_s, tm_s, norm_g[l], w_pad, a_qk_g[l], b_qk_g[l], g64, layer=l, depth=depth, prev=S)
        r3s = lambda a: a.reshape(bs, ts, a.shape[-1])
        rk = lambda a: _pad_rows(r3s(a), LANE)
        ka = cache_a_k[l].reshape(bs, past, 512).astype(BF16)
        va = cache_a_v[l].reshape(bs, past, 512).astype(BF16)
        br_a = _attn_a(r3s(S["aq"]), r3s(S["sag"]), slab_a, a_lambda[l], a_subln_g[l],
                       [(ka, va), (rk(S["akb"]), rk(S["avb"]))], tq=ts, qb0=qb_s, lam_init=lam_init)
        br_b = _dsa(r3s(S["bq"]), r3s(S["bqi"]), r3s(S["wif"]), r3s(S["sbg"]), slab_b, mslab, tri,
                    [(cache_b_k, cache_b_v, cache_b_kidx, l), (rk(S["kkb"]), rk(S["vvb"]), rk(S["kib"]), None)],
                    tq=ts, qb0=qb_s, topk=topk_s)
        br_c, sts = _retention(r3s(S["cq"]), r3s(S["ck"]), r3s(S["cv"]), r3s(S["scg"]),
                               _state_to_pairs(state_c[l].astype(F32)), log_gamma, c_gn_g[l], c=ts)
        ys = _merge(ys, br_a.reshape(bs * ts, W_BR), br_b.reshape(bs * ts, W_BR), br_c.reshape(bs * ts, W_BR),
                    S["smg"], wb, wo, tm_s)
        sc_s.append(_pairs_to_state(sts).astype(state_c.dtype))

    def caches(res, b, t):
        return (res["ak"].reshape(depth, b, t, H_A, 2, DH_A), res["av"].reshape(depth, b, t, H_A, DV_A),
                res["bk"].reshape(depth, b, t, DH_B), res["bv"].reshape(depth, b, t, DH_B),
                res["bki"].reshape(depth, b, t, D_IDX))

    return ((yp.reshape(bp, tp, D_MODEL), ys.reshape(bs, ts, D_MODEL))
            + caches(P, bp, tp) + (jnp.stack(sc_p),) + caches(S, bs, ts) + (jnp.stack(sc_s),))
```

```python
import functools
import math

import jax
import jax.numpy as jnp
from jax import lax
from jax.experimental import pallas as pl
from jax.experimental.pallas import tpu as pltpu

D_MODEL = 1024
CHUNK = 64
EPS = 1e-6
H_A = 4
DH_A = 64
DV_A = 128
H_B = 8
DH_B = 64
H_IDX = 4
D_IDX = 64
TOPK_MAX = 256
H_C = 4
DK_C = 64
DV_C = 128
W_BR = 512
N_BRANCH = 3
NUM_BUCKETS = 32
MAX_DISTANCE = 128
ROPE_BASE = 10000.0

LANE = 128
TQ = 128
NEG = -1e30
LOG2E = math.log2(math.e)
SEARCH_UNROLL = 8
INT_MIN = -(2 ** 31)
F32 = jnp.float32
BF16 = jnp.bfloat16
VMEM_LIMIT = 56 * 1024 * 1024

_SEC = {"aq": (0, 0, 512), "ak": (0, 512, 512), "av": (0, 1024, 512), "ag": (0, 1536, 512), "bq": (0, 2048, 512),
        "kv": (0, 2560, 128), "bqi": (0, 2688, 256), "kiw": (0, 2944, 128),
        "bg": (1, 0, 512), "cq": (1, 512, 256), "ck": (1, 768, 256), "cv": (1, 1024, 512), "cg": (1, 1536, 512),
        "mg": (1, 2048, 3072)}
HEAD_COLS = 3072
TAIL_START = 3012
TAIL_COLS = 5120


def _dot(a, b):
    return jnp.dot(a, b, preferred_element_type=F32)


def _dot_nt(a, b):
    return lax.dot_general(a, b, (((1,), (1,)), ((), ())), preferred_element_type=F32)


def _dot_tn(a, b):
    return lax.dot_general(a, b, (((0,), (0,)), ((), ())), preferred_element_type=F32)


def _proj_kernel(x_ref, g_ref, wh_ref, wt_ref, cq_cos_ref, cq_sin_ref, ck_cos_ref, ck_sin_ref,
                 gaq_ref, gak_ref, gbq_ref, gbk_ref, g64_ref,
                 aq_ref, ak_ref, akb_ref, av_ref, avb_ref, sag_ref, bq_ref, bqi_ref,
                 bk_ref, kkb_ref, bv_ref, vvb_ref, bki_ref, kib_ref, wif_ref, sbg_ref,
                 cq_ref, ck_ref, cv_ref, scg_ref, smg_ref):
    x = x_ref[...]
    ms = jnp.mean(x * x, axis=-1, keepdims=True)
    xn = ((x * lax.rsqrt(ms + EPS)) * g_ref[...]).astype(BF16)
    w_refs = (wh_ref, wt_ref)

    def mm(name, sub=None):
        op, c0, n = _SEC[name]
        if sub is not None:
            c0, n = c0 + sub[0], sub[1]
        return _dot(xn, w_refs[op][:, c0:c0 + n])

    def group_norm64(z, gain):
        zz = (z * z).astype(BF16)
        parts = [_dot(zz[:, j * LANE:(j + 1) * LANE], g64_ref[...]) for j in range(z.shape[1] // LANE)]
        ss = parts[0] if len(parts) == 1 else jnp.concatenate(parts, axis=1)
        return (z * lax.rsqrt(ss * (1.0 / 64.0) + EPS)) * gain

    lane = lax.broadcasted_iota(jnp.int32, (x.shape[0], LANE), 1)
    lo = lane < 64

    def rotate_half(z):
        first = (lane & 63) < DK_C // 2
        cols = []
        for j in range(z.shape[1] // LANE):
            zj = z[:, j * LANE:(j + 1) * LANE]
            cols.append(jnp.where(first, -pltpu.roll(zj, LANE - DK_C // 2, 1), pltpu.roll(zj, DK_C // 2, 1)))
        return jnp.concatenate(cols, axis=1)

    aq_ref[...] = group_norm64(mm("aq"), gaq_ref[...]).astype(BF16)
    ak = group_norm64(mm("ak"), gak_ref[...])
    ak_ref[...] = ak
    akb_ref[...] = ak.astype(BF16)
    av = mm("av")
    av_ref[...] = av
    avb_ref[...] = av.astype(BF16)
    z = mm("ag")
    sag_ref[...] = (z * jax.nn.sigmoid(z)).astype(BF16)
    bq_ref[...] = group_norm64(mm("bq"), gbq_ref[...]).astype(BF16)
    bqi_ref[...] = mm("bqi").astype(BF16)
    z = mm("kv")
    kv = jnp.where(lo, group_norm64(z, gbk_ref[...]), z)
    vk = pltpu.roll(kv, 64, 1)
    bk_ref[...] = kv[:, :DH_B]
    bv_ref[...] = vk[:, :DH_B]
    kkb_ref[...] = jnp.where(lo, kv, vk).astype(BF16)
    vvb_ref[...] = jnp.where(lo, vk, kv).astype(BF16)
    z = mm("kiw")
    zr = pltpu.roll(z, 64, 1)
    bki_ref[...] = z[:, :D_IDX]
    kib_ref[...] = jnp.where(lo, z, zr).astype(BF16)
    wif_ref[...] = zr
    z = mm("bg")
    sbg_ref[...] = (z * jax.nn.sigmoid(z)).astype(BF16)
    z = mm("cq")
    cq_ref[...] = (z * cq_cos_ref[...] + rotate_half(z) * cq_sin_ref[...]).astype(BF16)
    z = mm("ck")
    ck_ref[...] = (z * ck_cos_ref[...] + rotate_half(z) * ck_sin_ref[...]).astype(BF16)
    cv_ref[...] = mm("cv").astype(BF16)
    z = mm("cg")
    scg_ref[...] = (z * jax.nn.sigmoid(z)).astype(BF16)
    for j in range(N_BRANCH * 2):
        z = mm("mg", (j * 512, 512))
        smg_ref[:, j * 512:(j + 1) * 512] = jax.nn.sigmoid(z).astype(BF16)


def _rope_tables(pos, rows):
    half = DK_C // 2
    inv = ROPE_BASE ** (-jnp.arange(half, dtype=F32) / half)
    ang = pos.astype(F32)[:, None] * inv[None, :]
    cos = jnp.tile(jnp.cos(ang), (1, 2 * H_C))
    sin = jnp.tile(jnp.sin(ang), (1, 2 * H_C))
    reps = max(1, rows // cos.shape[0])
    cos = jnp.tile(cos, (reps, 1))
    sin = jnp.tile(sin, (reps, 1))
    kscale = DK_C ** -0.5
    return cos, sin, cos * kscale, sin * kscale


def _project(x2d, pos, tm, norm_g, w_head, w_tail, a_qk_g, b_qk_g, g64):
    n = x2d.shape[0]
    cq_cos, cq_sin, ck_cos, ck_sin = _rope_tables(pos, tm)
    ntab = cq_cos.shape[0] // tm
    gaq = jnp.tile(a_qk_g[0], 8)[None, :] * (LOG2E * DH_A ** -0.5)
    gak = jnp.tile(a_qk_g[1], 8)[None, :]
    gbq = jnp.tile(b_qk_g[0], 8)[None, :] * (LOG2E * DH_B ** -0.5)
    gbk = jnp.tile(b_qk_g[1], 2)[None, :]

    def row(width):
        return pl.BlockSpec((tm, width), lambda i: (i, 0))

    def const(shape, single=False):
        if single:
            return pl.BlockSpec(shape, lambda i: (0, 0), pipeline_mode=pl.Buffered(1))
        return pl.BlockSpec(shape, lambda i: (0, 0))

    tab = pl.BlockSpec((tm, 256), lambda i: (i % ntab, 0))
    outs = [("aq", 512, BF16), ("ak", 512, F32), ("akb", 512, BF16), ("av", 512, F32), ("avb", 512, BF16),
            ("sag", 512, BF16), ("bq", 512, BF16), ("bqi", 256, BF16), ("bk", 64, F32), ("kkb", 128, BF16),
            ("bv", 64, F32), ("vvb", 128, BF16), ("bki", 64, F32), ("kib", 128, BF16), ("wif", 128, F32),
            ("sbg", 512, BF16), ("cq", 256, BF16), ("ck", 256, BF16), ("cv", 512, BF16), ("scg", 512, BF16),
            ("smg", 3072, BF16)]
    res = pl.pallas_call(
        _proj_kernel,
        grid=(n // tm,),
        in_specs=[row(D_MODEL), const((1, D_MODEL)), const((D_MODEL, HEAD_COLS), single=True),
                  const((D_MODEL, TAIL_COLS), single=True),
                  tab, tab, tab, tab,
                  const((1, 512)), const((1, 512)), const((1, 512)), const((1, 128)), const((LANE, LANE))],
        out_specs=[row(w) for _, w, _ in outs],
        out_shape=[jax.ShapeDtypeStruct((n, w), dt) for _, w, dt in outs],
        compiler_params=pltpu.CompilerParams(dimension_semantics=("arbitrary",), vmem_limit_bytes=VMEM_LIMIT),
        name="proj",
    )(x2d, norm_g[None, :], w_head, w_tail, cq_cos, cq_sin, ck_cos, ck_sin, gaq, gak, gbq, gbk, g64)
    return {name: r for (name, _, _), r in zip(outs, res)}


def _t5_bucket(rel):
    nb = NUM_BUCKETS // 2
    max_exact = nb // 2
    ret = jnp.where(rel > 0, nb, 0)
    n = jnp.abs(rel)
    large = max_exact + (jnp.log(jnp.maximum(n, 1).astype(F32) / max_exact)
                         / math.log(MAX_DISTANCE / max_exact) * (nb - max_exact)).astype(jnp.int32)
    large = jnp.minimum(large, nb - 1)
    return ret + jnp.where(n < max_exact, n, large)


def _bias_slabs(bias_cols):
    i = jnp.arange(TQ, dtype=jnp.int32)[:, None]
    j = jnp.arange(LANE, dtype=jnp.int32)[None, :]
    vis0 = j < (i // CHUNK + 1) * CHUNK

    def lookup(rel):
        bucket = _t5_bucket(rel)
        out = jnp.zeros((bias_cols.shape[1],) + rel.shape, F32)
        for b in range(NUM_BUCKETS):
            out = jnp.where(bucket[None] == b, bias_cols[b].astype(F32)[:, None, None], out)
        return out

    far = lookup(-2 * LANE + j - i)
    c = far[:, :1, :1]
    tiles = [(far - c) * LOG2E, (lookup(-LANE + j - i) - c) * LOG2E,
             jnp.where(vis0[None], (lookup(j - i) - c) * LOG2E, NEG), jnp.full_like(far, NEG)]
    zero = jnp.zeros((TQ, LANE), F32)
    masks = [zero, zero, jnp.where(vis0, 0.0, -jnp.inf).astype(F32), jnp.full((TQ, LANE), -jnp.inf, F32)]
    return jnp.stack(tiles, axis=1), jnp.stack(masks, axis=0)


def _tile_types(qb, nkt):
    return [jnp.clip(kt - qb, -2, 1) + 2 for kt in range(nkt)]


def _variants(nqb, qb0, seg_rows):
    if len(seg_rows) > 1 or nqb == 1:
        assert nqb == 1
        return [(0, 1, tuple(seg_rows), max(0, qb0 - 1))]
    nvar = min(4, nqb)
    assert nqb % nvar == 0
    grp = nqb // nvar
    nkt_total = seg_rows[0] // LANE
    return [(v * grp, (v + 1) * grp, (min(nkt_total, qb0 + (v + 1) * grp) * LANE,), max(0, qb0 + v * grp - 1))
            for v in range(nvar)]


def _run_variants(qi, variants, body):
    if len(variants) == 1:
        body(variants[0][2], variants[0][3])
    else:
        for lo_q, hi_q, rows, n_far in variants:
            pl.when((qi >= lo_q) & (qi < hi_q))(functools.partial(body, rows, n_far))


def _head_stack(qp, lo, kw):
    if kw == LANE:
        zero = jnp.zeros_like(qp)
        return jnp.concatenate([jnp.where(lo, qp, zero), jnp.where(lo, zero, qp)], axis=0)
    return jnp.concatenate([qp[:, :kw], qp[:, kw:]], axis=0)


def _stack2(a, b):
    return jnp.concatenate([a, b], axis=0)


def _softmax_pv(parts, v_blocks):
    s2 = parts[0] if len(parts) == 1 else jnp.concatenate(parts, axis=1)
    m = jnp.max(s2, axis=1, keepdims=True)
    p = jnp.exp2(s2 - m)
    l = jnp.sum(p, axis=1, keepdims=True)
    pb = p.astype(BF16)
    dv = min(v.shape[1] for v in v_blocks)
    o, c0 = None, 0
    for v in v_blocks:
        term = _dot(pb[:, c0:c0 + v.shape[0]], v)[:, :dv]
        o = term if o is None else o + term
        c0 += v.shape[0]
    return o * (1.0 / l)


def _attn_a_kernel(*refs, tq, qb0, variants, lam_init, nseg):
    aq_ref, sag_ref, slab_ref, alam_ref, subg_ref = refs[:5]
    segs = [refs[5 + 2 * i:7 + 2 * i] for i in range(nseg)]
    out_ref = refs[5 + 2 * nseg]
    qi = pl.program_id(1)
    qb = qi + qb0
    lp = alam_ref[...]
    lam = (jnp.exp(jnp.sum(lp[0:1] * lp[1:2], axis=1, keepdims=True))
           - jnp.exp(jnp.sum(lp[2:3] * lp[3:4], axis=1, keepdims=True)) + lam_init)
    lo = lax.broadcasted_iota(jnp.int32, (tq, LANE), 1) < DH_A

    def body(seg_rows, n_far):
        nkt = sum(r // LANE for r in seg_rows)
        ty = _tile_types(qb, nkt)
        for h in range(H_A):
            hs = slice(h * LANE, (h + 1) * LANE)
            q2 = _head_stack(aq_ref[0, :, hs], lo, LANE)
            parts, kt = [], 0
            for (k_ref, _), rows in zip(segs, seg_rows):
                s = _dot_nt(q2, k_ref[0, :rows, hs])
                n_plain = min(max(n_far - kt, 0), rows // LANE)
                if n_plain:
                    parts.append(s[:, :n_plain * LANE])
                for j in range(n_plain, rows // LANE):
                    b = slab_ref[h, ty[kt + j], :tq]
                    parts.append(s[:, j * LANE:(j + 1) * LANE] + _stack2(b, b))
                kt += rows // LANE
            o2 = _softmax_pv(parts, [v_ref[0, :rows, hs] for (_, v_ref), rows in zip(segs, seg_rows)])
            o = o2[:tq] - lam * o2[tq:]
            o = o * lax.rsqrt(jnp.mean(o * o, axis=1, keepdims=True) + EPS) * subg_ref[...] * (1.0 - lam_init)
            out_ref[0, :, hs] = (o * sag_ref[0, :, hs].astype(F32)).astype(BF16)

    _run_variants(qi, variants, body)


def _attn_a(aq, sag, slab, a_lambda, subln_g, segs, *, tq, qb0, lam_init):
    b, t, _ = aq.shape
    nqb = t // tq
    seg_rows = [k.shape[1] for k, _ in segs]
    variants = _variants(nqb, qb0, seg_rows)
    kern = functools.partial(_attn_a_kernel, tq=tq, qb0=qb0, variants=variants, lam_init=lam_init, nseg=len(segs))
    qspec = pl.BlockSpec((1, tq, 512), lambda bi, qi: (bi, qi, 0))
    seg_specs, seg_args = [], []
    for k, v in segs:
        spec = pl.BlockSpec((1, k.shape[1], 512), lambda bi, qi: (bi, 0, 0))
        seg_specs += [spec, spec]
        seg_args += [k, v]
    return pl.pallas_call(
        kern,
        grid=(b, nqb),
        in_specs=[qspec, qspec,
                  pl.BlockSpec(slab.shape, lambda bi, qi: (0, 0, 0, 0)),
                  pl.BlockSpec((4, DH_A), lambda bi, qi: (0, 0)),
                  pl.BlockSpec((1, DV_A), lambda bi, qi: (0, 0))] + seg_specs,
        out_specs=qspec,
        out_shape=jax.ShapeDtypeStruct((b, t, 512), BF16),
        compiler_params=pltpu.CompilerParams(dimension_semantics=("arbitrary", "arbitrary"),
                                             vmem_limit_bytes=VMEM_LIMIT),
        name="attn_a",
    )(aq, sag, slab, a_lambda, subln_g[None, :], *seg_args)


def _dsa_kernel(*refs, tq, qb0, variants, topk, nseg):
    bq_ref, bqi_ref, wif_ref, sbg_ref, slab_ref, mslab_ref, tri_ref = refs[:7]
    segs = [refs[7 + 3 * i:10 + 3 * i] for i in range(nseg)]
    out_ref, keys_ref = refs[7 + 3 * nseg:]
    qi = pl.program_id(1)
    qb = qi + qb0
    lo = lax.broadcasted_iota(jnp.int32, (tq, LANE), 1) < DH_B
    kf = float(topk)

    def body(seg_rows, n_far):
        nkt = sum(r // LANE for r in seg_rows)
        ty = _tile_types(qb, nkt)

        wi = wif_ref[0]
        kt0 = 0
        for (_, _, ki_ref), rows in zip(segs, seg_rows):
            ki = ki_ref[0, :rows, :].astype(BF16)
            qis = jnp.concatenate([_head_stack(bqi_ref[0, :, hp * LANE:(hp + 1) * LANE], lo, ki.shape[1])
                                   for hp in range(H_IDX // 2)], axis=0)
            r = jnp.maximum(_dot_nt(qis, ki), 0.0)
            score = wi[:, 0:1] * r[0:tq]
            for h in range(1, H_IDX):
                score = score + wi[:, h:h + 1] * r[h * tq:(h + 1) * tq]
            score = score + jnp.concatenate([mslab_ref[ty[kt0 + j], :tq] for j in range(rows // LANE)], axis=1)
            bits = pltpu.bitcast(score, jnp.int32)
            keys_ref[:, kt0 * LANE:kt0 * LANE + rows] = bits ^ ((bits >> 31) & jnp.int32(0x7FFFFFFF))
            kt0 += rows // LANE

        def count(pred):
            acc = None
            for kt in range(nkt):
                ind = jnp.where(pred(keys_ref[:, kt * LANE:(kt + 1) * LANE]), 1.0, 0.0)
                acc = ind if acc is None else acc + ind
            return jnp.sum(acc, axis=1, keepdims=True)

        def search(i, t):
            cand = t + lax.shift_left(jnp.int32(1), 31 - i)
            return jnp.where(count(lambda x: x >= cand) >= kf, cand, t)

        t = lax.fori_loop(0, 32, search, jnp.full((tq, 1), INT_MIN, jnp.int32), unroll=SEARCH_UNROLL)

        need = kf - count(lambda x: x > t)
        run = jnp.zeros((tq, 1), F32)
        negm = []
        for kt in range(nkt):
            kt_keys = keys_ref[:, kt * LANE:(kt + 1) * LANE]
            eq = jnp.where(kt_keys == t, 1.0, 0.0)
            rank = _dot(eq.astype(BF16), tri_ref[...]) + run
            tie = jnp.where(rank <= need, eq, 0.0)
            negm.append(jnp.where(kt_keys > t, 0.0, (1.0 - tie) * NEG))
            run = run + jnp.sum(eq, axis=1, keepdims=True)
        far2 = [_stack2(negm[kt], negm[kt]) for kt in range(n_far)]

        for hp in range(H_B // 2):
            hs = slice(hp * LANE, (hp + 1) * LANE)
            qp = bq_ref[0, :, hs]
            parts, vs, kt = [], [], 0
            for (k_ref, v_ref, _), rows in zip(segs, seg_rows):
                k = k_ref[0, :rows, :].astype(BF16)
                s = _dot_nt(_head_stack(qp, lo, k.shape[1]), k)
                for j in range(rows // LANE):
                    if kt < n_far:
                        nb2 = far2[kt]
                    else:
                        nb2 = _stack2(negm[kt] + slab_ref[2 * hp, ty[kt], :tq],
                                      negm[kt] + slab_ref[2 * hp + 1, ty[kt], :tq])
                    parts.append(s[:, j * LANE:(j + 1) * LANE] + nb2)
                    kt += 1
                vs.append(v_ref[0, :rows, :].astype(BF16))
            o2 = _softmax_pv(parts, vs)
            if o2.shape[1] == LANE:
                o = jnp.where(lo, o2[:tq], o2[tq:])
            else:
                o = jnp.concatenate([o2[:tq], o2[tq:]], axis=1)
            out_ref[0, :, hs] = (o * sbg_ref[0, :, hs].astype(F32)).astype(BF16)

    _run_variants(qi, variants, body)


def _dsa(bq, bqi, wif, sbg, slab, mslab, tri, segs, *, tq, qb0, topk):
    b, t, _ = bq.shape
    nqb = t // tq
    seg_rows = [s[0].shape[-2] for s in segs]
    variants = _variants(nqb, qb0, seg_rows)
    kern = functools.partial(_dsa_kernel, tq=tq, qb0=qb0, variants=variants, topk=topk, nseg=len(segs))

    def qspec(w):
        return pl.BlockSpec((1, tq, w), lambda bi, qi: (bi, qi, 0))

    seg_specs, seg_args = [], []
    for k, v, ki, layer in segs:
        for a in (k, v, ki):
            if layer is None:
                seg_specs.append(pl.BlockSpec((1,) + a.shape[1:], lambda bi, qi: (bi, 0, 0)))
            else:
                seg_specs.append(pl.BlockSpec((None, 1) + a.shape[2:], lambda bi, qi, layer=layer: (layer, bi, 0, 0)))
            seg_args.append(a)
    return pl.pallas_call(
        kern,
        grid=(b, nqb),
        in_specs=[qspec(512), qspec(256), qspec(LANE), qspec(512),
                  pl.BlockSpec(slab.shape, lambda bi, qi: (0, 0, 0, 0)),
                  pl.BlockSpec(mslab.shape, lambda bi, qi: (0, 0, 0)),
                  pl.BlockSpec((LANE, LANE), lambda bi, qi: (0, 0))] + seg_specs,
        out_specs=qspec(512),
        out_shape=jax.ShapeDtypeStruct((b, t, 512), BF16),
        scratch_shapes=[pltpu.VMEM((tq, sum(seg_rows)), jnp.int32)],
        compiler_params=pltpu.CompilerParams(dimension_semantics=("arbitrary", "arbitrary"),
                                             vmem_limit_bytes=VMEM_LIMIT),
        name="dsa",
    )(bq, bqi, wif, sbg, slab, mslab, tri, *seg_args)


def _ret_kernel(cq_ref, ck_ref, cv_ref, scg_ref, st_ref, dmat_ref, dq_ref, dk_ref, gc_ref, bd_ref, gn_ref,
                out_ref, sto_ref, *, c, nchunks):
    lane = lax.broadcasted_iota(jnp.int32, (c, LANE), 1)
    lo = lane < DK_C
    for p in range(H_C // 2):
        st = st_ref[0, p]
        for ci in range(nchunks):
            rows = slice(ci * c, (ci + 1) * c)
            q = cq_ref[0, rows, p * LANE:(p + 1) * LANE]
            k = ck_ref[0, rows, p * LANE:(p + 1) * LANE]
            v = cv_ref[0, rows, p * 2 * DV_C:(p + 1) * 2 * DV_C]
            qd = (q.astype(F32) * dq_ref[p]).astype(BF16)
            cross = _dot(qd, st.astype(BF16))
            zero = jnp.zeros_like(q)
            for j in range(2):
                h = 2 * p + j
                qm = jnp.where(lo, q, zero) if j == 0 else jnp.where(lo, zero, q)
                a = (_dot_nt(qm, k) * dmat_ref[h]).astype(BF16)
                o = cross[:, j * DV_C:(j + 1) * DV_C] + _dot(a, v[:, j * DV_C:(j + 1) * DV_C])
                o = o * lax.rsqrt(jnp.mean(o * o, axis=1, keepdims=True) + EPS) * gn_ref[...]
                gate = scg_ref[0, rows, h * DV_C:(h + 1) * DV_C].astype(F32)
                out_ref[0, rows, h * DV_C:(h + 1) * DV_C] = (o * gate).astype(BF16)
            kd = (k.astype(F32) * dk_ref[p]).astype(BF16)
            st = (gc_ref[p] * st + _dot_tn(kd, v)) * bd_ref[...]
        sto_ref[0, p] = st


def _retention(cq, ck, cv, scg, state_pairs, log_gamma, gn_g, *, c):
    b, t, _ = cq.shape
    nchunks = t // c
    n = jnp.arange(c, dtype=F32)
    diff = n[:, None] - n[None, :]
    dmat = jnp.where(diff >= 0, jnp.exp(log_gamma[:, None, None] * jnp.maximum(diff, 0.0)[None]), 0.0)
    decay_q = jnp.exp((n[:, None] + 1.0) * log_gamma[None, :])
    decay_k = jnp.exp((c - 1.0 - n)[:, None] * log_gamma[None, :])
    decay_c = jnp.exp(c * log_gamma)

    def lanes(tab):
        return jnp.transpose(jnp.repeat(tab, DK_C, axis=1).reshape(c, H_C // 2, LANE), (1, 0, 2))

    gc = jnp.broadcast_to(jnp.repeat(decay_c, DK_C).reshape(H_C // 2, LANE, 1), (H_C // 2, LANE, 2 * DV_C))
    bd = (jnp.arange(LANE)[:, None] // DK_C == jnp.arange(2 * DV_C)[None, :] // DV_C).astype(F32)

    def full(a):
        return pl.BlockSpec(a.shape, lambda bi: (0,) * a.ndim)

    def tok(w):
        return pl.BlockSpec((1, t, w), lambda bi: (bi, 0, 0))

    stspec = pl.BlockSpec((1, H_C // 2, LANE, 2 * DV_C), lambda bi: (bi, 0, 0, 0))
    dq, dk, gnv = lanes(decay_q), lanes(decay_k), gn_g[None, :]
    return pl.pallas_call(
        functools.partial(_ret_kernel, c=c, nchunks=nchunks),
        grid=(b,),
        in_specs=[tok(256), tok(256), tok(512), tok(512), stspec,
                  full(dmat), full(dq), full(dk), full(gc), full(bd), full(gnv)],
        out_specs=[tok(512), stspec],
        out_shape=[jax.ShapeDtypeStruct((b, t, 512), BF16),
                   jax.ShapeDtypeStruct((b, H_C // 2, LANE, 2 * DV_C), F32)],
        compiler_params=pltpu.CompilerParams(dimension_semantics=("arbitrary",), vmem_limit_bytes=VMEM_LIMIT),
        name="retention",
    )(cq, ck, cv, scg, state_pairs, dmat, dq, dk, gc, bd, gnv)


def _state_to_pairs(st):
    b = st.shape[0]
    s4 = st.reshape(b, H_C // 2, 2, DK_C, DV_C)
    z = jnp.zeros_like(s4[:, :, 0])
    top = jnp.concatenate([s4[:, :, 0], z], axis=-1)
    bot = jnp.concatenate([z, s4[:, :, 1]], axis=-1)
    return jnp.concatenate([top, bot], axis=-2)


def _pairs_to_state(sp):
    b = sp.shape[0]
    s0 = sp[:, :, :DK_C, :DV_C]
    s1 = sp[:, :, DK_C:, DV_C:]
    return jnp.stack([s0, s1], axis=2).reshape(b, H_C, DK_C, DV_C)


def _merge_kernel(x_ref, ba_ref, bb_ref, bc_ref, smg_ref, wb_ref, wo_ref, y_ref):
    m = None
    for n, br in enumerate((ba_ref, bb_ref, bc_ref)):
        proj = _dot(br[...], wb_ref[n])
        term = smg_ref[:, n * D_MODEL:(n + 1) * D_MODEL].astype(F32) * proj
        m = term if m is None else m + term
    y_ref[...] = x_ref[...] + _dot(m.astype(BF16), wo_ref[...])


def _merge(x2d, br_a, br_b, br_c, smg, wb, wo, tm):
    n = x2d.shape[0]

    def row(w):
        return pl.BlockSpec((tm, w), lambda i: (i, 0))

    return pl.pallas_call(
        _merge_kernel,
        grid=(n // tm,),
        in_specs=[row(D_MODEL), row(W_BR), row(W_BR), row(W_BR), row(N_BRANCH * D_MODEL),
                  pl.BlockSpec((N_BRANCH, W_BR, D_MODEL), lambda i: (0, 0, 0)),
                  pl.BlockSpec((D_MODEL, D_MODEL), lambda i: (0, 0))],
        out_specs=row(D_MODEL),
        out_shape=jax.ShapeDtypeStruct((n, D_MODEL), F32),
        compiler_params=pltpu.CompilerParams(dimension_semantics=("arbitrary",), vmem_limit_bytes=VMEM_LIMIT),
        name="merge",
    )(x2d, br_a, br_b, br_c, smg, wb, wo)


def _pad_rows(a, rows):
    return jnp.pad(a, ((0, 0), (0, rows - a.shape[1]), (0, 0)))


def kernel(x_prompt, x_sample, cache_a_k, cache_a_v, cache_b_k, cache_b_v, cache_b_kidx, state_c, rel_bias,
           norm_g, w_in, a_qk_g, a_lambda, a_subln_g, b_qk_g, c_gn_g, w_branch, w_out):
    bp, tp, _ = x_prompt.shape
    bs, ts, _ = x_sample.shape
    depth = w_in.shape[0]
    past = cache_a_k.shape[2]
    assert tp % TQ == 0 and ts == CHUNK and past % LANE == 0
    pos_p = jnp.arange(tp, dtype=jnp.int32)
    pos_s = past + jnp.arange(ts, dtype=jnp.int32)
    topk_p = min(TOPK_MAX, tp // 4)
    topk_s = min(TOPK_MAX, (past + ts) // 4)
    log_gamma = jnp.log(1.0 - 2.0 ** (-5.0 - jnp.arange(H_C, dtype=F32)))

    slab_a, mslab = _bias_slabs(rel_bias[:, :H_A])
    slab_b, _ = _bias_slabs(rel_bias[:, H_A:])
    lane_i = jnp.arange(LANE)
    g64 = (lane_i[:, None] // 64 == lane_i[None, :] // 64).astype(BF16)
    tri = (lane_i[:, None] <= lane_i[None, :]).astype(BF16)
    c_prompt = 256 if tp % 256 == 0 else CHUNK
    tm_p = 512
    tm_s = 256 if (bs * ts) % 256 == 0 else ts
    qb_s = past // LANE

    yp = x_prompt.reshape(bp * tp, D_MODEL)
    ys = x_sample.reshape(bs * ts, D_MODEL)
    outs_p = {k: [] for k in ("ak", "av", "bk", "bv", "bki", "sc")}
    outs_s = {k: [] for k in ("ak", "av", "bk", "bv", "bki", "sc")}
    for l in range(depth):
        lam_init = 0.8 - 0.6 * math.exp(-0.3 * l)
        w_head = w_in[l, :, :HEAD_COLS].astype(BF16)
        w_tail = w_in[l, :, TAIL_START:TAIL_START + TAIL_COLS].astype(BF16)
        wb = w_branch[l].astype(BF16)
        wo = w_out[l].astype(BF16)

        P = _project(yp, pos_p, tm_p, norm_g[l], w_head, w_tail, a_qk_g[l], b_qk_g[l], g64)
        r3 = lambda a: a.reshape(bp, tp, a.shape[-1])
        br_a = _attn_a(r3(P["aq"]), r3(P["sag"]), slab_a, a_lambda[l], a_subln_g[l],
                       [(r3(P["akb"]), r3(P["avb"]))], tq=TQ, qb0=0, lam_init=lam_init)
        br_b = _dsa(r3(P["bq"]), r3(P["bqi"]), r3(P["wif"]), r3(P["sbg"]), slab_b, mslab, tri,
                    [(r3(P["kkb"]), r3(P["vvb"]), r3(P["kib"]), None)], tq=TQ, qb0=0, topk=topk_p)
        st0 = jnp.zeros((bp, H_C // 2, LANE, 2 * DV_C), F32)
        br_c, stp = _retention(r3(P["cq"]), r3(P["ck"]), r3(P["cv"]), r3(P["scg"]), st0, log_gamma, c_gn_g[l],
                               c=c_prompt)
        yp = _merge(yp, br_a.reshape(bp * tp, W_BR), br_b.reshape(bp * tp, W_BR), br_c.reshape(bp * tp, W_BR),
                    P["smg"], wb, wo, tm_p)
        outs_p["ak"].append(P["ak"].reshape(bp, tp, H_A, 2, DH_A))
        outs_p["av"].append(P["av"].reshape(bp, tp, H_A, DV_A))
        outs_p["bk"].append(P["bk"].reshape(bp, tp, DH_B))
        outs_p["bv"].append(P["bv"].reshape(bp, tp, DH_B))
        outs_p["bki"].append(P["bki"].reshape(bp, tp, D_IDX))
        outs_p["sc"].append(_pairs_to_state(stp))

        S = _project(ys, pos_s, tm_s, norm_g[l], w_head, w_tail, a_qk_g[l], b_qk_g[l], g64)
        r3s = lambda a: a.reshape(bs, ts, a.shape[-1])
        rk = lambda a: _pad_rows(r3s(a), LANE)
        ka = cache_a_k[l].reshape(bs, past, 512).astype(BF16)
        va = cache_a_v[l].reshape(bs, past, 512).astype(BF16)
        br_a = _attn_a(r3s(S["aq"]), r3s(S["sag"]), slab_a, a_lambda[l], a_subln_g[l],
                       [(ka, va), (rk(S["akb"]), rk(S["avb"]))], tq=ts, qb0=qb_s, lam_init=lam_init)
        br_b = _dsa(r3s(S["bq"]), r3s(S["bqi"]), r3s(S["wif"]), r3s(S["sbg"]), slab_b, mslab, tri,
                    [(cache_b_k, cache_b_v, cache_b_kidx, l), (rk(S["kkb"]), rk(S["vvb"]), rk(S["kib"]), None)],
                    tq=ts, qb0=qb_s, topk=topk_s)
        br_c, sts = _retention(r3s(S["cq"]), r3s(S["ck"]), r3s(S["cv"]), r3s(S["scg"]),
                               _state_to_pairs(state_c[l].astype(F32)), log_gamma, c_gn_g[l], c=ts)
        ys = _merge(ys, br_a.reshape(bs * ts, W_BR), br_b.reshape(bs * ts, W_BR), br_c.reshape(bs * ts, W_BR),
                    S["smg"], wb, wo, tm_s)
        outs_s["ak"].append(S["ak"].reshape(bs, ts, H_A, 2, DH_A))
        outs_s["av"].append(S["av"].reshape(bs, ts, H_A, DV_A))
        outs_s["bk"].append(S["bk"].reshape(bs, ts, DH_B))
        outs_s["bv"].append(S["bv"].reshape(bs, ts, DH_B))
        outs_s["bki"].append(S["bki"].reshape(bs, ts, D_IDX))
        outs_s["sc"].append(_pairs_to_state(sts).astype(state_c.dtype))

    order = ("ak", "av", "bk", "bv", "bki", "sc")
    return ((yp.reshape(bp, tp, D_MODEL), ys.reshape(bs, ts, D_MODEL))
            + tuple(jnp.stack(outs_p[k]) for k in order)
            + tuple(jnp.stack(outs_s[k]) for k in order))
```

```python
import functools
import math

import jax
import jax.numpy as jnp
from jax import lax
from jax.experimental import pallas as pl
from jax.experimental.pallas import tpu as pltpu

D_MODEL = 1024
CHUNK = 64
EPS = 1e-6
H_A = 4
DH_A = 64
DV_A = 128
H_B = 8
DH_B = 64
H_IDX = 4
D_IDX = 64
TOPK_MAX = 256
H_C = 4
DK_C = 64
DV_C = 128
W_BR = 512
N_BRANCH = 3
NUM_BUCKETS = 32
MAX_DISTANCE = 128
ROPE_BASE = 10000.0

LANE = 128
TQ = 128
NEG = -1e30
LOG2E = math.log2(math.e)
SEARCH_UNROLL = 8
INT_MIN = -(2 ** 31)
F32 = jnp.float32
BF16 = jnp.bfloat16
VMEM_LIMIT = 56 * 1024 * 1024

_SEC = {"aq": (0, 0, 512), "ak": (0, 512, 512), "av": (0, 1024, 512), "ag": (0, 1536, 512), "bq": (0, 2048, 512),
        "kv": (0, 2560, 128), "bqi": (0, 2688, 256), "kiw": (0, 2944, 128),
        "bg": (1, 0, 512), "cq": (1, 512, 256), "ck": (1, 768, 256), "cv": (1, 1024, 512), "cg": (1, 1536, 512),
        "mg": (1, 2048, 3072)}
HEAD_COLS = 3072
TAIL_START = 3012
TAIL_COLS = 5120


def _dot(a, b):
    return jnp.dot(a, b, preferred_element_type=F32)


def _dot_nt(a, b):
    return lax.dot_general(a, b, (((1,), (1,)), ((), ())), preferred_element_type=F32)


def _dot_tn(a, b):
    return lax.dot_general(a, b, (((0,), (0,)), ((), ())), preferred_element_type=F32)


def _proj_kernel(x_ref, g_ref, wh_ref, wt_ref, cq_cos_ref, cq_sin_ref, ck_cos_ref, ck_sin_ref,
                 gaq_ref, gak_ref, gbq_ref, gbk_ref, g64_ref,
                 aq_ref, ak_ref, akb_ref, av_ref, avb_ref, sag_ref, bq_ref, bqi_ref,
                 bk_ref, kkb_ref, bv_ref, vvb_ref, bki_ref, kib_ref, wif_ref, sbg_ref,
                 cq_ref, ck_ref, cv_ref, scg_ref, smg_ref):
    x = x_ref[...]
    ms = jnp.mean(x * x, axis=-1, keepdims=True)
    xn = ((x * lax.rsqrt(ms + EPS)) * g_ref[...]).astype(BF16)
    w_refs = (wh_ref, wt_ref)

    def mm(name, sub=None):
        op, c0, n = _SEC[name]
        if sub is not None:
            c0, n = c0 + sub[0], sub[1]
        return _dot(xn, w_refs[op][:, c0:c0 + n])

    def group_norm64(z, gain):
        zz = (z * z).astype(BF16)
        parts = [_dot(zz[:, j * LANE:(j + 1) * LANE], g64_ref[...]) for j in range(z.shape[1] // LANE)]
        ss = parts[0] if len(parts) == 1 else jnp.concatenate(parts, axis=1)
        return (z * lax.rsqrt(ss * (1.0 / 64.0) + EPS)) * gain

    lane = lax.broadcasted_iota(jnp.int32, (x.shape[0], LANE), 1)
    lo = lane < 64

    def rotate_half(z):
        first = (lane & 63) < DK_C // 2
        cols = []
        for j in range(z.shape[1] // LANE):
            zj = z[:, j * LANE:(j + 1) * LANE]
            cols.append(jnp.where(first, -pltpu.roll(zj, LANE - DK_C // 2, 1), pltpu.roll(zj, DK_C // 2, 1)))
        return jnp.concatenate(cols, axis=1)

    aq_ref[...] = group_norm64(mm("aq"), gaq_ref[...]).astype(BF16)
    ak = group_norm64(mm("ak"), gak_ref[...])
    ak_ref[...] = ak
    akb_ref[...] = ak.astype(BF16)
    av = mm("av")
    av_ref[...] = av
    avb_ref[...] = av.astype(BF16)
    z = mm("ag")
    sag_ref[...] = (z * jax.nn.sigmoid(z)).astype(BF16)
    bq_ref[...] = group_norm64(mm("bq"), gbq_ref[...]).astype(BF16)
    bqi_ref[...] = mm("bqi").astype(BF16)
    z = mm("kv")
    kv = jnp.where(lo, group_norm64(z, gbk_ref[...]), z)
    vk = pltpu.roll(kv, 64, 1)
    bk_ref[...] = kv[:, :DH_B]
    bv_ref[...] = vk[:, :DH_B]
    kkb_ref[...] = jnp.where(lo, kv, vk).astype(BF16)
    vvb_ref[...] = jnp.where(lo, vk, kv).astype(BF16)
    z = mm("kiw")
    zr = pltpu.roll(z, 64, 1)
    bki_ref[...] = z[:, :D_IDX]
    kib_ref[...] = jnp.where(lo, z, zr).astype(BF16)
    wif_ref[...] = zr
    z = mm("bg")
    sbg_ref[...] = (z * jax.nn.sigmoid(z)).astype(BF16)
    z = mm("cq")
    cq_ref[...] = (z * cq_cos_ref[...] + rotate_half(z) * cq_sin_ref[...]).astype(BF16)
    z = mm("ck")
    ck_ref[...] = (z * ck_cos_ref[...] + rotate_half(z) * ck_sin_ref[...]).astype(BF16)
    cv_ref[...] = mm("cv").astype(BF16)
    z = mm("cg")
    scg_ref[...] = (z * jax.nn.sigmoid(z)).astype(BF16)
    for j in range(N_BRANCH * 2):
        z = mm("mg", (j * 512, 512))
        smg_ref[:, j * 512:(j + 1) * 512] = jax.nn.sigmoid(z).astype(BF16)


def _rope_tables(pos, rows):
    half = DK_C // 2
    inv = ROPE_BASE ** (-jnp.arange(half, dtype=F32) / half)
    ang = pos.astype(F32)[:, None] * inv[None, :]
    cos = jnp.tile(jnp.cos(ang), (1, 2 * H_C))
    sin = jnp.tile(jnp.sin(ang), (1, 2 * H_C))
    reps = max(1, rows // cos.shape[0])
    cos = jnp.tile(cos, (reps, 1))
    sin = jnp.tile(sin, (reps, 1))
    kscale = DK_C ** -0.5
    return cos, sin, cos * kscale, sin * kscale


def _project(x2d, pos, tm, norm_g, w_head, w_tail, a_qk_g, b_qk_g, g64):
    n = x2d.shape[0]
    cq_cos, cq_sin, ck_cos, ck_sin = _rope_tables(pos, tm)
    ntab = cq_cos.shape[0] // tm
    gaq = jnp.tile(a_qk_g[0], 8)[None, :] * (LOG2E * DH_A ** -0.5)
    gak = jnp.tile(a_qk_g[1], 8)[None, :]
    gbq = jnp.tile(b_qk_g[0], 8)[None, :] * (LOG2E * DH_B ** -0.5)
    gbk = jnp.tile(b_qk_g[1], 2)[None, :]

    def row(width):
        return pl.BlockSpec((tm, width), lambda i: (i, 0))

    def const(shape, single=False):
        if single:
            return pl.BlockSpec(shape, lambda i: (0, 0), pipeline_mode=pl.Buffered(1))
        return pl.BlockSpec(shape, lambda i: (0, 0))

    tab = pl.BlockSpec((tm, 256), lambda i: (i % ntab, 0))
    outs = [("aq", 512, BF16), ("ak", 512, F32), ("akb", 512, BF16), ("av", 512, F32), ("avb", 512, BF16),
            ("sag", 512, BF16), ("bq", 512, BF16), ("bqi", 256, BF16), ("bk", 64, F32), ("kkb", 128, BF16),
            ("bv", 64, F32), ("vvb", 128, BF16), ("bki", 64, F32), ("kib", 128, BF16), ("wif", 128, F32),
            ("sbg", 512, BF16), ("cq", 256, BF16), ("ck", 256, BF16), ("cv", 512, BF16), ("scg", 512, BF16),
            ("smg", 3072, BF16)]
    res = pl.pallas_call(
        _proj_kernel,
        grid=(n // tm,),
        in_specs=[row(D_MODEL), const((1, D_MODEL)), const((D_MODEL, HEAD_COLS), single=True),
                  const((D_MODEL, TAIL_COLS), single=True),
                  tab, tab, tab, tab,
                  const((1, 512)), const((1, 512)), const((1, 512)), const((1, 128)), const((LANE, LANE))],
        out_specs=[row(w) for _, w, _ in outs],
        out_shape=[jax.ShapeDtypeStruct((n, w), dt) for _, w, dt in outs],
        compiler_params=pltpu.CompilerParams(dimension_semantics=("arbitrary",), vmem_limit_bytes=VMEM_LIMIT),
        name="proj",
    )(x2d, norm_g[None, :], w_head, w_tail, cq_cos, cq_sin, ck_cos, ck_sin, gaq, gak, gbq, gbk, g64)
    return {name: r for (name, _, _), r in zip(outs, res)}


def _t5_bucket(rel):
    nb = NUM_BUCKETS // 2
    max_exact = nb // 2
    ret = jnp.where(rel > 0, nb, 0)
    n = jnp.abs(rel)
    large = max_exact + (jnp.log(jnp.maximum(n, 1).astype(F32) / max_exact)
                         / math.log(MAX_DISTANCE / max_exact) * (nb - max_exact)).astype(jnp.int32)
    large = jnp.minimum(large, nb - 1)
    return ret + jnp.where(n < max_exact, n, large)


def _bias_slabs(bias_cols):
    i = jnp.arange(TQ, dtype=jnp.int32)[:, None]
    j = jnp.arange(LANE, dtype=jnp.int32)[None, :]
    vis0 = j < (i // CHUNK + 1) * CHUNK

    def lookup(rel):
        bucket = _t5_bucket(rel)
        out = jnp.zeros((bias_cols.shape[1],) + rel.shape, F32)
        for b in range(NUM_BUCKETS):
            out = jnp.where(bucket[None] == b, bias_cols[b].astype(F32)[:, None, None], out)
        return out

    far = lookup(-2 * LANE + j - i)
    c = far[:, :1, :1]
    tiles = [(far - c) * LOG2E, (lookup(-LANE + j - i) - c) * LOG2E,
             jnp.where(vis0[None], (lookup(j - i) - c) * LOG2E, NEG), jnp.full_like(far, NEG)]
    zero = jnp.zeros((TQ, LANE), F32)
    masks = [zero, zero, jnp.where(vis0, 0.0, -jnp.inf).astype(F32), jnp.full((TQ, LANE), -jnp.inf, F32)]
    return jnp.stack(tiles, axis=1), jnp.stack(masks, axis=0)


def _tile_types(qb, nkt):
    return [jnp.clip(kt - qb, -2, 1) + 2 for kt in range(nkt)]


def _variants(nqb, qb0, seg_rows):
    if len(seg_rows) > 1 or nqb == 1:
        assert nqb == 1
        return [(0, 1, tuple(seg_rows), max(0, qb0 - 1))]
    nvar = min(4, nqb)
    assert nqb % nvar == 0
    grp = nqb // nvar
    nkt_total = seg_rows[0] // LANE
    return [(v * grp, (v + 1) * grp, (min(nkt_total, qb0 + (v + 1) * grp) * LANE,), max(0, qb0 + v * grp - 1))
            for v in range(nvar)]


def _run_variants(qi, variants, body):
    if len(variants) == 1:
        body(variants[0][2], variants[0][3])
    else:
        for lo_q, hi_q, rows, n_far in variants:
            pl.when((qi >= lo_q) & (qi < hi_q))(functools.partial(body, rows, n_far))


def _head_stack(qp, lo, kw):
    if kw == LANE:
        zero = jnp.zeros_like(qp)
        return jnp.concatenate([jnp.where(lo, qp, zero), jnp.where(lo, zero, qp)], axis=0)
    return jnp.concatenate([qp[:, :kw], qp[:, kw:]], axis=0)


def _stack2(a, b):
    return jnp.concatenate([a, b], axis=0)


def _softmax_pv(parts, v_blocks):
    s2 = parts[0] if len(parts) == 1 else jnp.concatenate(parts, axis=1)
    m = jnp.max(s2, axis=1, keepdims=True)
    p = jnp.exp2(s2 - m)
    l = jnp.sum(p, axis=1, keepdims=True)
    pb = p.astype(BF16)
    dv = min(v.shape[0] if t else v.shape[1] for v, t in v_blocks)
    o, c0 = None, 0
    for v, t in v_blocks:
        rows = v.shape[1] if t else v.shape[0]
        pc = pb[:, c0:c0 + rows]
        term = (_dot_nt(pc, v) if t else _dot(pc, v))[:, :dv]
        o = term if o is None else o + term
        c0 += rows
    return o * (1.0 / l)


def _attn_a_kernel(*refs, tq, qb0, variants, lam_init, seg_kt):
    aq_ref, sag_ref, slab_ref, alam_ref, subg_ref = refs[:5]
    nseg = len(seg_kt)
    segs = [refs[5 + 2 * i:7 + 2 * i] for i in range(nseg)]
    out_ref = refs[5 + 2 * nseg]
    qi = pl.program_id(1)
    qb = qi + qb0
    lp = alam_ref[...]
    lam = (jnp.exp(jnp.sum(lp[0:1] * lp[1:2], axis=1, keepdims=True))
           - jnp.exp(jnp.sum(lp[2:3] * lp[3:4], axis=1, keepdims=True)) + lam_init)
    lo = lax.broadcasted_iota(jnp.int32, (tq, LANE), 1) < DH_A

    def body(seg_rows, n_far):
        nkt = sum(r // LANE for r in seg_rows)
        ty = _tile_types(qb, nkt)
        for h in range(H_A):
            hs = slice(h * LANE, (h + 1) * LANE)
            q2 = _head_stack(aq_ref[0, :, hs], lo, LANE)
            parts, kt = [], 0
            for (k_ref, _), rows, kt_major in zip(segs, seg_rows, seg_kt):
                s = _dot(q2, k_ref[0, hs, :rows]) if kt_major else _dot_nt(q2, k_ref[0, :rows, hs])
                n_plain = min(max(n_far - kt, 0), rows // LANE)
                if n_plain:
                    parts.append(s[:, :n_plain * LANE])
                for j in range(n_plain, rows // LANE):
                    b = slab_ref[h, ty[kt + j], :tq]
                    parts.append(s[:, j * LANE:(j + 1) * LANE] + _stack2(b, b))
                kt += rows // LANE
            o2 = _softmax_pv(parts, [(v_ref[0, :rows, hs], False) for (_, v_ref), rows in zip(segs, seg_rows)])
            o = o2[:tq] - lam * o2[tq:]
            o = o * lax.rsqrt(jnp.mean(o * o, axis=1, keepdims=True) + EPS) * subg_ref[...] * (1.0 - lam_init)
            out_ref[0, :, hs] = (o * sag_ref[0, :, hs].astype(F32)).astype(BF16)

    _run_variants(qi, variants, body)


def _seg_spec(a, layer):
    if layer is None:
        return pl.BlockSpec((1,) + a.shape[1:], lambda bi, qi: (bi, 0, 0))
    return pl.BlockSpec((None, 1) + a.shape[2:], lambda bi, qi, layer=layer: (layer, bi, 0, 0))


def _attn_a(aq, sag, slab, a_lambda, subln_g, segs, *, tq, qb0, lam_init):
    b, t, _ = aq.shape
    nqb = t // tq
    seg_kt = tuple(kt for _, _, kt, _ in segs)
    seg_rows = [k.shape[-1] if kt else k.shape[-2] for k, _, kt, _ in segs]
    variants = _variants(nqb, qb0, seg_rows)
    kern = functools.partial(_attn_a_kernel, tq=tq, qb0=qb0, variants=variants, lam_init=lam_init, seg_kt=seg_kt)
    qspec = pl.BlockSpec((1, tq, 512), lambda bi, qi: (bi, qi, 0))
    seg_specs, seg_args = [], []
    for k, v, _, layer in segs:
        seg_specs += [_seg_spec(k, layer), _seg_spec(v, layer)]
        seg_args += [k, v]
    return pl.pallas_call(
        kern,
        grid=(b, nqb),
        in_specs=[qspec, qspec,
                  pl.BlockSpec(slab.shape, lambda bi, qi: (0, 0, 0, 0)),
                  pl.BlockSpec((4, DH_A), lambda bi, qi: (0, 0)),
                  pl.BlockSpec((1, DV_A), lambda bi, qi: (0, 0))] + seg_specs,
        out_specs=qspec,
        out_shape=jax.ShapeDtypeStruct((b, t, 512), BF16),
        compiler_params=pltpu.CompilerParams(dimension_semantics=("arbitrary", "arbitrary"),
                                             vmem_limit_bytes=VMEM_LIMIT),
        name="attn_a",
    )(aq, sag, slab, a_lambda, subln_g[None, :], *seg_args)


def _dsa_kernel(*refs, tq, qb0, variants, topk, seg_t):
    bq_ref, bqi_ref, wif_ref, sbg_ref, slab_ref, mslab_ref, tri_ref = refs[:7]
    nseg = len(seg_t)
    segs = [refs[7 + 3 * i:10 + 3 * i] for i in range(nseg)]
    out_ref, keys_ref = refs[7 + 3 * nseg:]
    qi = pl.program_id(1)
    qb = qi + qb0
    lo = lax.broadcasted_iota(jnp.int32, (tq, LANE), 1) < DH_B
    kf = float(topk)

    def body(seg_rows, n_far):
        nkt = sum(r // LANE for r in seg_rows)
        ty = _tile_types(qb, nkt)

        wi = wif_ref[0]
        kt0 = 0
        for (_, _, ki_ref), rows, tr in zip(segs, seg_rows, seg_t):
            ki = (ki_ref[0, :, :rows] if tr else ki_ref[0, :rows, :]).astype(BF16)
            kw = ki.shape[0] if tr else ki.shape[1]
            qis = jnp.concatenate([_head_stack(bqi_ref[0, :, hp * LANE:(hp + 1) * LANE], lo, kw)
                                   for hp in range(H_IDX // 2)], axis=0)
            r = jnp.maximum(_dot(qis, ki) if tr else _dot_nt(qis, ki), 0.0)
            score = wi[:, 0:1] * r[0:tq]
            for h in range(1, H_IDX):
                score = score + wi[:, h:h + 1] * r[h * tq:(h + 1) * tq]
            score = score + jnp.concatenate([mslab_ref[ty[kt0 + j], :tq] for j in range(rows // LANE)], axis=1)
            bits = pltpu.bitcast(score, jnp.int32)
            keys_ref[:, kt0 * LANE:kt0 * LANE + rows] = bits ^ ((bits >> 31) & jnp.int32(0x7FFFFFFF))
            kt0 += rows // LANE

        def count(pred):
            acc = None
            for kt in range(nkt):
                ind = jnp.where(pred(keys_ref[:, kt * LANE:(kt + 1) * LANE]), 1.0, 0.0)
                acc = ind if acc is None else acc + ind
            return jnp.sum(acc, axis=1, keepdims=True)

        def search(i, t):
            cand = t + lax.shift_left(jnp.int32(1), 31 - i)
            return jnp.where(count(lambda x: x >= cand) >= kf, cand, t)

        t = lax.fori_loop(0, 32, search, jnp.full((tq, 1), INT_MIN, jnp.int32), unroll=SEARCH_UNROLL)

        need = kf - count(lambda x: x > t)
        run = jnp.zeros((tq, 1), F32)
        negm = []
        for kt in range(nkt):
            kt_keys = keys_ref[:, kt * LANE:(kt + 1) * LANE]
            eq = jnp.where(kt_keys == t, 1.0, 0.0)
            rank = _dot(eq.astype(BF16), tri_ref[...]) + run
            tie = jnp.where(rank <= need, eq, 0.0)
            negm.append(jnp.where(kt_keys > t, 0.0, (1.0 - tie) * NEG))
            run = run + jnp.sum(eq, axis=1, keepdims=True)
        far2 = [_stack2(negm[kt], negm[kt]) for kt in range(n_far)]

        for hp in range(H_B // 2):
            hs = slice(hp * LANE, (hp + 1) * LANE)
            qp = bq_ref[0, :, hs]
            parts, vs, kt = [], [], 0
            for (k_ref, v_ref, _), rows, tr in zip(segs, seg_rows, seg_t):
                k = (k_ref[0, :, :rows] if tr else k_ref[0, :rows, :]).astype(BF16)
                q2 = _head_stack(qp, lo, k.shape[0] if tr else k.shape[1])
                s = _dot(q2, k) if tr else _dot_nt(q2, k)
                for j in range(rows // LANE):
                    if kt < n_far:
                        nb2 = far2[kt]
                    else:
                        nb2 = _stack2(negm[kt] + slab_ref[2 * hp, ty[kt], :tq],
                                      negm[kt] + slab_ref[2 * hp + 1, ty[kt], :tq])
                    parts.append(s[:, j * LANE:(j + 1) * LANE] + nb2)
                    kt += 1
                vs.append(((v_ref[0, :, :rows] if tr else v_ref[0, :rows, :]).astype(BF16), tr))
            o2 = _softmax_pv(parts, vs)
            if o2.shape[1] == LANE:
                o = jnp.where(lo, o2[:tq], o2[tq:])
            else:
                o = jnp.concatenate([o2[:tq], o2[tq:]], axis=1)
            out_ref[0, :, hs] = (o * sbg_ref[0, :, hs].astype(F32)).astype(BF16)

    _run_variants(qi, variants, body)


def _dsa(bq, bqi, wif, sbg, slab, mslab, tri, segs, *, tq, qb0, topk):
    b, t, _ = bq.shape
    nqb = t // tq
    seg_t = tuple(tr for _, _, _, tr, _ in segs)
    seg_rows = [s[0].shape[-1] if s[3] else s[0].shape[-2] for s in segs]
    variants = _variants(nqb, qb0, seg_rows)
    kern = functools.partial(_dsa_kernel, tq=tq, qb0=qb0, variants=variants, topk=topk, seg_t=seg_t)

    def qspec(w):
        return pl.BlockSpec((1, tq, w), lambda bi, qi: (bi, qi, 0))

    seg_specs, seg_args = [], []
    for k, v, ki, _, layer in segs:
        for a in (k, v, ki):
            seg_specs.append(_seg_spec(a, layer))
            seg_args.append(a)
    return pl.pallas_call(
        kern,
        grid=(b, nqb),
        in_specs=[qspec(512), qspec(256), qspec(LANE), qspec(512),
                  pl.BlockSpec(slab.shape, lambda bi, qi: (0, 0, 0, 0)),
                  pl.BlockSpec(mslab.shape, lambda bi, qi: (0, 0, 0)),
                  pl.BlockSpec((LANE, LANE), lambda bi, qi: (0, 0))] + seg_specs,
        out_specs=qspec(512),
        out_shape=jax.ShapeDtypeStruct((b, t, 512), BF16),
        scratch_shapes=[pltpu.VMEM((tq, sum(seg_rows)), jnp.int32)],
        compiler_params=pltpu.CompilerParams(dimension_semantics=("arbitrary", "arbitrary"),
                                             vmem_limit_bytes=VMEM_LIMIT),
        name="dsa",
    )(bq, bqi, wif, sbg, slab, mslab, tri, *seg_args)


def _ret_kernel(cq_ref, ck_ref, cv_ref, scg_ref, st_ref, dmat_ref, dq_ref, dk_ref, gc_ref, bd_ref, gn_ref,
                out_ref, sto_ref, *, c, nchunks):
    lane = lax.broadcasted_iota(jnp.int32, (c, LANE), 1)
    lo = lane < DK_C
    for p in range(H_C // 2):
        st = st_ref[0, p]
        for ci in range(nchunks):
            rows = slice(ci * c, (ci + 1) * c)
            q = cq_ref[0, rows, p * LANE:(p + 1) * LANE]
            k = ck_ref[0, rows, p * LANE:(p + 1) * LANE]
            v = cv_ref[0, rows, p * 2 * DV_C:(p + 1) * 2 * DV_C]
            qd = (q.astype(F32) * dq_ref[p]).astype(BF16)
            cross = _dot(qd, st.astype(BF16))
            zero = jnp.zeros_like(q)
            for j in range(2):
                h = 2 * p + j
                qm = jnp.where(lo, q, zero) if j == 0 else jnp.where(lo, zero, q)
                a = (_dot_nt(qm, k) * dmat_ref[h]).astype(BF16)
                o = cross[:, j * DV_C:(j + 1) * DV_C] + _dot(a, v[:, j * DV_C:(j + 1) * DV_C])
                o = o * lax.rsqrt(jnp.mean(o * o, axis=1, keepdims=True) + EPS) * gn_ref[...]
                gate = scg_ref[0, rows, h * DV_C:(h + 1) * DV_C].astype(F32)
                out_ref[0, rows, h * DV_C:(h + 1) * DV_C] = (o * gate).astype(BF16)
            kd = (k.astype(F32) * dk_ref[p]).astype(BF16)
            st = (gc_ref[p] * st + _dot_tn(kd, v)) * bd_ref[...]
        sto_ref[0, p] = st


def _retention(cq, ck, cv, scg, state_pairs, log_gamma, gn_g, *, c):
    b, t, _ = cq.shape
    nchunks = t // c
    n = jnp.arange(c, dtype=F32)
    diff = n[:, None] - n[None, :]
    dmat = jnp.where(diff >= 0, jnp.exp(log_gamma[:, None, None] * jnp.maximum(diff, 0.0)[None]), 0.0)
    decay_q = jnp.exp((n[:, None] + 1.0) * log_gamma[None, :])
    decay_k = jnp.exp((c - 1.0 - n)[:, None] * log_gamma[None, :])
    decay_c = jnp.exp(c * log_gamma)

    def lanes(tab):
        return jnp.transpose(jnp.repeat(tab, DK_C, axis=1).reshape(c, H_C // 2, LANE), (1, 0, 2))

    gc = jnp.broadcast_to(jnp.repeat(decay_c, DK_C).reshape(H_C // 2, LANE, 1), (H_C // 2, LANE, 2 * DV_C))
    bd = (jnp.arange(LANE)[:, None] // DK_C == jnp.arange(2 * DV_C)[None, :] // DV_C).astype(F32)

    def full(a):
        return pl.BlockSpec(a.shape, lambda bi: (0,) * a.ndim)

    def tok(w):
        return pl.BlockSpec((1, t, w), lambda bi: (bi, 0, 0))

    stspec = pl.BlockSpec((1, H_C // 2, LANE, 2 * DV_C), lambda bi: (bi, 0, 0, 0))
    dq, dk, gnv = lanes(decay_q), lanes(decay_k), gn_g[None, :]
    return pl.pallas_call(
        functools.partial(_ret_kernel, c=c, nchunks=nchunks),
        grid=(b,),
        in_specs=[tok(256), tok(256), tok(512), tok(512), stspec,
                  full(dmat), full(dq), full(dk), full(gc), full(bd), full(gnv)],
        out_specs=[tok(512), stspec],
        out_shape=[jax.ShapeDtypeStruct((b, t, 512), BF16),
                   jax.ShapeDtypeStruct((b, H_C // 2, LANE, 2 * DV_C), F32)],
        compiler_params=pltpu.CompilerParams(dimension_semantics=("arbitrary",), vmem_limit_bytes=VMEM_LIMIT),
        name="retention",
    )(cq, ck, cv, scg, state_pairs, dmat, dq, dk, gc, bd, gnv)


def _state_to_pairs(st):
    b = st.shape[0]
    s4 = st.reshape(b, H_C // 2, 2, DK_C, DV_C)
    z = jnp.zeros_like(s4[:, :, 0])
    top = jnp.concatenate([s4[:, :, 0], z], axis=-1)
    bot = jnp.concatenate([z, s4[:, :, 1]], axis=-1)
    return jnp.concatenate([top, bot], axis=-2)


def _pairs_to_state(sp):
    b = sp.shape[0]
    s0 = sp[:, :, :DK_C, :DV_C]
    s1 = sp[:, :, DK_C:, DV_C:]
    return jnp.stack([s0, s1], axis=2).reshape(b, H_C, DK_C, DV_C)


def _merge_kernel(x_ref, ba_ref, bb_ref, bc_ref, smg_ref, wb_ref, wo_ref, y_ref):
    m = None
    for n, br in enumerate((ba_ref, bb_ref, bc_ref)):
        proj = _dot(br[...], wb_ref[n])
        term = smg_ref[:, n * D_MODEL:(n + 1) * D_MODEL].astype(F32) * proj
        m = term if m is None else m + term
    y_ref[...] = x_ref[...] + _dot(m.astype(BF16), wo_ref[...])


def _merge(x2d, br_a, br_b, br_c, smg, wb, wo, tm):
    n = x2d.shape[0]

    def row(w):
        return pl.BlockSpec((tm, w), lambda i: (i, 0))

    return pl.pallas_call(
        _merge_kernel,
        grid=(n // tm,),
        in_specs=[row(D_MODEL), row(W_BR), row(W_BR), row(W_BR), row(N_BRANCH * D_MODEL),
                  pl.BlockSpec((N_BRANCH, W_BR, D_MODEL), lambda i: (0, 0, 0)),
                  pl.BlockSpec((D_MODEL, D_MODEL), lambda i: (0, 0))],
        out_specs=row(D_MODEL),
        out_shape=jax.ShapeDtypeStruct((n, D_MODEL), F32),
        compiler_params=pltpu.CompilerParams(dimension_semantics=("arbitrary",), vmem_limit_bytes=VMEM_LIMIT),
        name="merge",
    )(x2d, br_a, br_b, br_c, smg, wb, wo)


def _pad_rows(a, rows):
    return jnp.pad(a, ((0, 0), (0, rows - a.shape[1]), (0, 0)))


def kernel(x_prompt, x_sample, cache_a_k, cache_a_v, cache_b_k, cache_b_v, cache_b_kidx, state_c, rel_bias,
           norm_g, w_in, a_qk_g, a_lambda, a_subln_g, b_qk_g, c_gn_g, w_branch, w_out):
    bp, tp, _ = x_prompt.shape
    bs, ts, _ = x_sample.shape
    depth = w_in.shape[0]
    past = cache_a_k.shape[2]
    assert tp % TQ == 0 and ts == CHUNK and past % LANE == 0
    pos_p = jnp.arange(tp, dtype=jnp.int32)
    pos_s = past + jnp.arange(ts, dtype=jnp.int32)
    topk_p = min(TOPK_MAX, tp // 4)
    topk_s = min(TOPK_MAX, (past + ts) // 4)
    log_gamma = jnp.log(1.0 - 2.0 ** (-5.0 - jnp.arange(H_C, dtype=F32)))

    slab_a, mslab = _bias_slabs(rel_bias[:, :H_A])
    slab_b, _ = _bias_slabs(rel_bias[:, H_A:])
    lane_i = jnp.arange(LANE)
    g64 = (lane_i[:, None] // 64 == lane_i[None, :] // 64).astype(BF16)
    tri = (lane_i[:, None] <= lane_i[None, :]).astype(BF16)
    c_prompt = 256 if tp % 256 == 0 else CHUNK
    tm_p = 512
    tm_s = 256 if (bs * ts) % 256 == 0 else ts
    qb_s = past // LANE
    ka_t = jnp.transpose(cache_a_k, (0, 1, 3, 4, 5, 2)).reshape(depth, bs, 512, past).astype(BF16)
    va = cache_a_v.reshape(depth, bs, past, 512).astype(BF16)
    kb_t, vb_t, kib_t = (jnp.transpose(c, (0, 1, 3, 2)) for c in (cache_b_k, cache_b_v, cache_b_kidx))

    yp = x_prompt.reshape(bp * tp, D_MODEL)
    ys = x_sample.reshape(bs * ts, D_MODEL)
    outs_p = {k: [] for k in ("ak", "av", "bk", "bv", "bki", "sc")}
    outs_s = {k: [] for k in ("ak", "av", "bk", "bv", "bki", "sc")}
    for l in range(depth):
        lam_init = 0.8 - 0.6 * math.exp(-0.3 * l)
        w_head = w_in[l, :, :HEAD_COLS].astype(BF16)
        w_tail = w_in[l, :, TAIL_START:TAIL_START + TAIL_COLS].astype(BF16)
        wb = w_branch[l].astype(BF16)
        wo = w_out[l].astype(BF16)

        P = _project(yp, pos_p, tm_p, norm_g[l], w_head, w_tail, a_qk_g[l], b_qk_g[l], g64)
        r3 = lambda a: a.reshape(bp, tp, a.shape[-1])
        br_a = _attn_a(r3(P["aq"]), r3(P["sag"]), slab_a, a_lambda[l], a_subln_g[l],
                       [(r3(P["akb"]), r3(P["avb"]), False, None)], tq=TQ, qb0=0, lam_init=lam_init)
        br_b = _dsa(r3(P["bq"]), r3(P["bqi"]), r3(P["wif"]), r3(P["sbg"]), slab_b, mslab, tri,
                    [(r3(P["kkb"]), r3(P["vvb"]), r3(P["kib"]), False, None)], tq=TQ, qb0=0, topk=topk_p)
        st0 = jnp.zeros((bp, H_C // 2, LANE, 2 * DV_C), F32)
        br_c, stp = _retention(r3(P["cq"]), r3(P["ck"]), r3(P["cv"]), r3(P["scg"]), st0, log_gamma, c_gn_g[l],
                               c=c_prompt)
        yp = _merge(yp, br_a.reshape(bp * tp, W_BR), br_b.reshape(bp * tp, W_BR), br_c.reshape(bp * tp, W_BR),
                    P["smg"], wb, wo, tm_p)
        outs_p["ak"].append(P["ak"].reshape(bp, tp, H_A, 2, DH_A))
        outs_p["av"].append(P["av"].reshape(bp, tp, H_A, DV_A))
        outs_p["bk"].append(P["bk"].reshape(bp, tp, DH_B))
        outs_p["bv"].append(P["bv"].reshape(bp, tp, DH_B))
        outs_p["bki"].append(P["bki"].reshape(bp, tp, D_IDX))
        outs_p["sc"].append(_pairs_to_state(stp))

        S = _project(ys, pos_s, tm_s, norm_g[l], w_head, w_tail, a_qk_g[l], b_qk_g[l], g64)
        r3s = lambda a: a.reshape(bs, ts, a.shape[-1])
        rk = lambda a: _pad_rows(r3s(a), LANE)
        br_a = _attn_a(r3s(S["aq"]), r3s(S["sag"]), slab_a, a_lambda[l], a_subln_g[l],
                       [(ka_t, va, True, l), (rk(S["akb"]), rk(S["avb"]), False, None)],
                       tq=ts, qb0=qb_s, lam_init=lam_init)
        br_b = _dsa(r3s(S["bq"]), r3s(S["bqi"]), r3s(S["wif"]), r3s(S["sbg"]), slab_b, mslab, tri,
                    [(kb_t, vb_t, kib_t, True, l), (rk(S["kkb"]), rk(S["vvb"]), rk(S["kib"]), False, None)],
                    tq=ts, qb0=qb_s, topk=topk_s)
        br_c, sts = _retention(r3s(S["cq"]), r3s(S["ck"]), r3s(S["cv"]), r3s(S["scg"]),
                               _state_to_pairs(state_c[l].astype(F32)), log_gamma, c_gn_g[l], c=ts)
        ys = _merge(ys, br_a.reshape(bs * ts, W_BR), br_b.reshape(bs * ts, W_BR), br_c.reshape(bs * ts, W_BR),
                    S["smg"], wb, wo, tm_s)
        outs_s["ak"].append(S["ak"].reshape(bs, ts, H_A, 2, DH_A))
        outs_s["av"].append(S["av"].reshape(bs, ts, H_A, DV_A))
        outs_s["bk"].append(S["bk"].reshape(bs, ts, DH_B))
        outs_s["bv"].append(S["bv"].reshape(bs, ts, DH_B))
        outs_s["bki"].append(S["bki"].reshape(bs, ts, D_IDX))
        outs_s["sc"].append(_pairs_to_state(sts).astype(state_c.dtype))

    order = ("ak", "av", "bk", "bv", "bki", "sc")
    return ((yp.reshape(bp, tp, D_MODEL), ys.reshape(bs, ts, D_MODEL))
            + tuple(jnp.stack(outs_p[k]) for k in order)
            + tuple(jnp.stack(outs_s[k]) for k in order))
```

```python
import functools
import math

import jax
import jax.numpy as jnp
from jax import lax
from jax.experimental import pallas as pl
from jax.experimental.pallas import tpu as pltpu

D_MODEL = 1024
CHUNK = 64
EPS = 1e-6
H_A = 4
DH_A = 64
DV_A = 128
H_B = 8
DH_B = 64
H_IDX = 4
D_IDX = 64
TOPK_MAX = 256
H_C = 4
DK_C = 64
DV_C = 128
W_BR = 512
N_BRANCH = 3
NUM_BUCKETS = 32
MAX_DISTANCE = 128
ROPE_BASE = 10000.0

LANE = 128
TQ = 128
NEG = -1e30
LOG2E = math.log2(math.e)
SEARCH_UNROLL = 8
INT_MIN = -(2 ** 31)
F32 = jnp.float32
BF16 = jnp.bfloat16
VMEM_LIMIT = 56 * 1024 * 1024

_SEC = {"aq": (0, 0, 512), "ak": (0, 512, 512), "av": (0, 1024, 512), "ag": (0, 1536, 512), "bq": (0, 2048, 512),
        "kv": (0, 2560, 128), "bqi": (0, 2688, 256), "kiw": (0, 2944, 128),
        "bg": (1, 0, 512), "cq": (1, 512, 256), "ck": (1, 768, 256), "cv": (1, 1024, 512), "cg": (1, 1536, 512),
        "mg": (1, 2048, 3072)}
HEAD_COLS = 3072
TAIL_START = 3012
TAIL_COLS = 5120


def _dot(a, b):
    return jnp.dot(a, b, preferred_element_type=F32)


def _dot_nt(a, b):
    return lax.dot_general(a, b, (((1,), (1,)), ((), ())), preferred_element_type=F32)


def _dot_tn(a, b):
    return lax.dot_general(a, b, (((0,), (0,)), ((), ())), preferred_element_type=F32)


def _proj_kernel(x_ref, g_ref, wh_ref, wt_ref, cq_cos_ref, cq_sin_ref, ck_cos_ref, ck_sin_ref,
                 gaq_ref, gak_ref, gbq_ref, gbk_ref, g64_ref,
                 aq_ref, ak_ref, akb_ref, av_ref, avb_ref, sag_ref, bq_ref, bqi_ref,
                 bk_ref, kkb_ref, bv_ref, vvb_ref, bki_ref, kib_ref, wif_ref, sbg_ref,
                 cq_ref, ck_ref, cv_ref, scg_ref, smg_ref, *, seq_minor):
    x = x_ref[...]
    ms = jnp.mean(x * x, axis=-1, keepdims=True)
    xn = ((x * lax.rsqrt(ms + EPS)) * g_ref[...]).astype(BF16)
    w_refs = (wh_ref, wt_ref)

    def mm(name, sub=None):
        op, c0, n = _SEC[name]
        if sub is not None:
            c0, n = c0 + sub[0], sub[1]
        return _dot(xn, w_refs[op][:, c0:c0 + n])

    def group_norm64(z, gain):
        zz = (z * z).astype(BF16)
        parts = [_dot(zz[:, j * LANE:(j + 1) * LANE], g64_ref[...]) for j in range(z.shape[1] // LANE)]
        ss = parts[0] if len(parts) == 1 else jnp.concatenate(parts, axis=1)
        return (z * lax.rsqrt(ss * (1.0 / 64.0) + EPS)) * gain

    lane = lax.broadcasted_iota(jnp.int32, (x.shape[0], LANE), 1)
    lo = lane < 64

    def rotate_half(z):
        first = (lane & 63) < DK_C // 2
        cols = []
        for j in range(z.shape[1] // LANE):
            zj = z[:, j * LANE:(j + 1) * LANE]
            cols.append(jnp.where(first, -pltpu.roll(zj, LANE - DK_C // 2, 1), pltpu.roll(zj, DK_C // 2, 1)))
        return jnp.concatenate(cols, axis=1)

    aq_ref[...] = group_norm64(mm("aq"), gaq_ref[...]).astype(BF16)
    ak = group_norm64(mm("ak"), gak_ref[...])
    akb_ref[...] = ak.astype(BF16)
    av = mm("av")
    avb_ref[...] = av.astype(BF16)
    if seq_minor:
        ak_ref[0] = jnp.transpose(ak)
        for h in range(H_A):
            av_ref[pl.ds(h, av.shape[0], stride=H_A), :] = av[:, h * DV_A:(h + 1) * DV_A]
    else:
        ak_ref[...] = ak
        av_ref[...] = av
    z = mm("ag")
    sag_ref[...] = (z * jax.nn.sigmoid(z)).astype(BF16)
    bq_ref[...] = group_norm64(mm("bq"), gbq_ref[...]).astype(BF16)
    bqi_ref[...] = mm("bqi").astype(BF16)
    z = mm("kv")
    kv = jnp.where(lo, group_norm64(z, gbk_ref[...]), z)
    vk = pltpu.roll(kv, 64, 1)
    if seq_minor:
        kv_t = jnp.transpose(kv)
        bk_ref[0] = kv_t[:DH_B]
        bv_ref[0] = kv_t[DH_B:]
    else:
        bk_ref[...] = kv[:, :DH_B]
        bv_ref[...] = vk[:, :DH_B]
    kkb_ref[...] = jnp.where(lo, kv, vk).astype(BF16)
    vvb_ref[...] = jnp.where(lo, vk, kv).astype(BF16)
    z = mm("kiw")
    zr = pltpu.roll(z, 64, 1)
    if seq_minor:
        bki_ref[0] = jnp.transpose(z)[:D_IDX]
    else:
        bki_ref[...] = z[:, :D_IDX]
    kib_ref[...] = jnp.where(lo, z, zr).astype(BF16)
    wif_ref[...] = zr
    z = mm("bg")
    sbg_ref[...] = (z * jax.nn.sigmoid(z)).astype(BF16)
    z = mm("cq")
    cq_ref[...] = (z * cq_cos_ref[...] + rotate_half(z) * cq_sin_ref[...]).astype(BF16)
    z = mm("ck")
    ck_ref[...] = (z * ck_cos_ref[...] + rotate_half(z) * ck_sin_ref[...]).astype(BF16)
    cv_ref[...] = mm("cv").astype(BF16)
    z = mm("cg")
    scg_ref[...] = (z * jax.nn.sigmoid(z)).astype(BF16)
    for j in range(N_BRANCH * 2):
        z = mm("mg", (j * 512, 512))
        smg_ref[:, j * 512:(j + 1) * 512] = jax.nn.sigmoid(z).astype(BF16)


def _rope_tables(pos, rows):
    half = DK_C // 2
    inv = ROPE_BASE ** (-jnp.arange(half, dtype=F32) / half)
    ang = pos.astype(F32)[:, None] * inv[None, :]
    cos = jnp.tile(jnp.cos(ang), (1, 2 * H_C))
    sin = jnp.tile(jnp.sin(ang), (1, 2 * H_C))
    reps = max(1, rows // cos.shape[0])
    cos = jnp.tile(cos, (reps, 1))
    sin = jnp.tile(sin, (reps, 1))
    kscale = DK_C ** -0.5
    return cos, sin, cos * kscale, sin * kscale


def _project(x2d, pos, tm, norm_g, w_head, w_tail, a_qk_g, b_qk_g, g64, seq_minor):
    n = x2d.shape[0]
    seq = pos.shape[0]
    assert not seq_minor or seq % tm == 0
    nt = max(1, seq // tm)
    cq_cos, cq_sin, ck_cos, ck_sin = _rope_tables(pos, tm)
    ntab = cq_cos.shape[0] // tm
    gaq = jnp.tile(a_qk_g[0], 8)[None, :] * (LOG2E * DH_A ** -0.5)
    gak = jnp.tile(a_qk_g[1], 8)[None, :]
    gbq = jnp.tile(b_qk_g[0], 8)[None, :] * (LOG2E * DH_B ** -0.5)
    gbk = jnp.tile(b_qk_g[1], 2)[None, :]

    def row(width):
        return pl.BlockSpec((tm, width), lambda i: (i, 0))

    def const(shape, single=False):
        if single:
            return pl.BlockSpec(shape, lambda i: (0, 0), pipeline_mode=pl.Buffered(1))
        return pl.BlockSpec(shape, lambda i: (0, 0))

    tab = pl.BlockSpec((tm, 256), lambda i: (i % ntab, 0))
    outs = [("aq", 512, BF16), ("ak", 512, F32), ("akb", 512, BF16), ("av", 512, F32), ("avb", 512, BF16),
            ("sag", 512, BF16), ("bq", 512, BF16), ("bqi", 256, BF16), ("bk", 64, F32), ("kkb", 128, BF16),
            ("bv", 64, F32), ("vvb", 128, BF16), ("bki", 64, F32), ("kib", 128, BF16), ("wif", 128, F32),
            ("sbg", 512, BF16), ("cq", 256, BF16), ("ck", 256, BF16), ("cv", 512, BF16), ("scg", 512, BF16),
            ("smg", 3072, BF16)]
    def out_spec(name, w):
        if seq_minor and name in ("ak", "bk", "bv", "bki"):
            return pl.BlockSpec((1, w, tm), lambda i: (i // nt, 0, i % nt))
        if seq_minor and name == "av":
            return pl.BlockSpec((tm * H_A, DV_A), lambda i: (i, 0))
        return row(w)

    def out_shape(name, w, dt):
        if seq_minor and name in ("ak", "bk", "bv", "bki"):
            return jax.ShapeDtypeStruct((n // seq, w, seq), dt)
        if seq_minor and name == "av":
            return jax.ShapeDtypeStruct((n * H_A, DV_A), dt)
        return jax.ShapeDtypeStruct((n, w), dt)

    res = pl.pallas_call(
        functools.partial(_proj_kernel, seq_minor=seq_minor),
        grid=(n // tm,),
        in_specs=[row(D_MODEL), const((1, D_MODEL)), const((D_MODEL, HEAD_COLS), single=True),
                  const((D_MODEL, TAIL_COLS), single=True),
                  tab, tab, tab, tab,
                  const((1, 512)), const((1, 512)), const((1, 512)), const((1, 128)), const((LANE, LANE))],
        out_specs=[out_spec(name, w) for name, w, _ in outs],
        out_shape=[out_shape(name, w, dt) for name, w, dt in outs],
        compiler_params=pltpu.CompilerParams(dimension_semantics=("arbitrary",), vmem_limit_bytes=VMEM_LIMIT),
        name="proj",
    )(x2d, norm_g[None, :], w_head, w_tail, cq_cos, cq_sin, ck_cos, ck_sin, gaq, gak, gbq, gbk, g64)
    return {name: r for (name, _, _), r in zip(outs, res)}


def _t5_bucket(rel):
    nb = NUM_BUCKETS // 2
    max_exact = nb // 2
    ret = jnp.where(rel > 0, nb, 0)
    n = jnp.abs(rel)
    large = max_exact + (jnp.log(jnp.maximum(n, 1).astype(F32) / max_exact)
                         / math.log(MAX_DISTANCE / max_exact) * (nb - max_exact)).astype(jnp.int32)
    large = jnp.minimum(large, nb - 1)
    return ret + jnp.where(n < max_exact, n, large)


def _bias_slabs(bias_cols):
    i = jnp.arange(TQ, dtype=jnp.int32)[:, None]
    j = jnp.arange(LANE, dtype=jnp.int32)[None, :]
    vis0 = j < (i // CHUNK + 1) * CHUNK

    def lookup(rel):
        bucket = _t5_bucket(rel)
        out = jnp.zeros((bias_cols.shape[1],) + rel.shape, F32)
        for b in range(NUM_BUCKETS):
            out = jnp.where(bucket[None] == b, bias_cols[b].astype(F32)[:, None, None], out)
        return out

    far = lookup(-2 * LANE + j - i)
    c = far[:, :1, :1]
    tiles = [(far - c) * LOG2E, (lookup(-LANE + j - i) - c) * LOG2E,
             jnp.where(vis0[None], (lookup(j - i) - c) * LOG2E, NEG), jnp.full_like(far, NEG)]
    zero = jnp.zeros((TQ, LANE), F32)
    masks = [zero, zero, jnp.where(vis0, 0.0, -jnp.inf).astype(F32), jnp.full((TQ, LANE), -jnp.inf, F32)]
    return jnp.stack(tiles, axis=1), jnp.stack(masks, axis=0)


def _tile_types(qb, nkt):
    return [jnp.clip(kt - qb, -2, 1) + 2 for kt in range(nkt)]


def _variants(nqb, qb0, seg_rows):
    if len(seg_rows) > 1 or nqb == 1:
        assert nqb == 1
        return [(0, 1, tuple(seg_rows), max(0, qb0 - 1))]
    nvar = min(4, nqb)
    assert nqb % nvar == 0
    grp = nqb // nvar
    nkt_total = seg_rows[0] // LANE
    return [(v * grp, (v + 1) * grp, (min(nkt_total, qb0 + (v + 1) * grp) * LANE,), max(0, qb0 + v * grp - 1))
            for v in range(nvar)]


def _run_variants(qi, variants, body):
    if len(variants) == 1:
        body(variants[0][2], variants[0][3])
    else:
        for lo_q, hi_q, rows, n_far in variants:
            pl.when((qi >= lo_q) & (qi < hi_q))(functools.partial(body, rows, n_far))


def _head_stack(qp, lo, kw):
    if kw == LANE:
        zero = jnp.zeros_like(qp)
        return jnp.concatenate([jnp.where(lo, qp, zero), jnp.where(lo, zero, qp)], axis=0)
    return jnp.concatenate([qp[:, :kw], qp[:, kw:]], axis=0)


def _stack2(a, b):
    return jnp.concatenate([a, b], axis=0)


def _softmax_pv(parts, v_blocks):
    s2 = parts[0] if len(parts) == 1 else jnp.concatenate(parts, axis=1)
    m = jnp.max(s2, axis=1, keepdims=True)
    p = jnp.exp2(s2 - m)
    l = jnp.sum(p, axis=1, keepdims=True)
    pb = p.astype(BF16)
    dv = min(v.shape[0] if t else v.shape[1] for v, t in v_blocks)
    o, c0 = None, 0
    for v, t in v_blocks:
        rows = v.shape[1] if t else v.shape[0]
        pc = pb[:, c0:c0 + rows]
        term = (_dot_nt(pc, v) if t else _dot(pc, v))[:, :dv]
        o = term if o is None else o + term
        c0 += rows
    return o * (1.0 / l)


def _attn_a_kernel(*refs, tq, qb0, variants, lam_init, seg_kt):
    aq_ref, sag_ref, slab_ref, alam_ref, subg_ref = refs[:5]
    nseg = len(seg_kt)
    segs = [refs[5 + 2 * i:7 + 2 * i] for i in range(nseg)]
    out_ref = refs[5 + 2 * nseg]
    qi = pl.program_id(1)
    qb = qi + qb0
    lp = alam_ref[...]
    lam = (jnp.exp(jnp.sum(lp[0:1] * lp[1:2], axis=1, keepdims=True))
           - jnp.exp(jnp.sum(lp[2:3] * lp[3:4], axis=1, keepdims=True)) + lam_init)
    lo = lax.broadcasted_iota(jnp.int32, (tq, LANE), 1) < DH_A

    def body(seg_rows, n_far):
        nkt = sum(r // LANE for r in seg_rows)
        ty = _tile_types(qb, nkt)
        for h in range(H_A):
            hs = slice(h * LANE, (h + 1) * LANE)
            q2 = _head_stack(aq_ref[0, :, hs], lo, LANE)
            parts, kt = [], 0
            for (k_ref, _), rows, kt_major in zip(segs, seg_rows, seg_kt):
                s = _dot(q2, k_ref[0, hs, :rows]) if kt_major else _dot_nt(q2, k_ref[0, :rows, hs])
                n_plain = min(max(n_far - kt, 0), rows // LANE)
                if n_plain:
                    parts.append(s[:, :n_plain * LANE])
                for j in range(n_plain, rows // LANE):
                    b = slab_ref[h, ty[kt + j], :tq]
                    parts.append(s[:, j * LANE:(j + 1) * LANE] + _stack2(b, b))
                kt += rows // LANE
            vs = [(v_ref[0, pl.ds(h, rows, stride=H_A), :].astype(BF16) if kt_major else v_ref[0, :rows, hs], False)
                  for (_, v_ref), rows, kt_major in zip(segs, seg_rows, seg_kt)]
            o2 = _softmax_pv(parts, vs)
            o = o2[:tq] - lam * o2[tq:]
            o = o * lax.rsqrt(jnp.mean(o * o, axis=1, keepdims=True) + EPS) * subg_ref[...] * (1.0 - lam_init)
            out_ref[0, :, hs] = (o * sag_ref[0, :, hs].astype(F32)).astype(BF16)

    _run_variants(qi, variants, body)


def _seg_spec(a, layer):
    if layer is None:
        return pl.BlockSpec((1,) + a.shape[1:], lambda bi, qi: (bi, 0, 0))
    return pl.BlockSpec((None, 1) + a.shape[2:], lambda bi, qi, layer=layer: (layer, bi, 0, 0))


def _attn_a(aq, sag, slab, a_lambda, subln_g, segs, *, tq, qb0, lam_init):
    b, t, _ = aq.shape
    nqb = t // tq
    seg_kt = tuple(kt for _, _, kt, _ in segs)
    seg_rows = [k.shape[-1] if kt else k.shape[-2] for k, _, kt, _ in segs]
    variants = _variants(nqb, qb0, seg_rows)
    kern = functools.partial(_attn_a_kernel, tq=tq, qb0=qb0, variants=variants, lam_init=lam_init, seg_kt=seg_kt)
    qspec = pl.BlockSpec((1, tq, 512), lambda bi, qi: (bi, qi, 0))
    seg_specs, seg_args = [], []
    for k, v, _, layer in segs:
        seg_specs += [_seg_spec(k, layer), _seg_spec(v, layer)]
        seg_args += [k, v]
    return pl.pallas_call(
        kern,
        grid=(b, nqb),
        in_specs=[qspec, qspec,
                  pl.BlockSpec(slab.shape, lambda bi, qi: (0, 0, 0, 0)),
                  pl.BlockSpec((4, DH_A), lambda bi, qi: (0, 0)),
                  pl.BlockSpec((1, DV_A), lambda bi, qi: (0, 0))] + seg_specs,
        out_specs=qspec,
        out_shape=jax.ShapeDtypeStruct((b, t, 512), BF16),
        compiler_params=pltpu.CompilerParams(dimension_semantics=("arbitrary", "arbitrary"),
                                             vmem_limit_bytes=VMEM_LIMIT),
        name="attn_a",
    )(aq, sag, slab, a_lambda, subln_g[None, :], *seg_args)


def _dsa_kernel(*refs, tq, qb0, variants, topk, seg_t):
    bq_ref, bqi_ref, wif_ref, sbg_ref, slab_ref, mslab_ref, tri_ref = refs[:7]
    nseg = len(seg_t)
    segs = [refs[7 + 3 * i:10 + 3 * i] for i in range(nseg)]
    out_ref, keys_ref = refs[7 + 3 * nseg:]
    qi = pl.program_id(1)
    qb = qi + qb0
    lo = lax.broadcasted_iota(jnp.int32, (tq, LANE), 1) < DH_B
    kf = float(topk)

    def body(seg_rows, n_far):
        nkt = sum(r // LANE for r in seg_rows)
        ty = _tile_types(qb, nkt)

        wi = wif_ref[0]
        kt0 = 0
        for (_, _, ki_ref), rows, tr in zip(segs, seg_rows, seg_t):
            ki = (ki_ref[0, :, :rows] if tr else ki_ref[0, :rows, :]).astype(BF16)
            kw = ki.shape[0] if tr else ki.shape[1]
            qis = jnp.concatenate([_head_stack(bqi_ref[0, :, hp * LANE:(hp + 1) * LANE], lo, kw)
                                   for hp in range(H_IDX // 2)], axis=0)
            r = jnp.maximum(_dot(qis, ki) if tr else _dot_nt(qis, ki), 0.0)
            score = wi[:, 0:1] * r[0:tq]
            for h in range(1, H_IDX):
                score = score + wi[:, h:h + 1] * r[h * tq:(h + 1) * tq]
            score = score + jnp.concatenate([mslab_ref[ty[kt0 + j], :tq] for j in range(rows // LANE)], axis=1)
            bits = pltpu.bitcast(score, jnp.int32)
            keys_ref[:, kt0 * LANE:kt0 * LANE + rows] = bits ^ ((bits >> 31) & jnp.int32(0x7FFFFFFF))
            kt0 += rows // LANE

        def count(pred):
            acc = None
            for kt in range(nkt):
                ind = jnp.where(pred(keys_ref[:, kt * LANE:(kt + 1) * LANE]), 1.0, 0.0)
                acc = ind if acc is None else acc + ind
            return jnp.sum(acc, axis=1, keepdims=True)

        def search(i, t):
            cand = t + lax.shift_left(jnp.int32(1), 31 - i)
            return jnp.where(count(lambda x: x >= cand) >= kf, cand, t)

        t = lax.fori_loop(0, 32, search, jnp.full((tq, 1), INT_MIN, jnp.int32), unroll=SEARCH_UNROLL)

        need = kf - count(lambda x: x > t)
        run = jnp.zeros((tq, 1), F32)
        negm = []
        for kt in range(nkt):
            kt_keys = keys_ref[:, kt * LANE:(kt + 1) * LANE]
            eq = jnp.where(kt_keys == t, 1.0, 0.0)
            rank = _dot(eq.astype(BF16), tri_ref[...]) + run
            tie = jnp.where(rank <= need, eq, 0.0)
            negm.append(jnp.where(kt_keys > t, 0.0, (1.0 - tie) * NEG))
            run = run + jnp.sum(eq, axis=1, keepdims=True)
        far2 = [_stack2(negm[kt], negm[kt]) for kt in range(n_far)]

        for hp in range(H_B // 2):
            hs = slice(hp * LANE, (hp + 1) * LANE)
            qp = bq_ref[0, :, hs]
            parts, vs, kt = [], [], 0
            for (k_ref, v_ref, _), rows, tr in zip(segs, seg_rows, seg_t):
                k = (k_ref[0, :, :rows] if tr else k_ref[0, :rows, :]).astype(BF16)
                q2 = _head_stack(qp, lo, k.shape[0] if tr else k.shape[1])
                s = _dot(q2, k) if tr else _dot_nt(q2, k)
                for j in range(rows // LANE):
                    if kt < n_far:
                        nb2 = far2[kt]
                    else:
                        nb2 = _stack2(negm[kt] + slab_ref[2 * hp, ty[kt], :tq],
                                      negm[kt] + slab_ref[2 * hp + 1, ty[kt], :tq])
                    parts.append(s[:, j * LANE:(j + 1) * LANE] + nb2)
                    kt += 1
                vs.append(((v_ref[0, :, :rows] if tr else v_ref[0, :rows, :]).astype(BF16), tr))
            o2 = _softmax_pv(parts, vs)
            if o2.shape[1] == LANE:
                o = jnp.where(lo, o2[:tq], o2[tq:])
            else:
                o = jnp.concatenate([o2[:tq], o2[tq:]], axis=1)
            out_ref[0, :, hs] = (o * sbg_ref[0, :, hs].astype(F32)).astype(BF16)

    _run_variants(qi, variants, body)


def _dsa(bq, bqi, wif, sbg, slab, mslab, tri, segs, *, tq, qb0, topk):
    b, t, _ = bq.shape
    nqb = t // tq
    seg_t = tuple(tr for _, _, _, tr, _ in segs)
    seg_rows = [s[0].shape[-1] if s[3] else s[0].shape[-2] for s in segs]
    variants = _variants(nqb, qb0, seg_rows)
    kern = functools.partial(_dsa_kernel, tq=tq, qb0=qb0, variants=variants, topk=topk, seg_t=seg_t)

    def qspec(w):
        return pl.BlockSpec((1, tq, w), lambda bi, qi: (bi, qi, 0))

    seg_specs, seg_args = [], []
    for k, v, ki, _, layer in segs:
        for a in (k, v, ki):
            seg_specs.append(_seg_spec(a, layer))
            seg_args.append(a)
    return pl.pallas_call(
        kern,
        grid=(b, nqb),
        in_specs=[qspec(512), qspec(256), qspec(LANE), qspec(512),
                  pl.BlockSpec(slab.shape, lambda bi, qi: (0, 0, 0, 0)),
                  pl.BlockSpec(mslab.shape, lambda bi, qi: (0, 0, 0)),
                  pl.BlockSpec((LANE, LANE), lambda bi, qi: (0, 0))] + seg_specs,
        out_specs=qspec(512),
        out_shape=jax.ShapeDtypeStruct((b, t, 512), BF16),
        scratch_shapes=[pltpu.VMEM((tq, sum(seg_rows)), jnp.int32)],
        compiler_params=pltpu.CompilerParams(dimension_semantics=("arbitrary", "arbitrary"),
                                             vmem_limit_bytes=VMEM_LIMIT),
        name="dsa",
    )(bq, bqi, wif, sbg, slab, mslab, tri, *seg_args)


def _ret_kernel(cq_ref, ck_ref, cv_ref, scg_ref, st_ref, dmat_ref, dq_ref, dk_ref, gc_ref, bd_ref, gn_ref,
                out_ref, sto_ref, *, c, nchunks):
    lane = lax.broadcasted_iota(jnp.int32, (c, LANE), 1)
    lo = lane < DK_C
    for p in range(H_C // 2):
        st = st_ref[0, p]
        for ci in range(nchunks):
            rows = slice(ci * c, (ci + 1) * c)
            q = cq_ref[0, rows, p * LANE:(p + 1) * LANE]
            k = ck_ref[0, rows, p * LANE:(p + 1) * LANE]
            v = cv_ref[0, rows, p * 2 * DV_C:(p + 1) * 2 * DV_C]
            qd = (q.astype(F32) * dq_ref[p]).astype(BF16)
            cross = _dot(qd, st.astype(BF16))
            zero = jnp.zeros_like(q)
            for j in range(2):
                h = 2 * p + j
                qm = jnp.where(lo, q, zero) if j == 0 else jnp.where(lo, zero, q)
                a = (_dot_nt(qm, k) * dmat_ref[h]).astype(BF16)
                o = cross[:, j * DV_C:(j + 1) * DV_C] + _dot(a, v[:, j * DV_C:(j + 1) * DV_C])
                o = o * lax.rsqrt(jnp.mean(o * o, axis=1, keepdims=True) + EPS) * gn_ref[...]
                gate = scg_ref[0, rows, h * DV_C:(h + 1) * DV_C].astype(F32)
                out_ref[0, rows, h * DV_C:(h + 1) * DV_C] = (o * gate).astype(BF16)
            kd = (k.astype(F32) * dk_ref[p]).astype(BF16)
            st = (gc_ref[p] * st + _dot_tn(kd, v)) * bd_ref[...]
        sto_ref[0, p] = st


def _retention(cq, ck, cv, scg, state_pairs, log_gamma, gn_g, *, c):
    b, t, _ = cq.shape
    nchunks = t // c
    n = jnp.arange(c, dtype=F32)
    diff = n[:, None] - n[None, :]
    dmat = jnp.where(diff >= 0, jnp.exp(log_gamma[:, None, None] * jnp.maximum(diff, 0.0)[None]), 0.0)
    decay_q = jnp.exp((n[:, None] + 1.0) * log_gamma[None, :])
    decay_k = jnp.exp((c - 1.0 - n)[:, None] * log_gamma[None, :])
    decay_c = jnp.exp(c * log_gamma)

    def lanes(tab):
        return jnp.transpose(jnp.repeat(tab, DK_C, axis=1).reshape(c, H_C // 2, LANE), (1, 0, 2))

    gc = jnp.broadcast_to(jnp.repeat(decay_c, DK_C).reshape(H_C // 2, LANE, 1), (H_C // 2, LANE, 2 * DV_C))
    bd = (jnp.arange(LANE)[:, None] // DK_C == jnp.arange(2 * DV_C)[None, :] // DV_C).astype(F32)

    def full(a):
        return pl.BlockSpec(a.shape, lambda bi: (0,) * a.ndim)

    def tok(w):
        return pl.BlockSpec((1, t, w), lambda bi: (bi, 0, 0))

    stspec = pl.BlockSpec((1, H_C // 2, LANE, 2 * DV_C), lambda bi: (bi, 0, 0, 0))
    dq, dk, gnv = lanes(decay_q), lanes(decay_k), gn_g[None, :]
    return pl.pallas_call(
        functools.partial(_ret_kernel, c=c, nchunks=nchunks),
        grid=(b,),
        in_specs=[tok(256), tok(256), tok(512), tok(512), stspec,
                  full(dmat), full(dq), full(dk), full(gc), full(bd), full(gnv)],
        out_specs=[tok(512), stspec],
        out_shape=[jax.ShapeDtypeStruct((b, t, 512), BF16),
                   jax.ShapeDtypeStruct((b, H_C // 2, LANE, 2 * DV_C), F32)],
        compiler_params=pltpu.CompilerParams(dimension_semantics=("arbitrary",), vmem_limit_bytes=VMEM_LIMIT),
        name="retention",
    )(cq, ck, cv, scg, state_pairs, dmat, dq, dk, gc, bd, gnv)


def _state_to_pairs(st):
    b = st.shape[0]
    s4 = st.reshape(b, H_C // 2, 2, DK_C, DV_C)
    z = jnp.zeros_like(s4[:, :, 0])
    top = jnp.concatenate([s4[:, :, 0], z], axis=-1)
    bot = jnp.concatenate([z, s4[:, :, 1]], axis=-1)
    return jnp.concatenate([top, bot], axis=-2)


def _pairs_to_state(sp):
    b = sp.shape[0]
    s0 = sp[:, :, :DK_C, :DV_C]
    s1 = sp[:, :, DK_C:, DV_C:]
    return jnp.stack([s0, s1], axis=2).reshape(b, H_C, DK_C, DV_C)


def _merge_kernel(x_ref, ba_ref, bb_ref, bc_ref, smg_ref, wb_ref, wo_ref, y_ref):
    m = None
    for n, br in enumerate((ba_ref, bb_ref, bc_ref)):
        proj = _dot(br[...], wb_ref[n])
        term = smg_ref[:, n * D_MODEL:(n + 1) * D_MODEL].astype(F32) * proj
        m = term if m is None else m + term
    y_ref[...] = x_ref[...] + _dot(m.astype(BF16), wo_ref[...])


def _merge(x2d, br_a, br_b, br_c, smg, wb, wo, tm):
    n = x2d.shape[0]

    def row(w):
        return pl.BlockSpec((tm, w), lambda i: (i, 0))

    return pl.pallas_call(
        _merge_kernel,
        grid=(n // tm,),
        in_specs=[row(D_MODEL), row(W_BR), row(W_BR), row(W_BR), row(N_BRANCH * D_MODEL),
                  pl.BlockSpec((N_BRANCH, W_BR, D_MODEL), lambda i: (0, 0, 0)),
                  pl.BlockSpec((D_MODEL, D_MODEL), lambda i: (0, 0))],
        out_specs=row(D_MODEL),
        out_shape=jax.ShapeDtypeStruct((n, D_MODEL), F32),
        compiler_params=pltpu.CompilerParams(dimension_semantics=("arbitrary",), vmem_limit_bytes=VMEM_LIMIT),
        name="merge",
    )(x2d, br_a, br_b, br_c, smg, wb, wo)


def _pad_rows(a, rows):
    return jnp.pad(a, ((0, 0), (0, rows - a.shape[1]), (0, 0)))


def kernel(x_prompt, x_sample, cache_a_k, cache_a_v, cache_b_k, cache_b_v, cache_b_kidx, state_c, rel_bias,
           norm_g, w_in, a_qk_g, a_lambda, a_subln_g, b_qk_g, c_gn_g, w_branch, w_out):
    bp, tp, _ = x_prompt.shape
    bs, ts, _ = x_sample.shape
    depth = w_in.shape[0]
    past = cache_a_k.shape[2]
    assert tp % TQ == 0 and ts == CHUNK and past % LANE == 0
    pos_p = jnp.arange(tp, dtype=jnp.int32)
    pos_s = past + jnp.arange(ts, dtype=jnp.int32)
    topk_p = min(TOPK_MAX, tp // 4)
    topk_s = min(TOPK_MAX, (past + ts) // 4)
    log_gamma = jnp.log(1.0 - 2.0 ** (-5.0 - jnp.arange(H_C, dtype=F32)))

    slab_a, mslab = _bias_slabs(rel_bias[:, :H_A])
    slab_b, _ = _bias_slabs(rel_bias[:, H_A:])
    lane_i = jnp.arange(LANE)
    g64 = (lane_i[:, None] // 64 == lane_i[None, :] // 64).astype(BF16)
    tri = (lane_i[:, None] <= lane_i[None, :]).astype(BF16)
    c_prompt = 256 if tp % 256 == 0 else CHUNK
    tm_p = 512
    tm_s = 256 if (bs * ts) % 256 == 0 else ts
    qb_s = past // LANE
    ka_t = jnp.transpose(cache_a_k, (0, 1, 3, 4, 5, 2)).reshape(depth, bs, 512, past).astype(BF16)
    va = cache_a_v.reshape(depth, bs, past * H_A, DV_A)
    kb_t, vb_t, kib_t = (jnp.transpose(c, (0, 1, 3, 2)) for c in (cache_b_k, cache_b_v, cache_b_kidx))

    yp = x_prompt.reshape(bp * tp, D_MODEL)
    ys = x_sample.reshape(bs * ts, D_MODEL)
    outs_p = {k: [] for k in ("ak", "av", "bk", "bv", "bki", "sc")}
    outs_s = {k: [] for k in ("ak", "av", "bk", "bv", "bki", "sc")}
    for l in range(depth):
        lam_init = 0.8 - 0.6 * math.exp(-0.3 * l)
        w_head = w_in[l, :, :HEAD_COLS].astype(BF16)
        w_tail = w_in[l, :, TAIL_START:TAIL_START + TAIL_COLS].astype(BF16)
        wb = w_branch[l].astype(BF16)
        wo = w_out[l].astype(BF16)

        P = _project(yp, pos_p, tm_p, norm_g[l], w_head, w_tail, a_qk_g[l], b_qk_g[l], g64, True)
        r3 = lambda a: a.reshape(bp, tp, a.shape[-1])
        br_a = _attn_a(r3(P["aq"]), r3(P["sag"]), slab_a, a_lambda[l], a_subln_g[l],
                       [(r3(P["akb"]), r3(P["avb"]), False, None)], tq=TQ, qb0=0, lam_init=lam_init)
        br_b = _dsa(r3(P["bq"]), r3(P["bqi"]), r3(P["wif"]), r3(P["sbg"]), slab_b, mslab, tri,
                    [(r3(P["kkb"]), r3(P["vvb"]), r3(P["kib"]), False, None)], tq=TQ, qb0=0, topk=topk_p)
        st0 = jnp.zeros((bp, H_C // 2, LANE, 2 * DV_C), F32)
        br_c, stp = _retention(r3(P["cq"]), r3(P["ck"]), r3(P["cv"]), r3(P["scg"]), st0, log_gamma, c_gn_g[l],
                               c=c_prompt)
        yp = _merge(yp, br_a.reshape(bp * tp, W_BR), br_b.reshape(bp * tp, W_BR), br_c.reshape(bp * tp, W_BR),
                    P["smg"], wb, wo, tm_p)
        outs_p["ak"].append(jnp.transpose(P["ak"].reshape(bp, H_A, 2, DH_A, tp), (0, 4, 1, 2, 3)))
        outs_p["av"].append(P["av"].reshape(bp, tp, H_A, DV_A))
        outs_p["bk"].append(jnp.transpose(P["bk"], (0, 2, 1)))
        outs_p["bv"].append(jnp.transpose(P["bv"], (0, 2, 1)))
        outs_p["bki"].append(jnp.transpose(P["bki"], (0, 2, 1)))
        outs_p["sc"].append(_pairs_to_state(stp))

        S = _project(ys, pos_s, tm_s, norm_g[l], w_head, w_tail, a_qk_g[l], b_qk_g[l], g64, False)
        r3s = lambda a: a.reshape(bs, ts, a.shape[-1])
        rk = lambda a: _pad_rows(r3s(a), LANE)
        br_a = _attn_a(r3s(S["aq"]), r3s(S["sag"]), slab_a, a_lambda[l], a_subln_g[l],
                       [(ka_t, va, True, l), (rk(S["akb"]), rk(S["avb"]), False, None)],
                       tq=ts, qb0=qb_s, lam_init=lam_init)
        br_b = _dsa(r3s(S["bq"]), r3s(S["bqi"]), r3s(S["wif"]), r3s(S["sbg"]), slab_b, mslab, tri,
                    [(kb_t, vb_t, kib_t, True, l), (rk(S["kkb"]), rk(S["vvb"]), rk(S["kib"]), False, None)],
                    tq=ts, qb0=qb_s, topk=topk_s)
        br_c, sts = _retention(r3s(S["cq"]), r3s(S["ck"]), r3s(S["cv"]), r3s(S["scg"]),
                               _state_to_pairs(state_c[l].astype(F32)), log_gamma, c_gn_g[l], c=ts)
        ys = _merge(ys, br_a.reshape(bs * ts, W_BR), br_b.reshape(bs * ts, W_BR), br_c.reshape(bs * ts, W_BR),
                    S["smg"], wb, wo, tm_s)
        outs_s["ak"].append(S["ak"].reshape(bs, ts, H_A, 2, DH_A))
        outs_s["av"].append(S["av"].reshape(bs, ts, H_A, DV_A))
        outs_s["bk"].append(S["bk"].reshape(bs, ts, DH_B))
        outs_s["bv"].append(S["bv"].reshape(bs, ts, DH_B))
        outs_s["bki"].append(S["bki"].reshape(bs, ts, D_IDX))
        outs_s["sc"].append(_pairs_to_state(sts).astype(state_c.dtype))

    order = ("ak", "av", "bk", "bv", "bki", "sc")
    return ((yp.reshape(bp, tp, D_MODEL), ys.reshape(bs, ts, D_MODEL))
            + tuple(jnp.stack(outs_p[k]) for k in order)
            + tuple(jnp.stack(outs_s[k]) for k in order))
```

```python
import functools
import math

import jax
import jax.numpy as jnp
from jax import lax
from jax.experimental import pallas as pl
from jax.experimental.pallas import tpu as pltpu

D_MODEL = 1024
CHUNK = 64
EPS = 1e-6
H_A = 4
DH_A = 64
DV_A = 128
H_B = 8
DH_B = 64
H_IDX = 4
D_IDX = 64
TOPK_MAX = 256
H_C = 4
DK_C = 64
DV_C = 128
W_BR = 512
N_BRANCH = 3
NUM_BUCKETS = 32
MAX_DISTANCE = 128
ROPE_BASE = 10000.0

LANE = 128
TQ = 128
NEG = -1e30
LOG2E = math.log2(math.e)
SEARCH_UNROLL = 8
RADIX4_MAX_TILES = 4
INT_MIN = -(2 ** 31)
F32 = jnp.float32
BF16 = jnp.bfloat16
VMEM_LIMIT = 56 * 1024 * 1024

_SEC = {"aq": (0, 0, 512), "ak": (0, 512, 512), "av": (0, 1024, 512), "ag": (0, 1536, 512), "bq": (0, 2048, 512),
        "kv": (0, 2560, 128), "bqi": (0, 2688, 256), "kiw": (0, 2944, 128),
        "bg": (1, 0, 512), "cq": (1, 512, 256), "ck": (1, 768, 256), "cv": (1, 1024, 512), "cg": (1, 1536, 512),
        "mg": (1, 2048, 3072)}
HEAD_COLS = 3072
TAIL_START = 3012
TAIL_COLS = 5120


def _dot(a, b):
    return jnp.dot(a, b, preferred_element_type=F32)


def _dot_nt(a, b):
    return lax.dot_general(a, b, (((1,), (1,)), ((), ())), preferred_element_type=F32)


def _dot_tn(a, b):
    return lax.dot_general(a, b, (((0,), (0,)), ((), ())), preferred_element_type=F32)


def _proj_kernel(x_ref, g_ref, wh_ref, wt_ref, cq_cos_ref, cq_sin_ref, ck_cos_ref, ck_sin_ref,
                 gaq_ref, gak_ref, gbq_ref, gbk_ref, g64_ref,
                 aq_ref, ak_ref, akb_ref, av_ref, avb_ref, sag_ref, bq_ref, bqi_ref,
                 bk_ref, kkb_ref, bv_ref, vvb_ref, bki_ref, kib_ref, wif_ref, sbg_ref,
                 cq_ref, ck_ref, cv_ref, scg_ref, smg_ref, *, seq_minor):
    x = x_ref[...]
    ms = jnp.mean(x * x, axis=-1, keepdims=True)
    xn = ((x * lax.rsqrt(ms + EPS)) * g_ref[...]).astype(BF16)
    w_refs = (wh_ref, wt_ref)

    def mm(name, sub=None):
        op, c0, n = _SEC[name]
        if sub is not None:
            c0, n = c0 + sub[0], sub[1]
        return _dot(xn, w_refs[op][:, c0:c0 + n])

    def group_norm64(z, gain):
        zz = (z * z).astype(BF16)
        parts = [_dot(zz[:, j * LANE:(j + 1) * LANE], g64_ref[...]) for j in range(z.shape[1] // LANE)]
        ss = parts[0] if len(parts) == 1 else jnp.concatenate(parts, axis=1)
        return (z * lax.rsqrt(ss * (1.0 / 64.0) + EPS)) * gain

    lane = lax.broadcasted_iota(jnp.int32, (x.shape[0], LANE), 1)
    lo = lane < 64

    def rotate_half(z):
        first = (lane & 63) < DK_C // 2
        cols = []
        for j in range(z.shape[1] // LANE):
            zj = z[:, j * LANE:(j + 1) * LANE]
            cols.append(jnp.where(first, -pltpu.roll(zj, LANE - DK_C // 2, 1), pltpu.roll(zj, DK_C // 2, 1)))
        return jnp.concatenate(cols, axis=1)

    aq_ref[...] = group_norm64(mm("aq"), gaq_ref[...]).astype(BF16)
    ak = group_norm64(mm("ak"), gak_ref[...])
    akb_ref[...] = ak.astype(BF16)
    av = mm("av")
    avb_ref[...] = av.astype(BF16)
    if seq_minor:
        ak_ref[0] = jnp.transpose(ak)
        for h in range(H_A):
            av_ref[pl.ds(h, av.shape[0], stride=H_A), :] = av[:, h * DV_A:(h + 1) * DV_A]
    else:
        ak_ref[...] = ak
        av_ref[...] = av
    z = mm("ag")
    sag_ref[...] = (z * jax.nn.sigmoid(z)).astype(BF16)
    bq_ref[...] = group_norm64(mm("bq"), gbq_ref[...]).astype(BF16)
    bqi_ref[...] = mm("bqi").astype(BF16)
    z = mm("kv")
    kv = jnp.where(lo, group_norm64(z, gbk_ref[...]), z)
    vk = pltpu.roll(kv, 64, 1)
    if seq_minor:
        kv_t = jnp.transpose(kv)
        bk_ref[0] = kv_t[:DH_B]
        bv_ref[0] = kv_t[DH_B:]
    else:
        bk_ref[...] = kv[:, :DH_B]
        bv_ref[...] = vk[:, :DH_B]
    kkb_ref[...] = jnp.where(lo, kv, vk).astype(BF16)
    vvb_ref[...] = jnp.where(lo, vk, kv).astype(BF16)
    z = mm("kiw")
    zr = pltpu.roll(z, 64, 1)
    if seq_minor:
        bki_ref[0] = jnp.transpose(z)[:D_IDX]
    else:
        bki_ref[...] = z[:, :D_IDX]
    kib_ref[...] = jnp.where(lo, z, zr).astype(BF16)
    wif_ref[...] = zr
    z = mm("bg")
    sbg_ref[...] = (z * jax.nn.sigmoid(z)).astype(BF16)
    z = mm("cq")
    cq_ref[...] = (z * cq_cos_ref[...] + rotate_half(z) * cq_sin_ref[...]).astype(BF16)
    z = mm("ck")
    ck_ref[...] = (z * ck_cos_ref[...] + rotate_half(z) * ck_sin_ref[...]).astype(BF16)
    cv_ref[...] = mm("cv").astype(BF16)
    z = mm("cg")
    scg_ref[...] = (z * jax.nn.sigmoid(z)).astype(BF16)
    for j in range(N_BRANCH * 2):
        z = mm("mg", (j * 512, 512))
        smg_ref[:, j * 512:(j + 1) * 512] = jax.nn.sigmoid(z).astype(BF16)


def _rope_tables(pos, rows):
    half = DK_C // 2
    inv = ROPE_BASE ** (-jnp.arange(half, dtype=F32) / half)
    ang = pos.astype(F32)[:, None] * inv[None, :]
    cos = jnp.tile(jnp.cos(ang), (1, 2 * H_C))
    sin = jnp.tile(jnp.sin(ang), (1, 2 * H_C))
    reps = max(1, rows // cos.shape[0])
    cos = jnp.tile(cos, (reps, 1))
    sin = jnp.tile(sin, (reps, 1))
    kscale = DK_C ** -0.5
    return cos, sin, cos * kscale, sin * kscale


def _project(x2d, pos, tm, norm_g, w_head, w_tail, a_qk_g, b_qk_g, g64, seq_minor):
    n = x2d.shape[0]
    seq = pos.shape[0]
    assert not seq_minor or seq % tm == 0
    nt = max(1, seq // tm)
    cq_cos, cq_sin, ck_cos, ck_sin = _rope_tables(pos, tm)
    ntab = cq_cos.shape[0] // tm
    gaq = jnp.tile(a_qk_g[0], 8)[None, :] * (LOG2E * DH_A ** -0.5)
    gak = jnp.tile(a_qk_g[1], 8)[None, :]
    gbq = jnp.tile(b_qk_g[0], 8)[None, :] * (LOG2E * DH_B ** -0.5)
    gbk = jnp.tile(b_qk_g[1], 2)[None, :]

    def row(width):
        return pl.BlockSpec((tm, width), lambda i: (i, 0))

    def const(shape, single=False):
        if single:
            return pl.BlockSpec(shape, lambda i: (0, 0), pipeline_mode=pl.Buffered(1))
        return pl.BlockSpec(shape, lambda i: (0, 0))

    tab = pl.BlockSpec((tm, 256), lambda i: (i % ntab, 0))
    outs = [("aq", 512, BF16), ("ak", 512, F32), ("akb", 512, BF16), ("av", 512, F32), ("avb", 512, BF16),
            ("sag", 512, BF16), ("bq", 512, BF16), ("bqi", 256, BF16), ("bk", 64, F32), ("kkb", 128, BF16),
            ("bv", 64, F32), ("vvb", 128, BF16), ("bki", 64, F32), ("kib", 128, BF16), ("wif", 128, F32),
            ("sbg", 512, BF16), ("cq", 256, BF16), ("ck", 256, BF16), ("cv", 512, BF16), ("scg", 512, BF16),
            ("smg", 3072, BF16)]
    def out_spec(name, w):
        if seq_minor and name in ("ak", "bk", "bv", "bki"):
            return pl.BlockSpec((1, w, tm), lambda i: (i // nt, 0, i % nt))
        if seq_minor and name == "av":
            return pl.BlockSpec((tm * H_A, DV_A), lambda i: (i, 0))
        return row(w)

    def out_shape(name, w, dt):
        if seq_minor and name in ("ak", "bk", "bv", "bki"):
            return jax.ShapeDtypeStruct((n // seq, w, seq), dt)
        if seq_minor and name == "av":
            return jax.ShapeDtypeStruct((n * H_A, DV_A), dt)
        return jax.ShapeDtypeStruct((n, w), dt)

    res = pl.pallas_call(
        functools.partial(_proj_kernel, seq_minor=seq_minor),
        grid=(n // tm,),
        in_specs=[row(D_MODEL), const((1, D_MODEL)), const((D_MODEL, HEAD_COLS), single=True),
                  const((D_MODEL, TAIL_COLS), single=True),
                  tab, tab, tab, tab,
                  const((1, 512)), const((1, 512)), const((1, 512)), const((1, 128)), const((LANE, LANE))],
        out_specs=[out_spec(name, w) for name, w, _ in outs],
        out_shape=[out_shape(name, w, dt) for name, w, dt in outs],
        compiler_params=pltpu.CompilerParams(dimension_semantics=("arbitrary",), vmem_limit_bytes=VMEM_LIMIT),
        name="proj",
    )(x2d, norm_g[None, :], w_head, w_tail, cq_cos, cq_sin, ck_cos, ck_sin, gaq, gak, gbq, gbk, g64)
    return {name: r for (name, _, _), r in zip(outs, res)}


def _t5_bucket(rel):
    nb = NUM_BUCKETS // 2
    max_exact = nb // 2
    ret = jnp.where(rel > 0, nb, 0)
    n = jnp.abs(rel)
    large = max_exact + (jnp.log(jnp.maximum(n, 1).astype(F32) / max_exact)
                         / math.log(MAX_DISTANCE / max_exact) * (nb - max_exact)).astype(jnp.int32)
    large = jnp.minimum(large, nb - 1)
    return ret + jnp.where(n < max_exact, n, large)


def _bias_slabs(bias_cols):
    i = jnp.arange(TQ, dtype=jnp.int32)[:, None]
    j = jnp.arange(LANE, dtype=jnp.int32)[None, :]
    vis0 = j < (i // CHUNK + 1) * CHUNK

    def lookup(rel):
        bucket = _t5_bucket(rel)
        out = jnp.zeros((bias_cols.shape[1],) + rel.shape, F32)
        for b in range(NUM_BUCKETS):
            out = jnp.where(bucket[None] == b, bias_cols[b].astype(F32)[:, None, None], out)
        return out

    far = lookup(-2 * LANE + j - i)
    c = far[:, :1, :1]
    tiles = [(far - c) * LOG2E, (lookup(-LANE + j - i) - c) * LOG2E,
             jnp.where(vis0[None], (lookup(j - i) - c) * LOG2E, NEG), jnp.full_like(far, NEG)]
    zero = jnp.zeros((TQ, LANE), F32)
    masks = [zero, zero, jnp.where(vis0, 0.0, -jnp.inf).astype(F32), jnp.full((TQ, LANE), -jnp.inf, F32)]
    return jnp.stack(tiles, axis=1), jnp.stack(masks, axis=0)


def _tile_types(qb, nkt):
    return [jnp.clip(kt - qb, -2, 1) + 2 for kt in range(nkt)]


def _variants(nqb, qb0, seg_rows):
    if len(seg_rows) > 1 or nqb == 1:
        assert nqb == 1
        return [(0, 1, tuple(seg_rows), max(0, qb0 - 1))]
    nvar = min(4, nqb)
    assert nqb % nvar == 0
    grp = nqb // nvar
    nkt_total = seg_rows[0] // LANE
    return [(v * grp, (v + 1) * grp, (min(nkt_total, qb0 + (v + 1) * grp) * LANE,), max(0, qb0 + v * grp - 1))
            for v in range(nvar)]


def _run_variants(qi, variants, body):
    if len(variants) == 1:
        body(variants[0][2], variants[0][3])
    else:
        for lo_q, hi_q, rows, n_far in variants:
            pl.when((qi >= lo_q) & (qi < hi_q))(functools.partial(body, rows, n_far))


def _head_stack(qp, lo, kw):
    if kw == LANE:
        zero = jnp.zeros_like(qp)
        return jnp.concatenate([jnp.where(lo, qp, zero), jnp.where(lo, zero, qp)], axis=0)
    return jnp.concatenate([qp[:, :kw], qp[:, kw:]], axis=0)


def _stack2(a, b):
    return jnp.concatenate([a, b], axis=0)


def _softmax_pv(parts, v_blocks):
    s2 = parts[0] if len(parts) == 1 else jnp.concatenate(parts, axis=1)
    m = jnp.max(s2, axis=1, keepdims=True)
    p = jnp.exp2(s2 - m)
    l = jnp.sum(p, axis=1, keepdims=True)
    pb = p.astype(BF16)
    dv = min(v.shape[0] if t else v.shape[1] for v, t in v_blocks)
    o, c0 = None, 0
    for v, t in v_blocks:
        rows = v.shape[1] if t else v.shape[0]
        pc = pb[:, c0:c0 + rows]
        term = (_dot_nt(pc, v) if t else _dot(pc, v))[:, :dv]
        o = term if o is None else o + term
        c0 += rows
    return o * (1.0 / l)


def _attn_a_kernel(*refs, tq, qb0, variants, lam_init, seg_kt):
    aq_ref, sag_ref, slab_ref, alam_ref, subg_ref = refs[:5]
    nseg = len(seg_kt)
    segs = [refs[5 + 2 * i:7 + 2 * i] for i in range(nseg)]
    out_ref = refs[5 + 2 * nseg]
    qi = pl.program_id(1)
    qb = qi + qb0
    lp = alam_ref[...]
    lam = (jnp.exp(jnp.sum(lp[0:1] * lp[1:2], axis=1, keepdims=True))
           - jnp.exp(jnp.sum(lp[2:3] * lp[3:4], axis=1, keepdims=True)) + lam_init)
    lo = lax.broadcasted_iota(jnp.int32, (tq, LANE), 1) < DH_A

    def body(seg_rows, n_far):
        nkt = sum(r // LANE for r in seg_rows)
        ty = _tile_types(qb, nkt)
        for h in range(H_A):
            hs = slice(h * LANE, (h + 1) * LANE)
            q2 = _head_stack(aq_ref[0, :, hs], lo, LANE)
            parts, kt = [], 0
            for (k_ref, _), rows, kt_major in zip(segs, seg_rows, seg_kt):
                s = (_dot(q2, k_ref[0, hs, :rows].astype(BF16)) if kt_major
                     else _dot_nt(q2, k_ref[0, :rows, hs]))
                n_plain = min(max(n_far - kt, 0), rows // LANE)
                if n_plain:
                    parts.append(s[:, :n_plain * LANE])
                for j in range(n_plain, rows // LANE):
                    b = slab_ref[h, ty[kt + j], :tq]
                    parts.append(s[:, j * LANE:(j + 1) * LANE] + _stack2(b, b))
                kt += rows // LANE
            vs = [(v_ref[0, pl.ds(h, rows, stride=H_A), :].astype(BF16) if kt_major else v_ref[0, :rows, hs], False)
                  for (_, v_ref), rows, kt_major in zip(segs, seg_rows, seg_kt)]
            o2 = _softmax_pv(parts, vs)
            o = o2[:tq] - lam * o2[tq:]
            o = o * lax.rsqrt(jnp.mean(o * o, axis=1, keepdims=True) + EPS) * subg_ref[...] * (1.0 - lam_init)
            out_ref[0, :, hs] = (o * sag_ref[0, :, hs].astype(F32)).astype(BF16)

    _run_variants(qi, variants, body)


def _seg_spec(a, layer):
    if layer is None:
        return pl.BlockSpec((1,) + a.shape[1:], lambda bi, qi: (bi, 0, 0))
    return pl.BlockSpec((None, 1) + a.shape[2:], lambda bi, qi, layer=layer: (layer, bi, 0, 0))


def _attn_a(aq, sag, slab, a_lambda, subln_g, segs, *, tq, qb0, lam_init):
    b, t, _ = aq.shape
    nqb = t // tq
    seg_kt = tuple(kt for _, _, kt, _ in segs)
    seg_rows = [k.shape[-1] if kt else k.shape[-2] for k, _, kt, _ in segs]
    variants = _variants(nqb, qb0, seg_rows)
    kern = functools.partial(_attn_a_kernel, tq=tq, qb0=qb0, variants=variants, lam_init=lam_init, seg_kt=seg_kt)
    qspec = pl.BlockSpec((1, tq, 512), lambda bi, qi: (bi, qi, 0))
    seg_specs, seg_args = [], []
    for k, v, _, layer in segs:
        seg_specs += [_seg_spec(k, layer), _seg_spec(v, layer)]
        seg_args += [k, v]
    return pl.pallas_call(
        kern,
        grid=(b, nqb),
        in_specs=[qspec, qspec,
                  pl.BlockSpec(slab.shape, lambda bi, qi: (0, 0, 0, 0)),
                  pl.BlockSpec((4, DH_A), lambda bi, qi: (0, 0)),
                  pl.BlockSpec((1, DV_A), lambda bi, qi: (0, 0))] + seg_specs,
        out_specs=qspec,
        out_shape=jax.ShapeDtypeStruct((b, t, 512), BF16),
        compiler_params=pltpu.CompilerParams(dimension_semantics=("arbitrary", "arbitrary"),
                                             vmem_limit_bytes=VMEM_LIMIT),
        name="attn_a",
    )(aq, sag, slab, a_lambda, subln_g[None, :], *seg_args)


def _dsa_kernel(*refs, tq, qb0, variants, topk, seg_t):
    bq_ref, bqi_ref, wif_ref, sbg_ref, slab_ref, mslab_ref, tri_ref = refs[:7]
    nseg = len(seg_t)
    segs = [refs[7 + 3 * i:10 + 3 * i] for i in range(nseg)]
    out_ref, keys_ref = refs[7 + 3 * nseg:]
    qi = pl.program_id(1)
    qb = qi + qb0
    lo = lax.broadcasted_iota(jnp.int32, (tq, LANE), 1) < DH_B
    kf = float(topk)

    def body(seg_rows, n_far):
        nkt = sum(r // LANE for r in seg_rows)
        ty = _tile_types(qb, nkt)

        wi = wif_ref[0]
        kt0 = 0
        for (_, _, ki_ref), rows, tr in zip(segs, seg_rows, seg_t):
            ki = (ki_ref[0, :, :rows] if tr else ki_ref[0, :rows, :]).astype(BF16)
            kw = ki.shape[0] if tr else ki.shape[1]
            qis = jnp.concatenate([_head_stack(bqi_ref[0, :, hp * LANE:(hp + 1) * LANE], lo, kw)
                                   for hp in range(H_IDX // 2)], axis=0)
            r = jnp.maximum(_dot(qis, ki) if tr else _dot_nt(qis, ki), 0.0)
            score = wi[:, 0:1] * r[0:tq]
            for h in range(1, H_IDX):
                score = score + wi[:, h:h + 1] * r[h * tq:(h + 1) * tq]
            score = score + jnp.concatenate([mslab_ref[ty[kt0 + j], :tq] for j in range(rows // LANE)], axis=1)
            bits = pltpu.bitcast(score, jnp.int32)
            keys_ref[:, kt0 * LANE:kt0 * LANE + rows] = bits ^ ((bits >> 31) & jnp.int32(0x7FFFFFFF))
            kt0 += rows // LANE

        def count(pred):
            acc = None
            for kt in range(nkt):
                ind = jnp.where(pred(keys_ref[:, kt * LANE:(kt + 1) * LANE]), 1.0, 0.0)
                acc = ind if acc is None else acc + ind
            return jnp.sum(acc, axis=1, keepdims=True)

        def search(i, t):
            cand = t + lax.shift_left(jnp.int32(1), 31 - i)
            return jnp.where(count(lambda x: x >= cand) >= kf, cand, t)

        def search4(i, t):
            d = lax.shift_left(jnp.int32(1), 30 - 2 * i)
            c1 = t + d
            c2 = t + lax.shift_left(d, 1)
            c3 = c2 + d
            accs = [None, None, None]
            for kt in range(nkt):
                tile = keys_ref[:, kt * LANE:(kt + 1) * LANE]
                for j, c in enumerate((c1, c2, c3)):
                    ind = jnp.where(tile >= c, 1.0, 0.0)
                    accs[j] = ind if accs[j] is None else accs[j] + ind
            n1, n2, n3 = (jnp.sum(a, axis=1, keepdims=True) for a in accs)
            return jnp.where(n3 >= kf, c3, jnp.where(n2 >= kf, c2, jnp.where(n1 >= kf, c1, t)))

        t0 = jnp.full((tq, 1), INT_MIN, jnp.int32)
        if nkt <= RADIX4_MAX_TILES:
            t = lax.fori_loop(0, 16, search4, t0, unroll=4)
        else:
            t = lax.fori_loop(0, 32, search, t0, unroll=SEARCH_UNROLL)

        need = kf - count(lambda x: x > t)
        run = jnp.zeros((tq, 1), F32)
        negm = []
        for kt in range(nkt):
            kt_keys = keys_ref[:, kt * LANE:(kt + 1) * LANE]
            eq = jnp.where(kt_keys == t, 1.0, 0.0)
            rank = _dot(eq.astype(BF16), tri_ref[...]) + run
            tie = jnp.where(rank <= need, eq, 0.0)
            negm.append(jnp.where(kt_keys > t, 0.0, (1.0 - tie) * NEG))
            run = run + jnp.sum(eq, axis=1, keepdims=True)
        far2 = [_stack2(negm[kt], negm[kt]) for kt in range(n_far)]

        for hp in range(H_B // 2):
            hs = slice(hp * LANE, (hp + 1) * LANE)
            qp = bq_ref[0, :, hs]
            parts, vs, kt = [], [], 0
            for (k_ref, v_ref, _), rows, tr in zip(segs, seg_rows, seg_t):
                k = (k_ref[0, :, :rows] if tr else k_ref[0, :rows, :]).astype(BF16)
                q2 = _head_stack(qp, lo, k.shape[0] if tr else k.shape[1])
                s = _dot(q2, k) if tr else _dot_nt(q2, k)
                for j in range(rows // LANE):
                    if kt < n_far:
                        nb2 = far2[kt]
                    else:
                        nb2 = _stack2(negm[kt] + slab_ref[2 * hp, ty[kt], :tq],
                                      negm[kt] + slab_ref[2 * hp + 1, ty[kt], :tq])
                    parts.append(s[:, j * LANE:(j + 1) * LANE] + nb2)
                    kt += 1
                vs.append(((v_ref[0, :, :rows] if tr else v_ref[0, :rows, :]).astype(BF16), tr))
            o2 = _softmax_pv(parts, vs)
            if o2.shape[1] == LANE:
                o = jnp.where(lo, o2[:tq], o2[tq:])
            else:
                o = jnp.concatenate([o2[:tq], o2[tq:]], axis=1)
            out_ref[0, :, hs] = (o * sbg_ref[0, :, hs].astype(F32)).astype(BF16)

    _run_variants(qi, variants, body)


def _dsa(bq, bqi, wif, sbg, slab, mslab, tri, segs, *, tq, qb0, topk):
    b, t, _ = bq.shape
    nqb = t // tq
    seg_t = tuple(tr for _, _, _, tr, _ in segs)
    seg_rows = [s[0].shape[-1] if s[3] else s[0].shape[-2] for s in segs]
    variants = _variants(nqb, qb0, seg_rows)
    kern = functools.partial(_dsa_kernel, tq=tq, qb0=qb0, variants=variants, topk=topk, seg_t=seg_t)

    def qspec(w):
        return pl.BlockSpec((1, tq, w), lambda bi, qi: (bi, qi, 0))

    seg_specs, seg_args = [], []
    for k, v, ki, _, layer in segs:
        for a in (k, v, ki):
            seg_specs.append(_seg_spec(a, layer))
            seg_args.append(a)
    return pl.pallas_call(
        kern,
        grid=(b, nqb),
        in_specs=[qspec(512), qspec(256), qspec(LANE), qspec(512),
                  pl.BlockSpec(slab.shape, lambda bi, qi: (0, 0, 0, 0)),
                  pl.BlockSpec(mslab.shape, lambda bi, qi: (0, 0, 0)),
                  pl.BlockSpec((LANE, LANE), lambda bi, qi: (0, 0))] + seg_specs,
        out_specs=qspec(512),
        out_shape=jax.ShapeDtypeStruct((b, t, 512), BF16),
        scratch_shapes=[pltpu.VMEM((tq, sum(seg_rows)), jnp.int32)],
        compiler_params=pltpu.CompilerParams(dimension_semantics=("arbitrary", "arbitrary"),
                                             vmem_limit_bytes=VMEM_LIMIT),
        name="dsa",
    )(bq, bqi, wif, sbg, slab, mslab, tri, *seg_args)


def _ret_kernel(cq_ref, ck_ref, cv_ref, scg_ref, st_ref, dmat_ref, dq_ref, dk_ref, gc_ref, bd_ref, gn_ref,
                out_ref, sto_ref, *, c, nchunks):
    lane = lax.broadcasted_iota(jnp.int32, (c, LANE), 1)
    lo = lane < DK_C
    for p in range(H_C // 2):
        st = st_ref[0, p]
        for ci in range(nchunks):
            rows = slice(ci * c, (ci + 1) * c)
            q = cq_ref[0, rows, p * LANE:(p + 1) * LANE]
            k = ck_ref[0, rows, p * LANE:(p + 1) * LANE]
            v = cv_ref[0, rows, p * 2 * DV_C:(p + 1) * 2 * DV_C]
            qd = (q.astype(F32) * dq_ref[p]).astype(BF16)
            cross = _dot(qd, st.astype(BF16))
            zero = jnp.zeros_like(q)
            for j in range(2):
                h = 2 * p + j
                qm = jnp.where(lo, q, zero) if j == 0 else jnp.where(lo, zero, q)
                a = (_dot_nt(qm, k) * dmat_ref[h]).astype(BF16)
                o = cross[:, j * DV_C:(j + 1) * DV_C] + _dot(a, v[:, j * DV_C:(j + 1) * DV_C])
                o = o * lax.rsqrt(jnp.mean(o * o, axis=1, keepdims=True) + EPS) * gn_ref[...]
                gate = scg_ref[0, rows, h * DV_C:(h + 1) * DV_C].astype(F32)
                out_ref[0, rows, h * DV_C:(h + 1) * DV_C] = (o * gate).astype(BF16)
            kd = (k.astype(F32) * dk_ref[p]).astype(BF16)
            st = (gc_ref[p] * st + _dot_tn(kd, v)) * bd_ref[...]
        sto_ref[0, p] = st


def _retention(cq, ck, cv, scg, state_pairs, log_gamma, gn_g, *, c):
    b, t, _ = cq.shape
    nchunks = t // c
    n = jnp.arange(c, dtype=F32)
    diff = n[:, None] - n[None, :]
    dmat = jnp.where(diff >= 0, jnp.exp(log_gamma[:, None, None] * jnp.maximum(diff, 0.0)[None]), 0.0)
    decay_q = jnp.exp((n[:, None] + 1.0) * log_gamma[None, :])
    decay_k = jnp.exp((c - 1.0 - n)[:, None] * log_gamma[None, :])
    decay_c = jnp.exp(c * log_gamma)

    def lanes(tab):
        return jnp.transpose(jnp.repeat(tab, DK_C, axis=1).reshape(c, H_C // 2, LANE), (1, 0, 2))

    gc = jnp.broadcast_to(jnp.repeat(decay_c, DK_C).reshape(H_C // 2, LANE, 1), (H_C // 2, LANE, 2 * DV_C))
    bd = (jnp.arange(LANE)[:, None] // DK_C == jnp.arange(2 * DV_C)[None, :] // DV_C).astype(F32)

    def full(a):
        return pl.BlockSpec(a.shape, lambda bi: (0,) * a.ndim)

    def tok(w):
        return pl.BlockSpec((1, t, w), lambda bi: (bi, 0, 0))

    stspec = pl.BlockSpec((1, H_C // 2, LANE, 2 * DV_C), lambda bi: (bi, 0, 0, 0))
    dq, dk, gnv = lanes(decay_q), lanes(decay_k), gn_g[None, :]
    return pl.pallas_call(
        functools.partial(_ret_kernel, c=c, nchunks=nchunks),
        grid=(b,),
        in_specs=[tok(256), tok(256), tok(512), tok(512), stspec,
                  full(dmat), full(dq), full(dk), full(gc), full(bd), full(gnv)],
        out_specs=[tok(512), stspec],
        out_shape=[jax.ShapeDtypeStruct((b, t, 512), BF16),
                   jax.ShapeDtypeStruct((b, H_C // 2, LANE, 2 * DV_C), F32)],
        compiler_params=pltpu.CompilerParams(dimension_semantics=("arbitrary",), vmem_limit_bytes=VMEM_LIMIT),
        name="retention",
    )(cq, ck, cv, scg, state_pairs, dmat, dq, dk, gc, bd, gnv)


def _state_to_pairs(st):
    b = st.shape[0]
    s4 = st.reshape(b, H_C // 2, 2, DK_C, DV_C)
    z = jnp.zeros_like(s4[:, :, 0])
    top = jnp.concatenate([s4[:, :, 0], z], axis=-1)
    bot = jnp.concatenate([z, s4[:, :, 1]], axis=-1)
    return jnp.concatenate([top, bot], axis=-2)


def _pairs_to_state(sp):
    b = sp.shape[0]
    s0 = sp[:, :, :DK_C, :DV_C]
    s1 = sp[:, :, DK_C:, DV_C:]
    return jnp.stack([s0, s1], axis=2).reshape(b, H_C, DK_C, DV_C)


def _merge_kernel(x_ref, ba_ref, bb_ref, bc_ref, smg_ref, wb_ref, wo_ref, y_ref):
    m = None
    for n, br in enumerate((ba_ref, bb_ref, bc_ref)):
        proj = _dot(br[...], wb_ref[n])
        term = smg_ref[:, n * D_MODEL:(n + 1) * D_MODEL].astype(F32) * proj
        m = term if m is None else m + term
    y_ref[...] = x_ref[...] + _dot(m.astype(BF16), wo_ref[...])


def _merge(x2d, br_a, br_b, br_c, smg, wb, wo, tm):
    n = x2d.shape[0]

    def row(w):
        return pl.BlockSpec((tm, w), lambda i: (i, 0))

    return pl.pallas_call(
        _merge_kernel,
        grid=(n // tm,),
        in_specs=[row(D_MODEL), row(W_BR), row(W_BR), row(W_BR), row(N_BRANCH * D_MODEL),
                  pl.BlockSpec((N_BRANCH, W_BR, D_MODEL), lambda i: (0, 0, 0)),
                  pl.BlockSpec((D_MODEL, D_MODEL), lambda i: (0, 0))],
        out_specs=row(D_MODEL),
        out_shape=jax.ShapeDtypeStruct((n, D_MODEL), F32),
        compiler_params=pltpu.CompilerParams(dimension_semantics=("arbitrary",), vmem_limit_bytes=VMEM_LIMIT),
        name="merge",
    )(x2d, br_a, br_b, br_c, smg, wb, wo)


def _pad_rows(a, rows):
    return jnp.pad(a, ((0, 0), (0, rows - a.shape[1]), (0, 0)))


def kernel(x_prompt, x_sample, cache_a_k, cache_a_v, cache_b_k, cache_b_v, cache_b_kidx, state_c, rel_bias,
           norm_g, w_in, a_qk_g, a_lambda, a_subln_g, b_qk_g, c_gn_g, w_branch, w_out):
    bp, tp, _ = x_prompt.shape
    bs, ts, _ = x_sample.shape
    depth = w_in.shape[0]
    past = cache_a_k.shape[2]
    assert tp % TQ == 0 and ts == CHUNK and past % LANE == 0
    pos_p = jnp.arange(tp, dtype=jnp.int32)
    pos_s = past + jnp.arange(ts, dtype=jnp.int32)
    topk_p = min(TOPK_MAX, tp // 4)
    topk_s = min(TOPK_MAX, (past + ts) // 4)
    log_gamma = jnp.log(1.0 - 2.0 ** (-5.0 - jnp.arange(H_C, dtype=F32)))

    slab_a, mslab = _bias_slabs(rel_bias[:, :H_A])
    slab_b, _ = _bias_slabs(rel_bias[:, H_A:])
    lane_i = jnp.arange(LANE)
    g64 = (lane_i[:, None] // 64 == lane_i[None, :] // 64).astype(BF16)
    tri = (lane_i[:, None] <= lane_i[None, :]).astype(BF16)
    c_prompt = 256 if tp % 256 == 0 else CHUNK
    tm_p = 512
    tm_s = 256 if (bs * ts) % 256 == 0 else ts
    qb_s = past // LANE
    ka_t = jnp.transpose(cache_a_k, (0, 1, 3, 4, 5, 2)).reshape(depth, bs, 512, past)
    va = cache_a_v.reshape(depth, bs, past * H_A, DV_A)
    kb_t, vb_t, kib_t = (jnp.transpose(c, (0, 1, 3, 2)) for c in (cache_b_k, cache_b_v, cache_b_kidx))

    yp = x_prompt.reshape(bp * tp, D_MODEL)
    ys = x_sample.reshape(bs * ts, D_MODEL)
    outs_p = {k: [] for k in ("ak", "av", "bk", "bv", "bki", "sc")}
    outs_s = {k: [] for k in ("ak", "av", "bk", "bv", "bki", "sc")}
    for l in range(depth):
        lam_init = 0.8 - 0.6 * math.exp(-0.3 * l)
        w_head = w_in[l, :, :HEAD_COLS].astype(BF16)
        w_tail = w_in[l, :, TAIL_START:TAIL_START + TAIL_COLS].astype(BF16)
        wb = w_branch[l].astype(BF16)
        wo = w_out[l].astype(BF16)

        P = _project(yp, pos_p, tm_p, norm_g[l], w_head, w_tail, a_qk_g[l], b_qk_g[l], g64, True)
        r3 = lambda a: a.reshape(bp, tp, a.shape[-1])
        br_a = _attn_a(r3(P["aq"]), r3(P["sag"]), slab_a, a_lambda[l], a_subln_g[l],
                       [(r3(P["akb"]), r3(P["avb"]), False, None)], tq=TQ, qb0=0, lam_init=lam_init)
        br_b = _dsa(r3(P["bq"]), r3(P["bqi"]), r3(P["wif"]), r3(P["sbg"]), slab_b, mslab, tri,
                    [(r3(P["kkb"]), r3(P["vvb"]), r3(P["kib"]), False, None)], tq=TQ, qb0=0, topk=topk_p)
        st0 = jnp.zeros((bp, H_C // 2, LANE, 2 * DV_C), F32)
        br_c, stp = _retention(r3(P["cq"]), r3(P["ck"]), r3(P["cv"]), r3(P["scg"]), st0, log_gamma, c_gn_g[l],
                               c=c_prompt)
        yp = _merge(yp, br_a.reshape(bp * tp, W_BR), br_b.reshape(bp * tp, W_BR), br_c.reshape(bp * tp, W_BR),
                    P["smg"], wb, wo, tm_p)
        outs_p["ak"].append(jnp.transpose(P["ak"].reshape(bp, H_A, 2, DH_A, tp), (0, 4, 1, 2, 3)))
        outs_p["av"].append(P["av"].reshape(bp, tp, H_A, DV_A))
        outs_p["bk"].append(jnp.transpose(P["bk"], (0, 2, 1)))
        outs_p["bv"].append(jnp.transpose(P["bv"], (0, 2, 1)))
        outs_p["bki"].append(jnp.transpose(P["bki"], (0, 2, 1)))
        outs_p["sc"].append(_pairs_to_state(stp))

        S = _project(ys, pos_s, tm_s, norm_g[l], w_head, w_tail, a_qk_g[l], b_qk_g[l], g64, False)
        r3s = lambda a: a.reshape(bs, ts, a.shape[-1])
        rk = lambda a: _pad_rows(r3s(a), LANE)
        br_a = _attn_a(r3s(S["aq"]), r3s(S["sag"]), slab_a, a_lambda[l], a_subln_g[l],
                       [(ka_t, va, True, l), (rk(S["akb"]), rk(S["avb"]), False, None)],
                       tq=ts, qb0=qb_s, lam_init=lam_init)
        br_b = _dsa(r3s(S["bq"]), r3s(S["bqi"]), r3s(S["wif"]), r3s(S["sbg"]), slab_b, mslab, tri,
                    [(kb_t, vb_t, kib_t, True, l), (rk(S["kkb"]), rk(S["vvb"]), rk(S["kib"]), False, None)],
                    tq=ts, qb0=qb_s, topk=topk_s)
        br_c, sts = _retention(r3s(S["cq"]), r3s(S["ck"]), r3s(S["cv"]), r3s(S["scg"]),
                               _state_to_pairs(state_c[l].astype(F32)), log_gamma, c_gn_g[l], c=ts)
        ys = _merge(ys, br_a.reshape(bs * ts, W_BR), br_b.reshape(bs * ts, W_BR), br_c.reshape(bs * ts, W_BR),
                    S["smg"], wb, wo, tm_s)
        outs_s["ak"].append(S["ak"].reshape(bs, ts, H_A, 2, DH_A))
        outs_s["av"].append(S["av"].reshape(bs, ts, H_A, DV_A))
        outs_s["bk"].append(S["bk"].reshape(bs, ts, DH_B))
        outs_s["bv"].append(S["bv"].reshape(bs, ts, DH_B))
        outs_s["bki"].append(S["bki"].reshape(bs, ts, D_IDX))
        outs_s["sc"].append(_pairs_to_state(sts).astype(state_c.dtype))

    order = ("ak", "av", "bk", "bv", "bki", "sc")
    return ((yp.reshape(bp, tp, D_MODEL), ys.reshape(bs, ts, D_MODEL))
            + tuple(jnp.stack(outs_p[k]) for k in order)
            + tuple(jnp.stack(outs_s[k]) for k in order))
```

```python
import functools
import math

import jax
import jax.numpy as jnp
from jax import lax
from jax.experimental import pallas as pl
from jax.experimental.pallas import tpu as pltpu

D_MODEL = 1024
CHUNK = 64
EPS = 1e-6
H_A = 4
DH_A = 64
DV_A = 128
H_B = 8
DH_B = 64
H_IDX = 4
D_IDX = 64
TOPK_MAX = 256
H_C = 4
DK_C = 64
DV_C = 128
W_BR = 512
N_BRANCH = 3
NUM_BUCKETS = 32
MAX_DISTANCE = 128
ROPE_BASE = 10000.0

LANE = 128
TQ = 128
NEG = -1e30
LOG2E = math.log2(math.e)
SEARCH_UNROLL = 8
ATTN_A_VARIANTS = 8
DSA_VARIANTS = 8
RADIX4_MAX_TILES = 4
INT_MIN = -(2 ** 31)
F32 = jnp.float32
BF16 = jnp.bfloat16
VMEM_LIMIT = 56 * 1024 * 1024

_SEC = {"aq": (0, 0, 512), "ak": (0, 512, 512), "av": (0, 1024, 512), "ag": (0, 1536, 512), "bq": (0, 2048, 512),
        "kv": (0, 2560, 128), "bqi": (0, 2688, 256), "kiw": (0, 2944, 128),
        "bg": (1, 0, 512), "cq": (1, 512, 256), "ck": (1, 768, 256), "cv": (1, 1024, 512), "cg": (1, 1536, 512),
        "mg": (1, 2048, 3072)}
HEAD_COLS = 3072
TAIL_START = 3012
TAIL_COLS = 5120


def _dot(a, b):
    return jnp.dot(a, b, preferred_element_type=F32)


def _dot_nt(a, b):
    return lax.dot_general(a, b, (((1,), (1,)), ((), ())), preferred_element_type=F32)


def _dot_tn(a, b):
    return lax.dot_general(a, b, (((0,), (0,)), ((), ())), preferred_element_type=F32)


def _proj_kernel(x_ref, g_ref, wh_ref, wt_ref, cq_cos_ref, cq_sin_ref, ck_cos_ref, ck_sin_ref,
                 gaq_ref, gak_ref, gbq_ref, gbk_ref, g64_ref,
                 aq_ref, ak_ref, akb_ref, av_ref, avb_ref, sag_ref, bq_ref, bqi_ref,
                 bk_ref, kkb_ref, bv_ref, vvb_ref, bki_ref, kib_ref, wif_ref, sbg_ref,
                 cq_ref, ck_ref, cv_ref, scg_ref, smg_ref, *, seq_minor):
    x = x_ref[...]
    ms = jnp.mean(x * x, axis=-1, keepdims=True)
    xn = ((x * lax.rsqrt(ms + EPS)) * g_ref[...]).astype(BF16)
    w_refs = (wh_ref, wt_ref)

    def mm(name, sub=None):
        op, c0, n = _SEC[name]
        if sub is not None:
            c0, n = c0 + sub[0], sub[1]
        return _dot(xn, w_refs[op][:, c0:c0 + n])

    def group_norm64(z, gain):
        zz = (z * z).astype(BF16)
        parts = [_dot(zz[:, j * LANE:(j + 1) * LANE], g64_ref[...]) for j in range(z.shape[1] // LANE)]
        ss = parts[0] if len(parts) == 1 else jnp.concatenate(parts, axis=1)
        return (z * lax.rsqrt(ss * (1.0 / 64.0) + EPS)) * gain

    lane = lax.broadcasted_iota(jnp.int32, (x.shape[0], LANE), 1)
    lo = lane < 64

    def rotate_half(z):
        first = (lane & 63) < DK_C // 2
        cols = []
        for j in range(z.shape[1] // LANE):
            zj = z[:, j * LANE:(j + 1) * LANE]
            cols.append(jnp.where(first, -pltpu.roll(zj, LANE - DK_C // 2, 1), pltpu.roll(zj, DK_C // 2, 1)))
        return jnp.concatenate(cols, axis=1)

    aq_ref[...] = group_norm64(mm("aq"), gaq_ref[...]).astype(BF16)
    ak = group_norm64(mm("ak"), gak_ref[...])
    akb_ref[...] = ak.astype(BF16)
    av = mm("av")
    avb_ref[...] = av.astype(BF16)
    if seq_minor:
        ak_ref[0] = jnp.transpose(ak)
        for h in range(H_A):
            av_ref[pl.ds(h, av.shape[0], stride=H_A), :] = av[:, h * DV_A:(h + 1) * DV_A]
    else:
        ak_ref[...] = ak
        av_ref[...] = av
    z = mm("ag")
    sag_ref[...] = (z * jax.nn.sigmoid(z)).astype(BF16)
    bq_ref[...] = group_norm64(mm("bq"), gbq_ref[...]).astype(BF16)
    bqi_ref[...] = mm("bqi").astype(BF16)
    z = mm("kv")
    kv = jnp.where(lo, group_norm64(z, gbk_ref[...]), z)
    vk = pltpu.roll(kv, 64, 1)
    if seq_minor:
        kv_t = jnp.transpose(kv)
        bk_ref[0] = kv_t[:DH_B]
        bv_ref[0] = kv_t[DH_B:]
    else:
        bk_ref[...] = kv[:, :DH_B]
        bv_ref[...] = vk[:, :DH_B]
    kkb_ref[...] = jnp.where(lo, kv, vk).astype(BF16)
    vvb_ref[...] = jnp.where(lo, vk, kv).astype(BF16)
    z = mm("kiw")
    zr = pltpu.roll(z, 64, 1)
    if seq_minor:
        bki_ref[0] = jnp.transpose(z)[:D_IDX]
    else:
        bki_ref[...] = z[:, :D_IDX]
    kib_ref[...] = jnp.where(lo, z, zr).astype(BF16)
    wif_ref[...] = zr
    z = mm("bg")
    sbg_ref[...] = (z * jax.nn.sigmoid(z)).astype(BF16)
    z = mm("cq")
    cq_ref[...] = (z * cq_cos_ref[...] + rotate_half(z) * cq_sin_ref[...]).astype(BF16)
    z = mm("ck")
    ck_ref[...] = (z * ck_cos_ref[...] + rotate_half(z) * ck_sin_ref[...]).astype(BF16)
    cv_ref[...] = mm("cv").astype(BF16)
    z = mm("cg")
    scg_ref[...] = (z * jax.nn.sigmoid(z)).astype(BF16)
    for j in range(N_BRANCH * 2):
        z = mm("mg", (j * 512, 512))
        smg_ref[:, j * 512:(j + 1) * 512] = jax.nn.sigmoid(z).astype(BF16)


def _rope_tables(pos, rows):
    half = DK_C // 2
    inv = ROPE_BASE ** (-jnp.arange(half, dtype=F32) / half)
    ang = pos.astype(F32)[:, None] * inv[None, :]
    cos = jnp.tile(jnp.cos(ang), (1, 2 * H_C))
    sin = jnp.tile(jnp.sin(ang), (1, 2 * H_C))
    reps = max(1, rows // cos.shape[0])
    cos = jnp.tile(cos, (reps, 1))
    sin = jnp.tile(sin, (reps, 1))
    kscale = DK_C ** -0.5
    return cos, sin, cos * kscale, sin * kscale


def _project(x2d, pos, tm, norm_g, w_head, w_tail, a_qk_g, b_qk_g, g64, seq_minor):
    n = x2d.shape[0]
    seq = pos.shape[0]
    assert not seq_minor or seq % tm == 0
    nt = max(1, seq // tm)
    cq_cos, cq_sin, ck_cos, ck_sin = _rope_tables(pos, tm)
    ntab = cq_cos.shape[0] // tm
    gaq = jnp.tile(a_qk_g[0], 8)[None, :] * (LOG2E * DH_A ** -0.5)
    gak = jnp.tile(a_qk_g[1], 8)[None, :]
    gbq = jnp.tile(b_qk_g[0], 8)[None, :] * (LOG2E * DH_B ** -0.5)
    gbk = jnp.tile(b_qk_g[1], 2)[None, :]

    def row(width):
        return pl.BlockSpec((tm, width), lambda i: (i, 0))

    def const(shape, single=False):
        if single:
            return pl.BlockSpec(shape, lambda i: (0, 0), pipeline_mode=pl.Buffered(1))
        return pl.BlockSpec(shape, lambda i: (0, 0))

    tab = pl.BlockSpec((tm, 256), lambda i: (i % ntab, 0))
    outs = [("aq", 512, BF16), ("ak", 512, F32), ("akb", 512, BF16), ("av", 512, F32), ("avb", 512, BF16),
            ("sag", 512, BF16), ("bq", 512, BF16), ("bqi", 256, BF16), ("bk", 64, F32), ("kkb", 128, BF16),
            ("bv", 64, F32), ("vvb", 128, BF16), ("bki", 64, F32), ("kib", 128, BF16), ("wif", 128, F32),
            ("sbg", 512, BF16), ("cq", 256, BF16), ("ck", 256, BF16), ("cv", 512, BF16), ("scg", 512, BF16),
            ("smg", 3072, BF16)]
    def out_spec(name, w):
        if seq_minor and name in ("ak", "bk", "bv", "bki"):
            return pl.BlockSpec((1, w, tm), lambda i: (i // nt, 0, i % nt))
        if seq_minor and name == "av":
            return pl.BlockSpec((tm * H_A, DV_A), lambda i: (i, 0))
        return row(w)

    def out_shape(name, w, dt):
        if seq_minor and name in ("ak", "bk", "bv", "bki"):
            return jax.ShapeDtypeStruct((n // seq, w, seq), dt)
        if seq_minor and name == "av":
            return jax.ShapeDtypeStruct((n * H_A, DV_A), dt)
        return jax.ShapeDtypeStruct((n, w), dt)

    res = pl.pallas_call(
        functools.partial(_proj_kernel, seq_minor=seq_minor),
        grid=(n // tm,),
        in_specs=[row(D_MODEL), const((1, D_MODEL)), const((D_MODEL, HEAD_COLS), single=True),
                  const((D_MODEL, TAIL_COLS), single=True),
                  tab, tab, tab, tab,
                  const((1, 512)), const((1, 512)), const((1, 512)), const((1, 128)), const((LANE, LANE))],
        out_specs=[out_spec(name, w) for name, w, _ in outs],
        out_shape=[out_shape(name, w, dt) for name, w, dt in outs],
        compiler_params=pltpu.CompilerParams(dimension_semantics=("arbitrary",), vmem_limit_bytes=VMEM_LIMIT),
        name="proj",
    )(x2d, norm_g[None, :], w_head, w_tail, cq_cos, cq_sin, ck_cos, ck_sin, gaq, gak, gbq, gbk, g64)
    return {name: r for (name, _, _), r in zip(outs, res)}


def _t5_bucket(rel):
    nb = NUM_BUCKETS // 2
    max_exact = nb // 2
    ret = jnp.where(rel > 0, nb, 0)
    n = jnp.abs(rel)
    large = max_exact + (jnp.log(jnp.maximum(n, 1).astype(F32) / max_exact)
                         / math.log(MAX_DISTANCE / max_exact) * (nb - max_exact)).astype(jnp.int32)
    large = jnp.minimum(large, nb - 1)
    return ret + jnp.where(n < max_exact, n, large)


def _bias_slabs(bias_cols):
    i = jnp.arange(TQ, dtype=jnp.int32)[:, None]
    j = jnp.arange(LANE, dtype=jnp.int32)[None, :]
    vis0 = j < (i // CHUNK + 1) * CHUNK

    def lookup(rel):
        bucket = _t5_bucket(rel)
        out = jnp.zeros((bias_cols.shape[1],) + rel.shape, F32)
        for b in range(NUM_BUCKETS):
            out = jnp.where(bucket[None] == b, bias_cols[b].astype(F32)[:, None, None], out)
        return out

    far = lookup(-2 * LANE + j - i)
    c = far[:, :1, :1]
    tiles = [(far - c) * LOG2E, (lookup(-LANE + j - i) - c) * LOG2E,
             jnp.where(vis0[None], (lookup(j - i) - c) * LOG2E, NEG), jnp.full_like(far, NEG)]
    zero = jnp.zeros((TQ, LANE), F32)
    masks = [zero, zero, jnp.where(vis0, 0.0, -jnp.inf).astype(F32), jnp.full((TQ, LANE), -jnp.inf, F32)]
    return jnp.stack(tiles, axis=1), jnp.stack(masks, axis=0)


def _tile_types(qb, nkt):
    return [jnp.clip(kt - qb, -2, 1) + 2 for kt in range(nkt)]


def _variants(nqb, qb0, seg_rows, max_variants):
    if len(seg_rows) > 1 or nqb == 1:
        assert nqb == 1
        return [(0, 1, tuple(seg_rows), max(0, qb0 - 1))]
    nvar = min(max_variants, nqb)
    assert nqb % nvar == 0
    grp = nqb // nvar
    nkt_total = seg_rows[0] // LANE
    return [(v * grp, (v + 1) * grp, (min(nkt_total, qb0 + (v + 1) * grp) * LANE,), max(0, qb0 + v * grp - 1))
            for v in range(nvar)]


def _run_variants(qi, variants, body):
    if len(variants) == 1:
        body(variants[0][2], variants[0][3])
    else:
        for lo_q, hi_q, rows, n_far in variants:
            pl.when((qi >= lo_q) & (qi < hi_q))(functools.partial(body, rows, n_far))


def _head_stack(qp, lo, kw):
    if kw == LANE:
        zero = jnp.zeros_like(qp)
        return jnp.concatenate([jnp.where(lo, qp, zero), jnp.where(lo, zero, qp)], axis=0)
    return jnp.concatenate([qp[:, :kw], qp[:, kw:]], axis=0)


def _stack2(a, b):
    return jnp.concatenate([a, b], axis=0)


def _softmax_pv(parts, v_blocks):
    s2 = parts[0] if len(parts) == 1 else jnp.concatenate(parts, axis=1)
    m = jnp.max(s2, axis=1, keepdims=True)
    p = jnp.exp2(s2 - m)
    l = jnp.sum(p, axis=1, keepdims=True)
    pb = p.astype(BF16)
    dv = min(v.shape[0] if t else v.shape[1] for v, t in v_blocks)
    o, c0 = None, 0
    for v, t in v_blocks:
        rows = v.shape[1] if t else v.shape[0]
        pc = pb[:, c0:c0 + rows]
        term = (_dot_nt(pc, v) if t else _dot(pc, v))[:, :dv]
        o = term if o is None else o + term
        c0 += rows
    return o * (1.0 / l)


def _attn_a_kernel(*refs, tq, qb0, variants, lam_init, seg_kt):
    aq_ref, sag_ref, slab_ref, alam_ref, subg_ref = refs[:5]
    nseg = len(seg_kt)
    segs = [refs[5 + 2 * i:7 + 2 * i] for i in range(nseg)]
    out_ref = refs[5 + 2 * nseg]
    qi = pl.program_id(1)
    qb = qi + qb0
    lp = alam_ref[...]
    lam = (jnp.exp(jnp.sum(lp[0:1] * lp[1:2], axis=1, keepdims=True))
           - jnp.exp(jnp.sum(lp[2:3] * lp[3:4], axis=1, keepdims=True)) + lam_init)
    lo = lax.broadcasted_iota(jnp.int32, (tq, LANE), 1) < DH_A

    def body(seg_rows, n_far):
        nkt = sum(r // LANE for r in seg_rows)
        ty = _tile_types(qb, nkt)
        for h in range(H_A):
            hs = slice(h * LANE, (h + 1) * LANE)
            q2 = _head_stack(aq_ref[0, :, hs], lo, LANE)
            parts, kt = [], 0
            for (k_ref, _), rows, kt_major in zip(segs, seg_rows, seg_kt):
                s = (_dot(q2, k_ref[0, hs, :rows].astype(BF16)) if kt_major
                     else _dot_nt(q2, k_ref[0, :rows, hs]))
                n_plain = min(max(n_far - kt, 0), rows // LANE)
                if n_plain:
                    parts.append(s[:, :n_plain * LANE])
                for j in range(n_plain, rows // LANE):
                    b = slab_ref[h, ty[kt + j], :tq]
                    parts.append(s[:, j * LANE:(j + 1) * LANE] + _stack2(b, b))
                kt += rows // LANE
            vs = [(v_ref[0, pl.ds(h, rows, stride=H_A), :].astype(BF16) if kt_major else v_ref[0, :rows, hs], False)
                  for (_, v_ref), rows, kt_major in zip(segs, seg_rows, seg_kt)]
            o2 = _softmax_pv(parts, vs)
            o = o2[:tq] - lam * o2[tq:]
            o = o * lax.rsqrt(jnp.mean(o * o, axis=1, keepdims=True) + EPS) * subg_ref[...] * (1.0 - lam_init)
            out_ref[0, :, hs] = (o * sag_ref[0, :, hs].astype(F32)).astype(BF16)

    _run_variants(qi, variants, body)


def _seg_spec(a, layer):
    if layer is None:
        return pl.BlockSpec((1,) + a.shape[1:], lambda bi, qi: (bi, 0, 0))
    return pl.BlockSpec((None, 1) + a.shape[2:], lambda bi, qi, layer=layer: (layer, bi, 0, 0))


def _attn_a(aq, sag, slab, a_lambda, subln_g, segs, *, tq, qb0, lam_init):
    b, t, _ = aq.shape
    nqb = t // tq
    seg_kt = tuple(kt for _, _, kt, _ in segs)
    seg_rows = [k.shape[-1] if kt else k.shape[-2] for k, _, kt, _ in segs]
    variants = _variants(nqb, qb0, seg_rows, ATTN_A_VARIANTS)
    kern = functools.partial(_attn_a_kernel, tq=tq, qb0=qb0, variants=variants, lam_init=lam_init, seg_kt=seg_kt)
    qspec = pl.BlockSpec((1, tq, 512), lambda bi, qi: (bi, qi, 0))
    seg_specs, seg_args = [], []
    for k, v, _, layer in segs:
        seg_specs += [_seg_spec(k, layer), _seg_spec(v, layer)]
        seg_args += [k, v]
    return pl.pallas_call(
        kern,
        grid=(b, nqb),
        in_specs=[qspec, qspec,
                  pl.BlockSpec(slab.shape, lambda bi, qi: (0, 0, 0, 0)),
                  pl.BlockSpec((4, DH_A), lambda bi, qi: (0, 0)),
                  pl.BlockSpec((1, DV_A), lambda bi, qi: (0, 0))] + seg_specs,
        out_specs=qspec,
        out_shape=jax.ShapeDtypeStruct((b, t, 512), BF16),
        compiler_params=pltpu.CompilerParams(dimension_semantics=("arbitrary", "arbitrary"),
                                             vmem_limit_bytes=VMEM_LIMIT),
        name="attn_a",
    )(aq, sag, slab, a_lambda, subln_g[None, :], *seg_args)


def _dsa_kernel(*refs, tq, qb0, variants, topk, seg_t):
    bq_ref, bqi_ref, wif_ref, sbg_ref, slab_ref, mslab_ref, tri_ref = refs[:7]
    nseg = len(seg_t)
    segs = [refs[7 + 3 * i:10 + 3 * i] for i in range(nseg)]
    out_ref, keys_ref = refs[7 + 3 * nseg:]
    qi = pl.program_id(1)
    qb = qi + qb0
    lo = lax.broadcasted_iota(jnp.int32, (tq, LANE), 1) < DH_B
    kf = float(topk)

    def body(seg_rows, n_far):
        nkt = sum(r // LANE for r in seg_rows)
        ty = _tile_types(qb, nkt)

        wi = wif_ref[0]
        kt0 = 0
        for (_, _, ki_ref), rows, tr in zip(segs, seg_rows, seg_t):
            ki = (ki_ref[0, :, :rows] if tr else ki_ref[0, :rows, :]).astype(BF16)
            kw = ki.shape[0] if tr else ki.shape[1]
            qis = jnp.concatenate([_head_stack(bqi_ref[0, :, hp * LANE:(hp + 1) * LANE], lo, kw)
                                   for hp in range(H_IDX // 2)], axis=0)
            r = jnp.maximum(_dot(qis, ki) if tr else _dot_nt(qis, ki), 0.0)
            score = wi[:, 0:1] * r[0:tq]
            for h in range(1, H_IDX):
                score = score + wi[:, h:h + 1] * r[h * tq:(h + 1) * tq]
            score = score + jnp.concatenate([mslab_ref[ty[kt0 + j], :tq] for j in range(rows // LANE)], axis=1)
            bits = pltpu.bitcast(score, jnp.int32)
            keys_ref[:, kt0 * LANE:kt0 * LANE + rows] = bits ^ ((bits >> 31) & jnp.int32(0x7FFFFFFF))
            kt0 += rows // LANE

        def count(pred):
            acc = None
            for kt in range(nkt):
                ind = jnp.where(pred(keys_ref[:, kt * LANE:(kt + 1) * LANE]), 1.0, 0.0)
                acc = ind if acc is None else acc + ind
            return jnp.sum(acc, axis=1, keepdims=True)

        def search(i, t):
            cand = t + lax.shift_left(jnp.int32(1), 31 - i)
            return jnp.where(count(lambda x: x >= cand) >= kf, cand, t)

        def search4(i, t):
            d = lax.shift_left(jnp.int32(1), 30 - 2 * i)
            c1 = t + d
            c2 = t + lax.shift_left(d, 1)
            c3 = c2 + d
            accs = [None, None, None]
            for kt in range(nkt):
                tile = keys_ref[:, kt * LANE:(kt + 1) * LANE]
                for j, c in enumerate((c1, c2, c3)):
                    ind = jnp.where(tile >= c, 1.0, 0.0)
                    accs[j] = ind if accs[j] is None else accs[j] + ind
            n1, n2, n3 = (jnp.sum(a, axis=1, keepdims=True) for a in accs)
            return jnp.where(n3 >= kf, c3, jnp.where(n2 >= kf, c2, jnp.where(n1 >= kf, c1, t)))

        t0 = jnp.full((tq, 1), INT_MIN, jnp.int32)
        if nkt <= RADIX4_MAX_TILES:
            t = lax.fori_loop(0, 16, search4, t0, unroll=4)
        else:
            t = lax.fori_loop(0, 32, search, t0, unroll=SEARCH_UNROLL)

        need = kf - count(lambda x: x > t)
        run = jnp.zeros((tq, 1), F32)
        negm = []
        for kt in range(nkt):
            kt_keys = keys_ref[:, kt * LANE:(kt + 1) * LANE]
            eq = jnp.where(kt_keys == t, 1.0, 0.0)
            rank = _dot(eq.astype(BF16), tri_ref[...]) + run
            tie = jnp.where(rank <= need, eq, 0.0)
            negm.append(jnp.where(kt_keys > t, 0.0, (1.0 - tie) * NEG))
            run = run + jnp.sum(eq, axis=1, keepdims=True)
        far2 = [_stack2(negm[kt], negm[kt]) for kt in range(n_far)]

        for hp in range(H_B // 2):
            hs = slice(hp * LANE, (hp + 1) * LANE)
            qp = bq_ref[0, :, hs]
            parts, vs, kt = [], [], 0
            for (k_ref, v_ref, _), rows, tr in zip(segs, seg_rows, seg_t):
                k = (k_ref[0, :, :rows] if tr else k_ref[0, :rows, :]).astype(BF16)
                q2 = _head_stack(qp, lo, k.shape[0] if tr else k.shape[1])
                s = _dot(q2, k) if tr else _dot_nt(q2, k)
                for j in range(rows // LANE):
                    if kt < n_far:
                        nb2 = far2[kt]
                    else:
                        nb2 = _stack2(negm[kt] + slab_ref[2 * hp, ty[kt], :tq],
                                      negm[kt] + slab_ref[2 * hp + 1, ty[kt], :tq])
                    parts.append(s[:, j * LANE:(j + 1) * LANE] + nb2)
                    kt += 1
                vs.append(((v_ref[0, :, :rows] if tr else v_ref[0, :rows, :]).astype(BF16), tr))
            o2 = _softmax_pv(parts, vs)
            if o2.shape[1] == LANE:
                o = jnp.where(lo, o2[:tq], o2[tq:])
            else:
                o = jnp.concatenate([o2[:tq], o2[tq:]], axis=1)
            out_ref[0, :, hs] = (o * sbg_ref[0, :, hs].astype(F32)).astype(BF16)

    _run_variants(qi, variants, body)


def _dsa(bq, bqi, wif, sbg, slab, mslab, tri, segs, *, tq, qb0, topk):
    b, t, _ = bq.shape
    nqb = t // tq
    seg_t = tuple(tr for _, _, _, tr, _ in segs)
    seg_rows = [s[0].shape[-1] if s[3] else s[0].shape[-2] for s in segs]
    variants = _variants(nqb, qb0, seg_rows, DSA_VARIANTS)
    kern = functools.partial(_dsa_kernel, tq=tq, qb0=qb0, variants=variants, topk=topk, seg_t=seg_t)

    def qspec(w):
        return pl.BlockSpec((1, tq, w), lambda bi, qi: (bi, qi, 0))

    seg_specs, seg_args = [], []
    for k, v, ki, _, layer in segs:
        for a in (k, v, ki):
            seg_specs.append(_seg_spec(a, layer))
            seg_args.append(a)
    return pl.pallas_call(
        kern,
        grid=(b, nqb),
        in_specs=[qspec(512), qspec(256), qspec(LANE), qspec(512),
                  pl.BlockSpec(slab.shape, lambda bi, qi: (0, 0, 0, 0)),
                  pl.BlockSpec(mslab.shape, lambda bi, qi: (0, 0, 0)),
                  pl.BlockSpec((LANE, LANE), lambda bi, qi: (0, 0))] + seg_specs,
        out_specs=qspec(512),
        out_shape=jax.ShapeDtypeStruct((b, t, 512), BF16),
        scratch_shapes=[pltpu.VMEM((tq, sum(seg_rows)), jnp.int32)],
        compiler_params=pltpu.CompilerParams(dimension_semantics=("arbitrary", "arbitrary"),
                                             vmem_limit_bytes=VMEM_LIMIT),
        name="dsa",
    )(bq, bqi, wif, sbg, slab, mslab, tri, *seg_args)


def _ret_kernel(cq_ref, ck_ref, cv_ref, scg_ref, st_ref, dmat_ref, dq_ref, dk_ref, gc_ref, bd_ref, gn_ref,
                out_ref, sto_ref, *, c, nchunks):
    lane = lax.broadcasted_iota(jnp.int32, (c, LANE), 1)
    lo = lane < DK_C
    for p in range(H_C // 2):
        st = st_ref[0, p]
        for ci in range(nchunks):
            rows = slice(ci * c, (ci + 1) * c)
            q = cq_ref[0, rows, p * LANE:(p + 1) * LANE]
            k = ck_ref[0, rows, p * LANE:(p + 1) * LANE]
            v = cv_ref[0, rows, p * 2 * DV_C:(p + 1) * 2 * DV_C]
            qd = (q.astype(F32) * dq_ref[p]).astype(BF16)
            cross = _dot(qd, st.astype(BF16))
            zero = jnp.zeros_like(q)
            for j in range(2):
                h = 2 * p + j
                qm = jnp.where(lo, q, zero) if j == 0 else jnp.where(lo, zero, q)
                a = (_dot_nt(qm, k) * dmat_ref[h]).astype(BF16)
                o = cross[:, j * DV_C:(j + 1) * DV_C] + _dot(a, v[:, j * DV_C:(j + 1) * DV_C])
                o = o * lax.rsqrt(jnp.mean(o * o, axis=1, keepdims=True) + EPS) * gn_ref[...]
                gate = scg_ref[0, rows, h * DV_C:(h + 1) * DV_C].astype(F32)
                out_ref[0, rows, h * DV_C:(h + 1) * DV_C] = (o * gate).astype(BF16)
            kd = (k.astype(F32) * dk_ref[p]).astype(BF16)
            st = (gc_ref[p] * st + _dot_tn(kd, v)) * bd_ref[...]
        sto_ref[0, p] = st


def _retention(cq, ck, cv, scg, state_pairs, log_gamma, gn_g, *, c):
    b, t, _ = cq.shape
    nchunks = t // c
    n = jnp.arange(c, dtype=F32)
    diff = n[:, None] - n[None, :]
    dmat = jnp.where(diff >= 0, jnp.exp(log_gamma[:, None, None] * jnp.maximum(diff, 0.0)[None]), 0.0)
    decay_q = jnp.exp((n[:, None] + 1.0) * log_gamma[None, :])
    decay_k = jnp.exp((c - 1.0 - n)[:, None] * log_gamma[None, :])
    decay_c = jnp.exp(c * log_gamma)

    def lanes(tab):
        return jnp.transpose(jnp.repeat(tab, DK_C, axis=1).reshape(c, H_C // 2, LANE), (1, 0, 2))

    gc = jnp.broadcast_to(jnp.repeat(decay_c, DK_C).reshape(H_C // 2, LANE, 1), (H_C // 2, LANE, 2 * DV_C))
    bd = (jnp.arange(LANE)[:, None] // DK_C == jnp.arange(2 * DV_C)[None, :] // DV_C).astype(F32)

    def full(a):
        return pl.BlockSpec(a.shape, lambda bi: (0,) * a.ndim)

    def tok(w):
        return pl.BlockSpec((1, t, w), lambda bi: (bi, 0, 0))

    stspec = pl.BlockSpec((1, H_C // 2, LANE, 2 * DV_C), lambda bi: (bi, 0, 0, 0))
    dq, dk, gnv = lanes(decay_q), lanes(decay_k), gn_g[None, :]
    return pl.pallas_call(
        functools.partial(_ret_kernel, c=c, nchunks=nchunks),
        grid=(b,),
        in_specs=[tok(256), tok(256), tok(512), tok(512), stspec,
                  full(dmat), full(dq), full(dk), full(gc), full(bd), full(gnv)],
        out_specs=[tok(512), stspec],
        out_shape=[jax.ShapeDtypeStruct((b, t, 512), BF16),
                   jax.ShapeDtypeStruct((b, H_C // 2, LANE, 2 * DV_C), F32)],
        compiler_params=pltpu.CompilerParams(dimension_semantics=("arbitrary",), vmem_limit_bytes=VMEM_LIMIT),
        name="retention",
    )(cq, ck, cv, scg, state_pairs, dmat, dq, dk, gc, bd, gnv)


def _state_to_pairs(st):
    b = st.shape[0]
    s4 = st.reshape(b, H_C // 2, 2, DK_C, DV_C)
    z = jnp.zeros_like(s4[:, :, 0])
    top = jnp.concatenate([s4[:, :, 0], z], axis=-1)
    bot = jnp.concatenate([z, s4[:, :, 1]], axis=-1)
    return jnp.concatenate([top, bot], axis=-2)


def _pairs_to_state(sp):
    b = sp.shape[0]
    s0 = sp[:, :, :DK_C, :DV_C]
    s1 = sp[:, :, DK_C:, DV_C:]
    return jnp.stack([s0, s1], axis=2).reshape(b, H_C, DK_C, DV_C)


def _merge_kernel(x_ref, ba_ref, bb_ref, bc_ref, smg_ref, wb_ref, wo_ref, y_ref):
    m = None
    for n, br in enumerate((ba_ref, bb_ref, bc_ref)):
        proj = _dot(br[...], wb_ref[n])
        term = smg_ref[:, n * D_MODEL:(n + 1) * D_MODEL].astype(F32) * proj
        m = term if m is None else m + term
    y_ref[...] = x_ref[...] + _dot(m.astype(BF16), wo_ref[...])


def _merge(x2d, br_a, br_b, br_c, smg, wb, wo, tm):
    n = x2d.shape[0]

    def row(w):
        return pl.BlockSpec((tm, w), lambda i: (i, 0))

    return pl.pallas_call(
        _merge_kernel,
        grid=(n // tm,),
        in_specs=[row(D_MODEL), row(W_BR), row(W_BR), row(W_BR), row(N_BRANCH * D_MODEL),
                  pl.BlockSpec((N_BRANCH, W_BR, D_MODEL), lambda i: (0, 0, 0)),
                  pl.BlockSpec((D_MODEL, D_MODEL), lambda i: (0, 0))],
        out_specs=row(D_MODEL),
        out_shape=jax.ShapeDtypeStruct((n, D_MODEL), F32),
        compiler_params=pltpu.CompilerParams(dimension_semantics=("arbitrary",), vmem_limit_bytes=VMEM_LIMIT),
        name="merge",
    )(x2d, br_a, br_b, br_c, smg, wb, wo)


def _pad_rows(a, rows):
    return jnp.pad(a, ((0, 0), (0, rows - a.shape[1]), (0, 0)))


def kernel(x_prompt, x_sample, cache_a_k, cache_a_v, cache_b_k, cache_b_v, cache_b_kidx, state_c, rel_bias,
           norm_g, w_in, a_qk_g, a_lambda, a_subln_g, b_qk_g, c_gn_g, w_branch, w_out):
    bp, tp, _ = x_prompt.shape
    bs, ts, _ = x_sample.shape
    depth = w_in.shape[0]
    past = cache_a_k.shape[2]
    assert tp % TQ == 0 and ts == CHUNK and past % LANE == 0
    pos_p = jnp.arange(tp, dtype=jnp.int32)
    pos_s = past + jnp.arange(ts, dtype=jnp.int32)
    topk_p = min(TOPK_MAX, tp // 4)
    topk_s = min(TOPK_MAX, (past + ts) // 4)
    log_gamma = jnp.log(1.0 - 2.0 ** (-5.0 - jnp.arange(H_C, dtype=F32)))

    slab_a, mslab = _bias_slabs(rel_bias[:, :H_A])
    slab_b, _ = _bias_slabs(rel_bias[:, H_A:])
    lane_i = jnp.arange(LANE)
    g64 = (lane_i[:, None] // 64 == lane_i[None, :] // 64).astype(BF16)
    tri = (lane_i[:, None] <= lane_i[None, :]).astype(BF16)
    c_prompt = 256 if tp % 256 == 0 else CHUNK
    tm_p = 512
    tm_s = 256 if (bs * ts) % 256 == 0 else ts
    qb_s = past // LANE
    ka_t = jnp.transpose(cache_a_k, (0, 1, 3, 4, 5, 2)).reshape(depth, bs, 512, past)
    va = cache_a_v.reshape(depth, bs, past * H_A, DV_A)
    kb_t, vb_t, kib_t = (jnp.transpose(c, (0, 1, 3, 2)) for c in (cache_b_k, cache_b_v, cache_b_kidx))

    yp = x_prompt.reshape(bp * tp, D_MODEL)
    ys = x_sample.reshape(bs * ts, D_MODEL)
    outs_p = {k: [] for k in ("ak", "av", "bk", "bv", "bki", "sc")}
    outs_s = {k: [] for k in ("ak", "av", "bk", "bv", "bki", "sc")}
    for l in range(depth):
        lam_init = 0.8 - 0.6 * math.exp(-0.3 * l)
        w_head = w_in[l, :, :HEAD_COLS].astype(BF16)
        w_tail = w_in[l, :, TAIL_START:TAIL_START + TAIL_COLS].astype(BF16)
        wb = w_branch[l].astype(BF16)
        wo = w_out[l].astype(BF16)

        P = _project(yp, pos_p, tm_p, norm_g[l], w_head, w_tail, a_qk_g[l], b_qk_g[l], g64, True)
        r3 = lambda a: a.reshape(bp, tp, a.shape[-1])
        br_a = _attn_a(r3(P["aq"]), r3(P["sag"]), slab_a, a_lambda[l], a_subln_g[l],
                       [(r3(P["akb"]), r3(P["avb"]), False, None)], tq=TQ, qb0=0, lam_init=lam_init)
        br_b = _dsa(r3(P["bq"]), r3(P["bqi"]), r3(P["wif"]), r3(P["sbg"]), slab_b, mslab, tri,
                    [(r3(P["kkb"]), r3(P["vvb"]), r3(P["kib"]), False, None)], tq=TQ, qb0=0, topk=topk_p)
        st0 = jnp.zeros((bp, H_C // 2, LANE, 2 * DV_C), F32)
        br_c, stp = _retention(r3(P["cq"]), r3(P["ck"]), r3(P["cv"]), r3(P["scg"]), st0, log_gamma, c_gn_g[l],
                               c=c_prompt)
        yp = _merge(yp, br_a.reshape(bp * tp, W_BR), br_b.reshape(bp * tp, W_BR), br_c.reshape(bp * tp, W_BR),
                    P["smg"], wb, wo, tm_p)
        outs_p["ak"].append(jnp.transpose(P["ak"].reshape(bp, H_A, 2, DH_A, tp), (0, 4, 1, 2, 3)))
        outs_p["av"].append(P["av"].reshape(bp, tp, H_A, DV_A))
        outs_p["bk"].append(jnp.transpose(P["bk"], (0, 2, 1)))
        outs_p["bv"].append(jnp.transpose(P["bv"], (0, 2, 1)))
        outs_p["bki"].append(jnp.transpose(P["bki"], (0, 2, 1)))
        outs_p["sc"].append(_pairs_to_state(stp))

        S = _project(ys, pos_s, tm_s, norm_g[l], w_head, w_tail, a_qk_g[l], b_qk_g[l], g64, False)
        r3s = lambda a: a.reshape(bs, ts, a.shape[-1])
        rk = lambda a: _pad_rows(r3s(a), LANE)
        br_a = _attn_a(r3s(S["aq"]), r3s(S["sag"]), slab_a, a_lambda[l], a_subln_g[l],
                       [(ka_t, va, True, l), (rk(S["akb"]), rk(S["avb"]), False, None)],
                       tq=ts, qb0=qb_s, lam_init=lam_init)
        br_b = _dsa(r3s(S["bq"]), r3s(S["bqi"]), r3s(S["wif"]), r3s(S["sbg"]), slab_b, mslab, tri,
                    [(kb_t, vb_t, kib_t, True, l), (rk(S["kkb"]), rk(S["vvb"]), rk(S["kib"]), False, None)],
                    tq=ts, qb0=qb_s, topk=topk_s)
        br_c, sts = _retention(r3s(S["cq"]), r3s(S["ck"]), r3s(S["cv"]), r3s(S["scg"]),
                               _state_to_pairs(state_c[l].astype(F32)), log_gamma, c_gn_g[l], c=ts)
        ys = _merge(ys, br_a.reshape(bs * ts, W_BR), br_b.reshape(bs * ts, W_BR), br_c.reshape(bs * ts, W_BR),
                    S["smg"], wb, wo, tm_s)
        outs_s["ak"].append(S["ak"].reshape(bs, ts, H_A, 2, DH_A))
        outs_s["av"].append(S["av"].reshape(bs, ts, H_A, DV_A))
        outs_s["bk"].append(S["bk"].reshape(bs, ts, DH_B))
        outs_s["bv"].append(S["bv"].reshape(bs, ts, DH_B))
        outs_s["bki"].append(S["bki"].reshape(bs, ts, D_IDX))
        outs_s["sc"].append(_pairs_to_state(sts).astype(state_c.dtype))

    order = ("ak", "av", "bk", "bv", "bki", "sc")
    return ((yp.reshape(bp, tp, D_MODEL), ys.reshape(bs, ts, D_MODEL))
            + tuple(jnp.stack(outs_p[k]) for k in order)
            + tuple(jnp.stack(outs_s[k]) for k in order))
```

```python
import functools
import math

import jax
import jax.numpy as jnp
from jax import lax
from jax.experimental import pallas as pl
from jax.experimental.pallas import tpu as pltpu

D_MODEL = 1024
CHUNK = 64
EPS = 1e-6
H_A = 4
DH_A = 64
DV_A = 128
H_B = 8
DH_B = 64
H_IDX = 4
D_IDX = 64
TOPK_MAX = 256
H_C = 4
DK_C = 64
DV_C = 128
W_BR = 512
N_BRANCH = 3
NUM_BUCKETS = 32
MAX_DISTANCE = 128
ROPE_BASE = 10000.0

LANE = 128
TQ = 128
NEG = -1e30
LOG2E = math.log2(math.e)
SEARCH_UNROLL = 16
ATTN_A_VARIANTS = 8
DSA_VARIANTS = 8
RADIX4_MAX_TILES = 4
INT_MIN = -(2 ** 31)
F32 = jnp.float32
BF16 = jnp.bfloat16
VMEM_LIMIT = 56 * 1024 * 1024

_SEC = {"aq": (0, 0, 512), "ak": (0, 512, 512), "av": (0, 1024, 512), "ag": (0, 1536, 512), "bq": (0, 2048, 512),
        "kv": (0, 2560, 128), "bqi": (0, 2688, 256), "kiw": (0, 2944, 128),
        "bg": (1, 0, 512), "cq": (1, 512, 256), "ck": (1, 768, 256), "cv": (1, 1024, 512), "cg": (1, 1536, 512),
        "mg": (1, 2048, 3072)}
HEAD_COLS = 3072
TAIL_START = 3012
TAIL_COLS = 5120


def _dot(a, b):
    return jnp.dot(a, b, preferred_element_type=F32)


def _dot_nt(a, b):
    return lax.dot_general(a, b, (((1,), (1,)), ((), ())), preferred_element_type=F32)


def _dot_tn(a, b):
    return lax.dot_general(a, b, (((0,), (0,)), ((), ())), preferred_element_type=F32)


_STACKED = ("ak", "av", "bk", "bv", "bki")


def _proj_kernel(*refs, seq_minor, layer):
    (x_ref, g_ref, wh_ref, wt_ref, cq_cos_ref, cq_sin_ref, ck_cos_ref, ck_sin_ref,
     gaq_ref, gak_ref, gbq_ref, gbk_ref, g64_ref) = refs[:13]
    n_prev = len(_STACKED) if seq_minor and layer else 0
    prev_refs = refs[13:13 + n_prev]
    (aq_ref, ak_ref, akb_ref, av_ref, avb_ref, sag_ref, bq_ref, bqi_ref,
     bk_ref, kkb_ref, bv_ref, vvb_ref, bki_ref, kib_ref, wif_ref, sbg_ref,
     cq_ref, ck_ref, cv_ref, scg_ref, smg_ref) = refs[13 + n_prev:]
    for ref, prev in zip((ak_ref, av_ref, bk_ref, bv_ref, bki_ref), prev_refs):
        ref[:layer] = prev[...]
    x = x_ref[...]
    ms = jnp.mean(x * x, axis=-1, keepdims=True)
    xn = ((x * lax.rsqrt(ms + EPS)) * g_ref[...]).astype(BF16)
    w_refs = (wh_ref, wt_ref)

    def mm(name, sub=None):
        op, c0, n = _SEC[name]
        if sub is not None:
            c0, n = c0 + sub[0], sub[1]
        return _dot(xn, w_refs[op][:, c0:c0 + n])

    def group_norm64(z, gain):
        zz = (z * z).astype(BF16)
        parts = [_dot(zz[:, j * LANE:(j + 1) * LANE], g64_ref[...]) for j in range(z.shape[1] // LANE)]
        ss = parts[0] if len(parts) == 1 else jnp.concatenate(parts, axis=1)
        return (z * lax.rsqrt(ss * (1.0 / 64.0) + EPS)) * gain

    lane = lax.broadcasted_iota(jnp.int32, (x.shape[0], LANE), 1)
    lo = lane < 64

    def rotate_half(z):
        first = (lane & 63) < DK_C // 2
        cols = []
        for j in range(z.shape[1] // LANE):
            zj = z[:, j * LANE:(j + 1) * LANE]
            cols.append(jnp.where(first, -pltpu.roll(zj, LANE - DK_C // 2, 1), pltpu.roll(zj, DK_C // 2, 1)))
        return jnp.concatenate(cols, axis=1)

    aq_ref[...] = group_norm64(mm("aq"), gaq_ref[...]).astype(BF16)
    ak = group_norm64(mm("ak"), gak_ref[...])
    akb_ref[...] = ak.astype(BF16)
    av = mm("av")
    avb_ref[...] = av.astype(BF16)
    if seq_minor:
        ak_ref[layer, 0] = jnp.transpose(ak)
        for h in range(H_A):
            av_ref[layer, pl.ds(h, av.shape[0], stride=H_A), :] = av[:, h * DV_A:(h + 1) * DV_A]
    else:
        ak_ref[...] = ak
        av_ref[...] = av
    z = mm("ag")
    sag_ref[...] = (z * jax.nn.sigmoid(z)).astype(BF16)
    bq_ref[...] = group_norm64(mm("bq"), gbq_ref[...]).astype(BF16)
    bqi_ref[...] = mm("bqi").astype(BF16)
    z = mm("kv")
    kv = jnp.where(lo, group_norm64(z, gbk_ref[...]), z)
    vk = pltpu.roll(kv, 64, 1)
    if seq_minor:
        kv_t = jnp.transpose(kv)
        bk_ref[layer, 0] = kv_t[:DH_B]
        bv_ref[layer, 0] = kv_t[DH_B:]
    else:
        bk_ref[...] = kv[:, :DH_B]
        bv_ref[...] = vk[:, :DH_B]
    kkb_ref[...] = jnp.where(lo, kv, vk).astype(BF16)
    vvb_ref[...] = jnp.where(lo, vk, kv).astype(BF16)
    z = mm("kiw")
    zr = pltpu.roll(z, 64, 1)
    if seq_minor:
        bki_ref[layer, 0] = jnp.transpose(z)[:D_IDX]
    else:
        bki_ref[...] = z[:, :D_IDX]
    kib_ref[...] = jnp.where(lo, z, zr).astype(BF16)
    wif_ref[...] = zr
    z = mm("bg")
    sbg_ref[...] = (z * jax.nn.sigmoid(z)).astype(BF16)
    z = mm("cq")
    cq_ref[...] = (z * cq_cos_ref[...] + rotate_half(z) * cq_sin_ref[...]).astype(BF16)
    z = mm("ck")
    ck_ref[...] = (z * ck_cos_ref[...] + rotate_half(z) * ck_sin_ref[...]).astype(BF16)
    cv_ref[...] = mm("cv").astype(BF16)
    z = mm("cg")
    scg_ref[...] = (z * jax.nn.sigmoid(z)).astype(BF16)
    for j in range(N_BRANCH * 2):
        z = mm("mg", (j * 512, 512))
        smg_ref[:, j * 512:(j + 1) * 512] = jax.nn.sigmoid(z).astype(BF16)


def _rope_tables(pos, rows):
    half = DK_C // 2
    inv = ROPE_BASE ** (-jnp.arange(half, dtype=F32) / half)
    ang = pos.astype(F32)[:, None] * inv[None, :]
    cos = jnp.tile(jnp.cos(ang), (1, 2 * H_C))
    sin = jnp.tile(jnp.sin(ang), (1, 2 * H_C))
    reps = max(1, rows // cos.shape[0])
    cos = jnp.tile(cos, (reps, 1))
    sin = jnp.tile(sin, (reps, 1))
    kscale = DK_C ** -0.5
    return cos, sin, cos * kscale, sin * kscale


def _project(x2d, pos, tm, norm_g, w_head, w_tail, a_qk_g, b_qk_g, g64, seq_minor, layer=0, prev=None):
    n = x2d.shape[0]
    seq = pos.shape[0]
    assert not seq_minor or seq % tm == 0
    nt = max(1, seq // tm)
    cq_cos, cq_sin, ck_cos, ck_sin = _rope_tables(pos, tm)
    ntab = cq_cos.shape[0] // tm
    gaq = jnp.tile(a_qk_g[0], 8)[None, :] * (LOG2E * DH_A ** -0.5)
    gak = jnp.tile(a_qk_g[1], 8)[None, :]
    gbq = jnp.tile(b_qk_g[0], 8)[None, :] * (LOG2E * DH_B ** -0.5)
    gbk = jnp.tile(b_qk_g[1], 2)[None, :]

    def row(width):
        return pl.BlockSpec((tm, width), lambda i: (i, 0))

    def const(shape, single=False):
        if single:
            return pl.BlockSpec(shape, lambda i: (0, 0), pipeline_mode=pl.Buffered(1))
        return pl.BlockSpec(shape, lambda i: (0, 0))

    tab = pl.BlockSpec((tm, 256), lambda i: (i % ntab, 0))
    outs = [("aq", 512, BF16), ("ak", 512, F32), ("akb", 512, BF16), ("av", 512, F32), ("avb", 512, BF16),
            ("sag", 512, BF16), ("bq", 512, BF16), ("bqi", 256, BF16), ("bk", 64, F32), ("kkb", 128, BF16),
            ("bv", 64, F32), ("vvb", 128, BF16), ("bki", 64, F32), ("kib", 128, BF16), ("wif", 128, F32),
            ("sbg", 512, BF16), ("cq", 256, BF16), ("ck", 256, BF16), ("cv", 512, BF16), ("scg", 512, BF16),
            ("smg", 3072, BF16)]
    def stacked_spec(name, w, depth, **kw):
        if name == "av":
            return pl.BlockSpec((depth, tm * H_A, DV_A), lambda i: (0, i, 0), **kw)
        return pl.BlockSpec((depth, 1, w, tm), lambda i: (0, i // nt, 0, i % nt), **kw)

    def out_spec(name, w):
        return stacked_spec(name, w, layer + 1) if seq_minor and name in _STACKED else row(w)

    def out_shape(name, w, dt):
        if seq_minor and name == "av":
            return jax.ShapeDtypeStruct((layer + 1, n * H_A, DV_A), dt)
        if seq_minor and name in _STACKED:
            return jax.ShapeDtypeStruct((layer + 1, n // seq, w, seq), dt)
        return jax.ShapeDtypeStruct((n, w), dt)

    widths = {name: w for name, w, _ in outs}
    prev_args = [prev[name] for name in _STACKED] if seq_minor and layer else []
    prev_specs = ([stacked_spec(name, widths[name], layer, pipeline_mode=pl.Buffered(1)) for name in _STACKED]
                  if prev_args else [])

    res = pl.pallas_call(
        functools.partial(_proj_kernel, seq_minor=seq_minor, layer=layer),
        grid=(n // tm,),
        in_specs=[row(D_MODEL), const((1, D_MODEL)), const((D_MODEL, HEAD_COLS), single=True),
                  const((D_MODEL, TAIL_COLS), single=True),
                  tab, tab, tab, tab,
                  const((1, 512)), const((1, 512)), const((1, 512)), const((1, 128)), const((LANE, LANE))] + prev_specs,
        out_specs=[out_spec(name, w) for name, w, _ in outs],
        out_shape=[out_shape(name, w, dt) for name, w, dt in outs],
        compiler_params=pltpu.CompilerParams(dimension_semantics=("arbitrary",), vmem_limit_bytes=VMEM_LIMIT),
        name="proj",
    )(x2d, norm_g[None, :], w_head, w_tail, cq_cos, cq_sin, ck_cos, ck_sin, gaq, gak, gbq, gbk, g64, *prev_args)
    return {name: r for (name, _, _), r in zip(outs, res)}


def _t5_bucket(rel):
    nb = NUM_BUCKETS // 2
    max_exact = nb // 2
    ret = jnp.where(rel > 0, nb, 0)
    n = jnp.abs(rel)
    large = max_exact + (jnp.log(jnp.maximum(n, 1).astype(F32) / max_exact)
                         / math.log(MAX_DISTANCE / max_exact) * (nb - max_exact)).astype(jnp.int32)
    large = jnp.minimum(large, nb - 1)
    return ret + jnp.where(n < max_exact, n, large)


def _bias_slabs(bias_cols):
    i = jnp.arange(TQ, dtype=jnp.int32)[:, None]
    j = jnp.arange(LANE, dtype=jnp.int32)[None, :]
    vis0 = j < (i // CHUNK + 1) * CHUNK

    def lookup(rel):
        bucket = _t5_bucket(rel)
        out = jnp.zeros((bias_cols.shape[1],) + rel.shape, F32)
        for b in range(NUM_BUCKETS):
            out = jnp.where(bucket[None] == b, bias_cols[b].astype(F32)[:, None, None], out)
        return out

    far = lookup(-2 * LANE + j - i)
    c = far[:, :1, :1]
    tiles = [(far - c) * LOG2E, (lookup(-LANE + j - i) - c) * LOG2E,
             jnp.where(vis0[None], (lookup(j - i) - c) * LOG2E, NEG), jnp.full_like(far, NEG)]
    zero = jnp.zeros((TQ, LANE), F32)
    masks = [zero, zero, jnp.where(vis0, 0.0, -jnp.inf).astype(F32), jnp.full((TQ, LANE), -jnp.inf, F32)]
    return jnp.stack(tiles, axis=1), jnp.stack(masks, axis=0)


def _tile_types(qb, nkt):
    return [jnp.clip(kt - qb, -2, 1) + 2 for kt in range(nkt)]


def _variants(nqb, qb0, seg_rows, max_variants):
    if len(seg_rows) > 1 or nqb == 1:
        assert nqb == 1
        return [(0, 1, tuple(seg_rows), max(0, qb0 - 1))]
    nvar = min(max_variants, nqb)
    assert nqb % nvar == 0
    grp = nqb // nvar
    nkt_total = seg_rows[0] // LANE
    return [(v * grp, (v + 1) * grp, (min(nkt_total, qb0 + (v + 1) * grp) * LANE,), max(0, qb0 + v * grp - 1))
            for v in range(nvar)]


def _run_variants(qi, variants, body):
    if len(variants) == 1:
        body(variants[0][2], variants[0][3])
    else:
        for lo_q, hi_q, rows, n_far in variants:
            pl.when((qi >= lo_q) & (qi < hi_q))(functools.partial(body, rows, n_far))


def _head_stack(qp, lo, kw):
    if kw == LANE:
        zero = jnp.zeros_like(qp)
        return jnp.concatenate([jnp.where(lo, qp, zero), jnp.where(lo, zero, qp)], axis=0)
    return jnp.concatenate([qp[:, :kw], qp[:, kw:]], axis=0)


def _stack2(a, b):
    return jnp.concatenate([a, b], axis=0)


def _softmax_pv(parts, v_blocks):
    s2 = parts[0] if len(parts) == 1 else jnp.concatenate(parts, axis=1)
    m = jnp.max(s2, axis=1, keepdims=True)
    p = jnp.exp2(s2 - m)
    l = jnp.sum(p, axis=1, keepdims=True)
    pb = p.astype(BF16)
    dv = min(v.shape[0] if t else v.shape[1] for v, t in v_blocks)
    o, c0 = None, 0
    for v, t in v_blocks:
        rows = v.shape[1] if t else v.shape[0]
        pc = pb[:, c0:c0 + rows]
        term = (_dot_nt(pc, v) if t else _dot(pc, v))[:, :dv]
        o = term if o is None else o + term
        c0 += rows
    return o * (1.0 / l)


def _attn_a_kernel(*refs, tq, qb0, variants, lam_init, seg_kt):
    aq_ref, sag_ref, slab_ref, alam_ref, subg_ref = refs[:5]
    nseg = len(seg_kt)
    segs = [refs[5 + 2 * i:7 + 2 * i] for i in range(nseg)]
    out_ref = refs[5 + 2 * nseg]
    qi = pl.program_id(1)
    qb = qi + qb0
    lp = alam_ref[...]
    lam = (jnp.exp(jnp.sum(lp[0:1] * lp[1:2], axis=1, keepdims=True))
           - jnp.exp(jnp.sum(lp[2:3] * lp[3:4], axis=1, keepdims=True)) + lam_init)
    lo = lax.broadcasted_iota(jnp.int32, (tq, LANE), 1) < DH_A

    def body(seg_rows, n_far):
        nkt = sum(r // LANE for r in seg_rows)
        ty = _tile_types(qb, nkt)
        for h in range(H_A):
            hs = slice(h * LANE, (h + 1) * LANE)
            q2 = _head_stack(aq_ref[0, :, hs], lo, LANE)
            parts, kt = [], 0
            for (k_ref, _), rows, kt_major in zip(segs, seg_rows, seg_kt):
                s = (_dot(q2, k_ref[0, hs, :rows].astype(BF16)) if kt_major
                     else _dot_nt(q2, k_ref[0, :rows, hs]))
                n_plain = min(max(n_far - kt, 0), rows // LANE)
                if n_plain:
                    parts.append(s[:, :n_plain * LANE])
                for j in range(n_plain, rows // LANE):
                    b = slab_ref[h, ty[kt + j], :tq]
                    parts.append(s[:, j * LANE:(j + 1) * LANE] + _stack2(b, b))
                kt += rows // LANE
            vs = [(v_ref[0, pl.ds(h, rows, stride=H_A), :].astype(BF16) if kt_major else v_ref[0, :rows, hs], False)
                  for (_, v_ref), rows, kt_major in zip(segs, seg_rows, seg_kt)]
            o2 = _softmax_pv(parts, vs)
            o = o2[:tq] - lam * o2[tq:]
            o = o * lax.rsqrt(jnp.mean(o * o, axis=1, keepdims=True) + EPS) * subg_ref[...] * (1.0 - lam_init)
            out_ref[0, :, hs] = (o * sag_ref[0, :, hs].astype(F32)).astype(BF16)

    _run_variants(qi, variants, body)


def _seg_spec(a, layer):
    if layer is None:
        return pl.BlockSpec((1,) + a.shape[1:], lambda bi, qi: (bi, 0, 0))
    return pl.BlockSpec((None, 1) + a.shape[2:], lambda bi, qi, layer=layer: (layer, bi, 0, 0))


def _attn_a(aq, sag, slab, a_lambda, subln_g, segs, *, tq, qb0, lam_init):
    b, t, _ = aq.shape
    nqb = t // tq
    seg_kt = tuple(kt for _, _, kt, _ in segs)
    seg_rows = [k.shape[-1] if kt else k.shape[-2] for k, _, kt, _ in segs]
    variants = _variants(nqb, qb0, seg_rows, ATTN_A_VARIANTS)
    kern = functools.partial(_attn_a_kernel, tq=tq, qb0=qb0, variants=variants, lam_init=lam_init, seg_kt=seg_kt)
    qspec = pl.BlockSpec((1, tq, 512), lambda bi, qi: (bi, qi, 0))
    seg_specs, seg_args = [], []
    for k, v, _, layer in segs:
        seg_specs += [_seg_spec(k, layer), _seg_spec(v, layer)]
        seg_args += [k, v]
    return pl.pallas_call(
        kern,
        grid=(b, nqb),
        in_specs=[qspec, qspec,
                  pl.BlockSpec(slab.shape, lambda bi, qi: (0, 0, 0, 0)),
                  pl.BlockSpec((4, DH_A), lambda bi, qi: (0, 0)),
                  pl.BlockSpec((1, DV_A), lambda bi, qi: (0, 0))] + seg_specs,
        out_specs=qspec,
        out_shape=jax.ShapeDtypeStruct((b, t, 512), BF16),
        compiler_params=pltpu.CompilerParams(dimension_semantics=("arbitrary", "arbitrary"),
                                             vmem_limit_bytes=VMEM_LIMIT),
        name="attn_a",
    )(aq, sag, slab, a_lambda, subln_g[None, :], *seg_args)


def _dsa_kernel(*refs, tq, qb0, variants, topk, seg_t):
    bq_ref, bqi_ref, wif_ref, sbg_ref, slab_ref, mslab_ref, tri_ref = refs[:7]
    nseg = len(seg_t)
    segs = [refs[7 + 3 * i:10 + 3 * i] for i in range(nseg)]
    out_ref, keys_ref = refs[7 + 3 * nseg:]
    qi = pl.program_id(1)
    qb = qi + qb0
    lo = lax.broadcasted_iota(jnp.int32, (tq, LANE), 1) < DH_B
    kf = float(topk)

    def body(seg_rows, n_far):
        nkt = sum(r // LANE for r in seg_rows)
        ty = _tile_types(qb, nkt)

        wi = wif_ref[0]
        kt0 = 0
        for (_, _, ki_ref), rows, tr in zip(segs, seg_rows, seg_t):
            ki = (ki_ref[0, :, :rows] if tr else ki_ref[0, :rows, :]).astype(BF16)
            kw = ki.shape[0] if tr else ki.shape[1]
            qis = jnp.concatenate([_head_stack(bqi_ref[0, :, hp * LANE:(hp + 1) * LANE], lo, kw)
                                   for hp in range(H_IDX // 2)], axis=0)
            r = jnp.maximum(_dot(qis, ki) if tr else _dot_nt(qis, ki), 0.0)
            score = wi[:, 0:1] * r[0:tq]
            for h in range(1, H_IDX):
                score = score + wi[:, h:h + 1] * r[h * tq:(h + 1) * tq]
            score = score + jnp.concatenate([mslab_ref[ty[kt0 + j], :tq] for j in range(rows // LANE)], axis=1)
            bits = pltpu.bitcast(score, jnp.int32)
            keys_ref[:, kt0 * LANE:kt0 * LANE + rows] = bits ^ ((bits >> 31) & jnp.int32(0x7FFFFFFF))
            kt0 += rows // LANE

        def count(pred):
            acc = None
            for kt in range(nkt):
                ind = jnp.where(pred(keys_ref[:, kt * LANE:(kt + 1) * LANE]), 1.0, 0.0)
                acc = ind if acc is None else acc + ind
            return jnp.sum(acc, axis=1, keepdims=True)

        def search(i, t):
            cand = t + lax.shift_left(jnp.int32(1), 31 - i)
            return jnp.where(count(lambda x: x >= cand) >= kf, cand, t)

        def search4(i, t):
            d = lax.shift_left(jnp.int32(1), 30 - 2 * i)
            c1 = t + d
            c2 = t + lax.shift_left(d, 1)
            c3 = c2 + d
            accs = [None, None, None]
            for kt in range(nkt):
                tile = keys_ref[:, kt * LANE:(kt + 1) * LANE]
                for j, c in enumerate((c1, c2, c3)):
                    ind = jnp.where(tile >= c, 1.0, 0.0)
                    accs[j] = ind if accs[j] is None else accs[j] + ind
            n1, n2, n3 = (jnp.sum(a, axis=1, keepdims=True) for a in accs)
            return jnp.where(n3 >= kf, c3, jnp.where(n2 >= kf, c2, jnp.where(n1 >= kf, c1, t)))

        t0 = jnp.full((tq, 1), INT_MIN, jnp.int32)
        if nkt <= RADIX4_MAX_TILES:
            t = lax.fori_loop(0, 16, search4, t0, unroll=4)
        else:
            t = lax.fori_loop(0, 32, search, t0, unroll=SEARCH_UNROLL)

        need = kf - count(lambda x: x > t)
        run = jnp.zeros((tq, 1), F32)
        negm = []
        for kt in range(nkt):
            kt_keys = keys_ref[:, kt * LANE:(kt + 1) * LANE]
            eq = jnp.where(kt_keys == t, 1.0, 0.0)
            rank = _dot(eq.astype(BF16), tri_ref[...]) + run
            tie = jnp.where(rank <= need, eq, 0.0)
            negm.append(jnp.where(kt_keys > t, 0.0, (1.0 - tie) * NEG))
            run = run + jnp.sum(eq, axis=1, keepdims=True)
        far2 = [_stack2(negm[kt], negm[kt]) for kt in range(n_far)]

        for hp in range(H_B // 2):
            hs = slice(hp * LANE, (hp + 1) * LANE)
            qp = bq_ref[0, :, hs]
            parts, vs, kt = [], [], 0
            for (k_ref, v_ref, _), rows, tr in zip(segs, seg_rows, seg_t):
                k = (k_ref[0, :, :rows] if tr else k_ref[0, :rows, :]).astype(BF16)
                q2 = _head_stack(qp, lo, k.shape[0] if tr else k.shape[1])
                s = _dot(q2, k) if tr else _dot_nt(q2, k)
                for j in range(rows // LANE):
                    if kt < n_far:
                        nb2 = far2[kt]
                    else:
                        nb2 = _stack2(negm[kt] + slab_ref[2 * hp, ty[kt], :tq],
                                      negm[kt] + slab_ref[2 * hp + 1, ty[kt], :tq])
                    parts.append(s[:, j * LANE:(j + 1) * LANE] + nb2)
                    kt += 1
                vs.append(((v_ref[0, :, :rows] if tr else v_ref[0, :rows, :]).astype(BF16), tr))
            o2 = _softmax_pv(parts, vs)
            if o2.shape[1] == LANE:
                o = jnp.where(lo, o2[:tq], o2[tq:])
            else:
                o = jnp.concatenate([o2[:tq], o2[tq:]], axis=1)
            out_ref[0, :, hs] = (o * sbg_ref[0, :, hs].astype(F32)).astype(BF16)

    _run_variants(qi, variants, body)


def _dsa(bq, bqi, wif, sbg, slab, mslab, tri, segs, *, tq, qb0, topk):
    b, t, _ = bq.shape
    nqb = t // tq
    seg_t = tuple(tr for _, _, _, tr, _ in segs)
    seg_rows = [s[0].shape[-1] if s[3] else s[0].shape[-2] for s in segs]
    variants = _variants(nqb, qb0, seg_rows, DSA_VARIANTS)
    kern = functools.partial(_dsa_kernel, tq=tq, qb0=qb0, variants=variants, topk=topk, seg_t=seg_t)

    def qspec(w):
        return pl.BlockSpec((1, tq, w), lambda bi, qi: (bi, qi, 0))

    seg_specs, seg_args = [], []
    for k, v, ki, _, layer in segs:
        for a in (k, v, ki):
            seg_specs.append(_seg_spec(a, layer))
            seg_args.append(a)
    return pl.pallas_call(
        kern,
        grid=(b, nqb),
        in_specs=[qspec(512), qspec(256), qspec(LANE), qspec(512),
                  pl.BlockSpec(slab.shape, lambda bi, qi: (0, 0, 0, 0)),
                  pl.BlockSpec(mslab.shape, lambda bi, qi: (0, 0, 0)),
                  pl.BlockSpec((LANE, LANE), lambda bi, qi: (0, 0))] + seg_specs,
        out_specs=qspec(512),
        out_shape=jax.ShapeDtypeStruct((b, t, 512), BF16),
        scratch_shapes=[pltpu.VMEM((tq, sum(seg_rows)), jnp.int32)],
        compiler_params=pltpu.CompilerParams(dimension_semantics=("arbitrary", "arbitrary"),
                                             vmem_limit_bytes=VMEM_LIMIT),
        name="dsa",
    )(bq, bqi, wif, sbg, slab, mslab, tri, *seg_args)


def _ret_kernel(cq_ref, ck_ref, cv_ref, scg_ref, st_ref, dmat_ref, dq_ref, dk_ref, gc_ref, bd_ref, gn_ref,
                out_ref, sto_ref, *, c, nchunks):
    lane = lax.broadcasted_iota(jnp.int32, (c, LANE), 1)
    lo = lane < DK_C
    for p in range(H_C // 2):
        st = st_ref[0, p]
        for ci in range(nchunks):
            rows = slice(ci * c, (ci + 1) * c)
            q = cq_ref[0, rows, p * LANE:(p + 1) * LANE]
            k = ck_ref[0, rows, p * LANE:(p + 1) * LANE]
            v = cv_ref[0, rows, p * 2 * DV_C:(p + 1) * 2 * DV_C]
            qd = (q.astype(F32) * dq_ref[p]).astype(BF16)
            cross = _dot(qd, st.astype(BF16))
            zero = jnp.zeros_like(q)
            for j in range(2):
                h = 2 * p + j
                qm = jnp.where(lo, q, zero) if j == 0 else jnp.where(lo, zero, q)
                a = (_dot_nt(qm, k) * dmat_ref[h]).astype(BF16)
                o = cross[:, j * DV_C:(j + 1) * DV_C] + _dot(a, v[:, j * DV_C:(j + 1) * DV_C])
                o = o * lax.rsqrt(jnp.mean(o * o, axis=1, keepdims=True) + EPS) * gn_ref[...]
                gate = scg_ref[0, rows, h * DV_C:(h + 1) * DV_C].astype(F32)
                out_ref[0, rows, h * DV_C:(h + 1) * DV_C] = (o * gate).astype(BF16)
            kd = (k.astype(F32) * dk_ref[p]).astype(BF16)
            st = (gc_ref[p] * st + _dot_tn(kd, v)) * bd_ref[...]
        sto_ref[0, p] = st


def _retention(cq, ck, cv, scg, state_pairs, log_gamma, gn_g, *, c):
    b, t, _ = cq.shape
    nchunks = t // c
    n = jnp.arange(c, dtype=F32)
    diff = n[:, None] - n[None, :]
    dmat = jnp.where(diff >= 0, jnp.exp(log_gamma[:, None, None] * jnp.maximum(diff, 0.0)[None]), 0.0)
    decay_q = jnp.exp((n[:, None] + 1.0) * log_gamma[None, :])
    decay_k = jnp.exp((c - 1.0 - n)[:, None] * log_gamma[None, :])
    decay_c = jnp.exp(c * log_gamma)

    def lanes(tab):
        return jnp.transpose(jnp.repeat(tab, DK_C, axis=1).reshape(c, H_C // 2, LANE), (1, 0, 2))

    gc = jnp.broadcast_to(jnp.repeat(decay_c, DK_C).reshape(H_C // 2, LANE, 1), (H_C // 2, LANE, 2 * DV_C))
    bd = (jnp.arange(LANE)[:, None] // DK_C == jnp.arange(2 * DV_C)[None, :] // DV_C).astype(F32)

    def full(a):
        return pl.BlockSpec(a.shape, lambda bi: (0,) * a.ndim)

    def tok(w):
        return pl.BlockSpec((1, t, w), lambda bi: (bi, 0, 0))

    stspec = pl.BlockSpec((1, H_C // 2, LANE, 2 * DV_C), lambda bi: (bi, 0, 0, 0))
    dq, dk, gnv = lanes(decay_q), lanes(decay_k), gn_g[None, :]
    return pl.pallas_call(
        functools.partial(_ret_kernel, c=c, nchunks=nchunks),
        grid=(b,),
        in_specs=[tok(256), tok(256), tok(512), tok(512), stspec,
                  full(dmat), full(dq), full(dk), full(gc), full(bd), full(gnv)],
        out_specs=[tok(512), stspec],
        out_shape=[jax.ShapeDtypeStruct((b, t, 512), BF16),
                   jax.ShapeDtypeStruct((b, H_C // 2, LANE, 2 * DV_C), F32)],
        compiler_params=pltpu.CompilerParams(dimension_semantics=("arbitrary",), vmem_limit_bytes=VMEM_LIMIT),
        name="retention",
    )(cq, ck, cv, scg, state_pairs, dmat, dq, dk, gc, bd, gnv)


def _state_to_pairs(st):
    b = st.shape[0]
    s4 = st.reshape(b, H_C // 2, 2, DK_C, DV_C)
    z = jnp.zeros_like(s4[:, :, 0])
    top = jnp.concatenate([s4[:, :, 0], z], axis=-1)
    bot = jnp.concatenate([z, s4[:, :, 1]], axis=-1)
    return jnp.concatenate([top, bot], axis=-2)


def _pairs_to_state(sp):
    b = sp.shape[0]
    s0 = sp[:, :, :DK_C, :DV_C]
    s1 = sp[:, :, DK_C:, DV_C:]
    return jnp.stack([s0, s1], axis=2).reshape(b, H_C, DK_C, DV_C)


def _merge_kernel(x_ref, ba_ref, bb_ref, bc_ref, smg_ref, wb_ref, wo_ref, y_ref):
    m = None
    for n, br in enumerate((ba_ref, bb_ref, bc_ref)):
        proj = _dot(br[...], wb_ref[n])
        term = smg_ref[:, n * D_MODEL:(n + 1) * D_MODEL].astype(F32) * proj
        m = term if m is None else m + term
    y_ref[...] = x_ref[...] + _dot(m.astype(BF16), wo_ref[...])


def _merge(x2d, br_a, br_b, br_c, smg, wb, wo, tm):
    n = x2d.shape[0]

    def row(w):
        return pl.BlockSpec((tm, w), lambda i: (i, 0))

    return pl.pallas_call(
        _merge_kernel,
        grid=(n // tm,),
        in_specs=[row(D_MODEL), row(W_BR), row(W_BR), row(W_BR), row(N_BRANCH * D_MODEL),
                  pl.BlockSpec((N_BRANCH, W_BR, D_MODEL), lambda i: (0, 0, 0)),
                  pl.BlockSpec((D_MODEL, D_MODEL), lambda i: (0, 0))],
        out_specs=row(D_MODEL),
        out_shape=jax.ShapeDtypeStruct((n, D_MODEL), F32),
        compiler_params=pltpu.CompilerParams(dimension_semantics=("arbitrary",), vmem_limit_bytes=VMEM_LIMIT),
        name="merge",
    )(x2d, br_a, br_b, br_c, smg, wb, wo)


def _pad_rows(a, rows):
    return jnp.pad(a, ((0, 0), (0, rows - a.shape[1]), (0, 0)))


def kernel(x_prompt, x_sample, cache_a_k, cache_a_v, cache_b_k, cache_b_v, cache_b_kidx, state_c, rel_bias,
           norm_g, w_in, a_qk_g, a_lambda, a_subln_g, b_qk_g, c_gn_g, w_branch, w_out):
    bp, tp, _ = x_prompt.shape
    bs, ts, _ = x_sample.shape
    depth = w_in.shape[0]
    past = cache_a_k.shape[2]
    assert tp % TQ == 0 and ts == CHUNK and past % LANE == 0
    pos_p = jnp.arange(tp, dtype=jnp.int32)
    pos_s = past + jnp.arange(ts, dtype=jnp.int32)
    topk_p = min(TOPK_MAX, tp // 4)
    topk_s = min(TOPK_MAX, (past + ts) // 4)
    log_gamma = jnp.log(1.0 - 2.0 ** (-5.0 - jnp.arange(H_C, dtype=F32)))

    slab_a, mslab = _bias_slabs(rel_bias[:, :H_A])
    slab_b, _ = _bias_slabs(rel_bias[:, H_A:])
    lane_i = jnp.arange(LANE)
    g64 = (lane_i[:, None] // 64 == lane_i[None, :] // 64).astype(BF16)
    tri = (lane_i[:, None] <= lane_i[None, :]).astype(BF16)
    c_prompt = 256 if tp % 256 == 0 else CHUNK
    tm_p = 512
    tm_s = 256 if (bs * ts) % 256 == 0 else ts
    qb_s = past // LANE
    ka_t = jnp.transpose(cache_a_k, (0, 1, 3, 4, 5, 2)).reshape(depth, bs, 512, past)
    va = cache_a_v.reshape(depth, bs, past * H_A, DV_A)
    kb_t, vb_t, kib_t = (jnp.transpose(c, (0, 1, 3, 2)) for c in (cache_b_k, cache_b_v, cache_b_kidx))

    yp = x_prompt.reshape(bp * tp, D_MODEL)
    ys = x_sample.reshape(bs * ts, D_MODEL)
    P = None
    sc_p = []
    outs_s = {k: [] for k in ("ak", "av", "bk", "bv", "bki", "sc")}
    for l in range(depth):
        lam_init = 0.8 - 0.6 * math.exp(-0.3 * l)
        w_head = w_in[l, :, :HEAD_COLS].astype(BF16)
        w_tail = w_in[l, :, TAIL_START:TAIL_START + TAIL_COLS].astype(BF16)
        wb = w_branch[l].astype(BF16)
        wo = w_out[l].astype(BF16)

        P = _project(yp, pos_p, tm_p, norm_g[l], w_head, w_tail, a_qk_g[l], b_qk_g[l], g64, True, l, P)
        r3 = lambda a: a.reshape(bp, tp, a.shape[-1])
        br_a = _attn_a(r3(P["aq"]), r3(P["sag"]), slab_a, a_lambda[l], a_subln_g[l],
                       [(r3(P["akb"]), r3(P["avb"]), False, None)], tq=TQ, qb0=0, lam_init=lam_init)
        br_b = _dsa(r3(P["bq"]), r3(P["bqi"]), r3(P["wif"]), r3(P["sbg"]), slab_b, mslab, tri,
                    [(r3(P["kkb"]), r3(P["vvb"]), r3(P["kib"]), False, None)], tq=TQ, qb0=0, topk=topk_p)
        st0 = jnp.zeros((bp, H_C // 2, LANE, 2 * DV_C), F32)
        br_c, stp = _retention(r3(P["cq"]), r3(P["ck"]), r3(P["cv"]), r3(P["scg"]), st0, log_gamma, c_gn_g[l],
                               c=c_prompt)
        yp = _merge(yp, br_a.reshape(bp * tp, W_BR), br_b.reshape(bp * tp, W_BR), br_c.reshape(bp * tp, W_BR),
                    P["smg"], wb, wo, tm_p)
        sc_p.append(_pairs_to_state(stp))

        S = _project(ys, pos_s, tm_s, norm_g[l], w_head, w_tail, a_qk_g[l], b_qk_g[l], g64, False)
        r3s = lambda a: a.reshape(bs, ts, a.shape[-1])
        rk = lambda a: _pad_rows(r3s(a), LANE)
        br_a = _attn_a(r3s(S["aq"]), r3s(S["sag"]), slab_a, a_lambda[l], a_subln_g[l],
                       [(ka_t, va, True, l), (rk(S["akb"]), rk(S["avb"]), False, None)],
                       tq=ts, qb0=qb_s, lam_init=lam_init)
        br_b = _dsa(r3s(S["bq"]), r3s(S["bqi"]), r3s(S["wif"]), r3s(S["sbg"]), slab_b, mslab, tri,
                    [(kb_t, vb_t, kib_t, True, l), (rk(S["kkb"]), rk(S["vvb"]), rk(S["kib"]), False, None)],
                    tq=ts, qb0=qb_s, topk=topk_s)
        br_c, sts = _retention(r3s(S["cq"]), r3s(S["ck"]), r3s(S["cv"]), r3s(S["scg"]),
                               _state_to_pairs(state_c[l].astype(F32)), log_gamma, c_gn_g[l], c=ts)
        ys = _merge(ys, br_a.reshape(bs * ts, W_BR), br_b.reshape(bs * ts, W_BR), br_c.reshape(bs * ts, W_BR),
                    S["smg"], wb, wo, tm_s)
        outs_s["ak"].append(S["ak"].reshape(bs, ts, H_A, 2, DH_A))
        outs_s["av"].append(S["av"].reshape(bs, ts, H_A, DV_A))
        outs_s["bk"].append(S["bk"].reshape(bs, ts, DH_B))
        outs_s["bv"].append(S["bv"].reshape(bs, ts, DH_B))
        outs_s["bki"].append(S["bki"].reshape(bs, ts, D_IDX))
        outs_s["sc"].append(_pairs_to_state(sts).astype(state_c.dtype))

    caches_p = (jnp.transpose(P["ak"].reshape(depth, bp, H_A, 2, DH_A, tp), (0, 1, 5, 2, 3, 4)),
                P["av"].reshape(depth, bp, tp, H_A, DV_A),
                jnp.transpose(P["bk"], (0, 1, 3, 2)), jnp.transpose(P["bv"], (0, 1, 3, 2)),
                jnp.transpose(P["bki"], (0, 1, 3, 2)), jnp.stack(sc_p))
    order = ("ak", "av", "bk", "bv", "bki", "sc")
    return ((yp.reshape(bp, tp, D_MODEL), ys.reshape(bs, ts, D_MODEL)) + caches_p
            + tuple(jnp.stack(outs_s[k]) for k in order))
```

```python
import functools
import math

import jax
import jax.numpy as jnp
from jax import lax
from jax.experimental import pallas as pl
from jax.experimental.pallas import tpu as pltpu

D_MODEL = 1024
CHUNK = 64
EPS = 1e-6
H_A = 4
DH_A = 64
DV_A = 128
H_B = 8
DH_B = 64
H_IDX = 4
D_IDX = 64
TOPK_MAX = 256
H_C = 4
DK_C = 64
DV_C = 128
W_BR = 512
N_BRANCH = 3
NUM_BUCKETS = 32
MAX_DISTANCE = 128
ROPE_BASE = 10000.0

LANE = 128
TQ = 128
NEG = -1e30
LOG2E = math.log2(math.e)
SEARCH_UNROLL = 16
ATTN_A_VARIANTS = 8
DSA_VARIANTS = 8
RADIX4_MAX_TILES = 4
INT_MIN = -(2 ** 31)
F32 = jnp.float32
BF16 = jnp.bfloat16
VMEM_LIMIT = 60 * 1024 * 1024

_SEC = {"aq": (0, 0, 512), "ak": (0, 512, 512), "av": (0, 1024, 512), "ag": (0, 1536, 512), "bq": (0, 2048, 512),
        "kv": (0, 2560, 128), "bqi": (0, 2688, 256), "kiw": (0, 2944, 128),
        "bg": (1, 0, 512), "cq": (1, 512, 256), "ck": (1, 768, 256), "cv": (1, 1024, 512), "cg": (1, 1536, 512),
        "mg": (1, 2048, 3072)}
HEAD_COLS = 3072
TAIL_START = 3012
TAIL_COLS = 5120


def _dot(a, b):
    return jnp.dot(a, b, preferred_element_type=F32)


def _dot_nt(a, b):
    return lax.dot_general(a, b, (((1,), (1,)), ((), ())), preferred_element_type=F32)


def _dot_tn(a, b):
    return lax.dot_general(a, b, (((0,), (0,)), ((), ())), preferred_element_type=F32)


_STACKED = ("ak", "av", "bk", "bv", "bki")
_CARRIED = ("ak", "av")


def _proj_kernel(*refs, seq_minor, layer):
    (x_ref, g_ref, wh_ref, wt_ref, cq_cos_ref, cq_sin_ref, ck_cos_ref, ck_sin_ref,
     gaq_ref, gak_ref, gbq_ref, gbk_ref, g64_ref) = refs[:13]
    n_prev = len(_CARRIED) if seq_minor and layer else 0
    prev_refs = refs[13:13 + n_prev]
    (aq_ref, ak_ref, akb_ref, av_ref, avb_ref, sag_ref, bq_ref, bqi_ref,
     bk_ref, kkb_ref, bv_ref, vvb_ref, bki_ref, kib_ref, wif_ref, sbg_ref,
     cq_ref, ck_ref, cv_ref, scg_ref, smg_ref) = refs[13 + n_prev:]
    for ref, prev in zip((ak_ref, av_ref), prev_refs):
        ref[:layer] = prev[...]
    x = x_ref[...]
    ms = jnp.mean(x * x, axis=-1, keepdims=True)
    xn = ((x * lax.rsqrt(ms + EPS)) * g_ref[...]).astype(BF16)
    w_refs = (wh_ref, wt_ref)

    def mm(name, sub=None):
        op, c0, n = _SEC[name]
        if sub is not None:
            c0, n = c0 + sub[0], sub[1]
        return _dot(xn, w_refs[op][:, c0:c0 + n])

    def group_norm64(z, gain):
        zz = (z * z).astype(BF16)
        parts = [_dot(zz[:, j * LANE:(j + 1) * LANE], g64_ref[...]) for j in range(z.shape[1] // LANE)]
        ss = parts[0] if len(parts) == 1 else jnp.concatenate(parts, axis=1)
        return (z * lax.rsqrt(ss * (1.0 / 64.0) + EPS)) * gain

    lane = lax.broadcasted_iota(jnp.int32, (x.shape[0], LANE), 1)
    lo = lane < 64

    def rotate_half(z):
        first = (lane & 63) < DK_C // 2
        cols = []
        for j in range(z.shape[1] // LANE):
            zj = z[:, j * LANE:(j + 1) * LANE]
            cols.append(jnp.where(first, -pltpu.roll(zj, LANE - DK_C // 2, 1), pltpu.roll(zj, DK_C // 2, 1)))
        return jnp.concatenate(cols, axis=1)

    aq_ref[...] = group_norm64(mm("aq"), gaq_ref[...]).astype(BF16)
    ak = group_norm64(mm("ak"), gak_ref[...])
    akb_ref[...] = ak.astype(BF16)
    av = mm("av")
    avb_ref[...] = av.astype(BF16)
    if seq_minor:
        ak_ref[layer, 0] = jnp.transpose(ak)
        for h in range(H_A):
            av_ref[layer, pl.ds(h, av.shape[0], stride=H_A), :] = av[:, h * DV_A:(h + 1) * DV_A]
    else:
        ak_ref[...] = ak
        av_ref[...] = av
    z = mm("ag")
    sag_ref[...] = (z * jax.nn.sigmoid(z)).astype(BF16)
    bq_ref[...] = group_norm64(mm("bq"), gbq_ref[...]).astype(BF16)
    bqi_ref[...] = mm("bqi").astype(BF16)
    z = mm("kv")
    kv = jnp.where(lo, group_norm64(z, gbk_ref[...]), z)
    vk = pltpu.roll(kv, 64, 1)
    if seq_minor:
        kv_t = jnp.transpose(kv)
        bk_ref[0, 0] = kv_t[:DH_B]
        bv_ref[0, 0] = kv_t[DH_B:]
    else:
        bk_ref[...] = kv[:, :DH_B]
        bv_ref[...] = vk[:, :DH_B]
    kkb_ref[...] = jnp.where(lo, kv, vk).astype(BF16)
    vvb_ref[...] = jnp.where(lo, vk, kv).astype(BF16)
    z = mm("kiw")
    zr = pltpu.roll(z, 64, 1)
    if seq_minor:
        bki_ref[0, 0] = jnp.transpose(z)[:D_IDX]
    else:
        bki_ref[...] = z[:, :D_IDX]
    kib_ref[...] = jnp.where(lo, z, zr).astype(BF16)
    wif_ref[...] = zr
    z = mm("bg")
    sbg_ref[...] = (z * jax.nn.sigmoid(z)).astype(BF16)
    z = mm("cq")
    cq_ref[...] = (z * cq_cos_ref[...] + rotate_half(z) * cq_sin_ref[...]).astype(BF16)
    z = mm("ck")
    ck_ref[...] = (z * ck_cos_ref[...] + rotate_half(z) * ck_sin_ref[...]).astype(BF16)
    cv_ref[...] = mm("cv").astype(BF16)
    z = mm("cg")
    scg_ref[...] = (z * jax.nn.sigmoid(z)).astype(BF16)
    for j in range(N_BRANCH * 2):
        z = mm("mg", (j * 512, 512))
        smg_ref[:, j * 512:(j + 1) * 512] = jax.nn.sigmoid(z).astype(BF16)


def _rope_tables(pos, rows):
    half = DK_C // 2
    inv = ROPE_BASE ** (-jnp.arange(half, dtype=F32) / half)
    ang = pos.astype(F32)[:, None] * inv[None, :]
    cos = jnp.tile(jnp.cos(ang), (1, 2 * H_C))
    sin = jnp.tile(jnp.sin(ang), (1, 2 * H_C))
    reps = max(1, rows // cos.shape[0])
    cos = jnp.tile(cos, (reps, 1))
    sin = jnp.tile(sin, (reps, 1))
    kscale = DK_C ** -0.5
    return cos, sin, cos * kscale, sin * kscale


def _project(x2d, pos, tm, norm_g, w_head, w_tail, a_qk_g, b_qk_g, g64, seq_minor, layer=0, prev=None):
    n = x2d.shape[0]
    seq = pos.shape[0]
    assert not seq_minor or seq % tm == 0
    nt = max(1, seq // tm)
    cq_cos, cq_sin, ck_cos, ck_sin = _rope_tables(pos, tm)
    ntab = cq_cos.shape[0] // tm
    gaq = jnp.tile(a_qk_g[0], 8)[None, :] * (LOG2E * DH_A ** -0.5)
    gak = jnp.tile(a_qk_g[1], 8)[None, :]
    gbq = jnp.tile(b_qk_g[0], 8)[None, :] * (LOG2E * DH_B ** -0.5)
    gbk = jnp.tile(b_qk_g[1], 2)[None, :]

    def row(width):
        return pl.BlockSpec((tm, width), lambda i: (i, 0))

    def const(shape, single=False):
        if single:
            return pl.BlockSpec(shape, lambda i: (0, 0), pipeline_mode=pl.Buffered(1))
        return pl.BlockSpec(shape, lambda i: (0, 0))

    tab = pl.BlockSpec((tm, 256), lambda i: (i % ntab, 0))
    outs = [("aq", 512, BF16), ("ak", 512, F32), ("akb", 512, BF16), ("av", 512, F32), ("avb", 512, BF16),
            ("sag", 512, BF16), ("bq", 512, BF16), ("bqi", 256, BF16), ("bk", 64, F32), ("kkb", 128, BF16),
            ("bv", 64, F32), ("vvb", 128, BF16), ("bki", 64, F32), ("kib", 128, BF16), ("wif", 128, F32),
            ("sbg", 512, BF16), ("cq", 256, BF16), ("ck", 256, BF16), ("cv", 512, BF16), ("scg", 512, BF16),
            ("smg", 3072, BF16)]
    def stacked_spec(name, w, depth, **kw):
        if name == "av":
            return pl.BlockSpec((depth, tm * H_A, DV_A), lambda i: (0, i, 0), **kw)
        return pl.BlockSpec((depth, 1, w, tm), lambda i: (0, i // nt, 0, i % nt), **kw)

    def slots(name):
        return layer + 1 if name in _CARRIED else 1

    def out_spec(name, w):
        return stacked_spec(name, w, slots(name)) if seq_minor and name in _STACKED else row(w)

    def out_shape(name, w, dt):
        if seq_minor and name == "av":
            return jax.ShapeDtypeStruct((slots(name), n * H_A, DV_A), dt)
        if seq_minor and name in _STACKED:
            return jax.ShapeDtypeStruct((slots(name), n // seq, w, seq), dt)
        return jax.ShapeDtypeStruct((n, w), dt)

    widths = {name: w for name, w, _ in outs}
    prev_args = [prev[name] for name in _CARRIED] if seq_minor and layer else []
    prev_specs = [stacked_spec(name, widths[name], layer) for name in _CARRIED] if prev_args else []

    res = pl.pallas_call(
        functools.partial(_proj_kernel, seq_minor=seq_minor, layer=layer),
        grid=(n // tm,),
        in_specs=[row(D_MODEL), const((1, D_MODEL)), const((D_MODEL, HEAD_COLS), single=True),
                  const((D_MODEL, TAIL_COLS), single=True),
                  tab, tab, tab, tab,
                  const((1, 512)), const((1, 512)), const((1, 512)), const((1, 128)), const((LANE, LANE))] + prev_specs,
        out_specs=[out_spec(name, w) for name, w, _ in outs],
        out_shape=[out_shape(name, w, dt) for name, w, dt in outs],
        compiler_params=pltpu.CompilerParams(dimension_semantics=("arbitrary",), vmem_limit_bytes=VMEM_LIMIT),
        name="proj",
    )(x2d, norm_g[None, :], w_head, w_tail, cq_cos, cq_sin, ck_cos, ck_sin, gaq, gak, gbq, gbk, g64, *prev_args)
    return {name: r for (name, _, _), r in zip(outs, res)}


def _t5_bucket(rel):
    nb = NUM_BUCKETS // 2
    max_exact = nb // 2
    ret = jnp.where(rel > 0, nb, 0)
    n = jnp.abs(rel)
    large = max_exact + (jnp.log(jnp.maximum(n, 1).astype(F32) / max_exact)
                         / math.log(MAX_DISTANCE / max_exact) * (nb - max_exact)).astype(jnp.int32)
    large = jnp.minimum(large, nb - 1)
    return ret + jnp.where(n < max_exact, n, large)


def _bias_slabs(bias_cols):
    i = jnp.arange(TQ, dtype=jnp.int32)[:, None]
    j = jnp.arange(LANE, dtype=jnp.int32)[None, :]
    vis0 = j < (i // CHUNK + 1) * CHUNK

    def lookup(rel):
        bucket = _t5_bucket(rel)
        out = jnp.zeros((bias_cols.shape[1],) + rel.shape, F32)
        for b in range(NUM_BUCKETS):
            out = jnp.where(bucket[None] == b, bias_cols[b].astype(F32)[:, None, None], out)
        return out

    far = lookup(-2 * LANE + j - i)
    c = far[:, :1, :1]
    tiles = [(far - c) * LOG2E, (lookup(-LANE + j - i) - c) * LOG2E,
             jnp.where(vis0[None], (lookup(j - i) - c) * LOG2E, NEG), jnp.full_like(far, NEG)]
    zero = jnp.zeros((TQ, LANE), F32)
    masks = [zero, zero, jnp.where(vis0, 0.0, -jnp.inf).astype(F32), jnp.full((TQ, LANE), -jnp.inf, F32)]
    return jnp.stack(tiles, axis=1), jnp.stack(masks, axis=0)


def _tile_types(qb, nkt):
    return [jnp.clip(kt - qb, -2, 1) + 2 for kt in range(nkt)]


def _variants(nqb, qb0, seg_rows, max_variants):
    if len(seg_rows) > 1 or nqb == 1:
        assert nqb == 1
        return [(0, 1, tuple(seg_rows), max(0, qb0 - 1))]
    nvar = min(max_variants, nqb)
    assert nqb % nvar == 0
    grp = nqb // nvar
    nkt_total = seg_rows[0] // LANE
    return [(v * grp, (v + 1) * grp, (min(nkt_total, qb0 + (v + 1) * grp) * LANE,), max(0, qb0 + v * grp - 1))
            for v in range(nvar)]


def _run_variants(qi, variants, body):
    if len(variants) == 1:
        body(variants[0][2], variants[0][3])
    else:
        for lo_q, hi_q, rows, n_far in variants:
            pl.when((qi >= lo_q) & (qi < hi_q))(functools.partial(body, rows, n_far))


def _head_stack(qp, lo, kw):
    if kw == LANE:
        zero = jnp.zeros_like(qp)
        return jnp.concatenate([jnp.where(lo, qp, zero), jnp.where(lo, zero, qp)], axis=0)
    return jnp.concatenate([qp[:, :kw], qp[:, kw:]], axis=0)


def _stack2(a, b):
    return jnp.concatenate([a, b], axis=0)


def _softmax_pv(parts, v_blocks):
    s2 = parts[0] if len(parts) == 1 else jnp.concatenate(parts, axis=1)
    m = jnp.max(s2, axis=1, keepdims=True)
    p = jnp.exp2(s2 - m)
    l = jnp.sum(p, axis=1, keepdims=True)
    pb = p.astype(BF16)
    dv = min(v.shape[0] if t else v.shape[1] for v, t in v_blocks)
    o, c0 = None, 0
    for v, t in v_blocks:
        rows = v.shape[1] if t else v.shape[0]
        pc = pb[:, c0:c0 + rows]
        term = (_dot_nt(pc, v) if t else _dot(pc, v))[:, :dv]
        o = term if o is None else o + term
        c0 += rows
    return o * (1.0 / l)


def _attn_a_kernel(*refs, tq, qb0, variants, lam_init, seg_kt):
    aq_ref, sag_ref, slab_ref, alam_ref, subg_ref = refs[:5]
    nseg = len(seg_kt)
    segs = [refs[5 + 2 * i:7 + 2 * i] for i in range(nseg)]
    out_ref = refs[5 + 2 * nseg]
    qi = pl.program_id(1)
    qb = qi + qb0
    lp = alam_ref[...]
    lam = (jnp.exp(jnp.sum(lp[0:1] * lp[1:2], axis=1, keepdims=True))
           - jnp.exp(jnp.sum(lp[2:3] * lp[3:4], axis=1, keepdims=True)) + lam_init)
    lo = lax.broadcasted_iota(jnp.int32, (tq, LANE), 1) < DH_A

    def body(seg_rows, n_far):
        nkt = sum(r // LANE for r in seg_rows)
        ty = _tile_types(qb, nkt)
        for h in range(H_A):
            hs = slice(h * LANE, (h + 1) * LANE)
            q2 = _head_stack(aq_ref[0, :, hs], lo, LANE)
            parts, kt = [], 0
            for (k_ref, _), rows, kt_major in zip(segs, seg_rows, seg_kt):
                s = (_dot(q2, k_ref[0, hs, :rows].astype(BF16)) if kt_major
                     else _dot_nt(q2, k_ref[0, :rows, hs]))
                n_plain = min(max(n_far - kt, 0), rows // LANE)
                if n_plain:
                    parts.append(s[:, :n_plain * LANE])
                for j in range(n_plain, rows // LANE):
                    b = slab_ref[h, ty[kt + j], :tq]
                    parts.append(s[:, j * LANE:(j + 1) * LANE] + _stack2(b, b))
                kt += rows // LANE
            vs = [(v_ref[0, pl.ds(h, rows, stride=H_A), :].astype(BF16) if kt_major else v_ref[0, :rows, hs], False)
                  for (_, v_ref), rows, kt_major in zip(segs, seg_rows, seg_kt)]
            o2 = _softmax_pv(parts, vs)
            o = o2[:tq] - lam * o2[tq:]
            o = o * lax.rsqrt(jnp.mean(o * o, axis=1, keepdims=True) + EPS) * subg_ref[...] * (1.0 - lam_init)
            out_ref[0, :, hs] = (o * sag_ref[0, :, hs].astype(F32)).astype(BF16)

    _run_variants(qi, variants, body)


def _seg_spec(a, layer):
    if layer is None:
        return pl.BlockSpec((1,) + a.shape[1:], lambda bi, qi: (bi, 0, 0))
    return pl.BlockSpec((None, 1) + a.shape[2:], lambda bi, qi, layer=layer: (layer, bi, 0, 0))


def _attn_a(aq, sag, slab, a_lambda, subln_g, segs, *, tq, qb0, lam_init):
    b, t, _ = aq.shape
    nqb = t // tq
    seg_kt = tuple(kt for _, _, kt, _ in segs)
    seg_rows = [k.shape[-1] if kt else k.shape[-2] for k, _, kt, _ in segs]
    variants = _variants(nqb, qb0, seg_rows, ATTN_A_VARIANTS)
    kern = functools.partial(_attn_a_kernel, tq=tq, qb0=qb0, variants=variants, lam_init=lam_init, seg_kt=seg_kt)
    qspec = pl.BlockSpec((1, tq, 512), lambda bi, qi: (bi, qi, 0))
    seg_specs, seg_args = [], []
    for k, v, _, layer in segs:
        seg_specs += [_seg_spec(k, layer), _seg_spec(v, layer)]
        seg_args += [k, v]
    return pl.pallas_call(
        kern,
        grid=(b, nqb),
        in_specs=[qspec, qspec,
                  pl.BlockSpec(slab.shape, lambda bi, qi: (0, 0, 0, 0)),
                  pl.BlockSpec((4, DH_A), lambda bi, qi: (0, 0)),
                  pl.BlockSpec((1, DV_A), lambda bi, qi: (0, 0))] + seg_specs,
        out_specs=qspec,
        out_shape=jax.ShapeDtypeStruct((b, t, 512), BF16),
        compiler_params=pltpu.CompilerParams(dimension_semantics=("arbitrary", "arbitrary"),
                                             vmem_limit_bytes=VMEM_LIMIT),
        name="attn_a",
    )(aq, sag, slab, a_lambda, subln_g[None, :], *seg_args)


def _dsa_kernel(*refs, tq, qb0, variants, topk, seg_t):
    bq_ref, bqi_ref, wif_ref, sbg_ref, slab_ref, mslab_ref, tri_ref = refs[:7]
    nseg = len(seg_t)
    segs = [refs[7 + 3 * i:10 + 3 * i] for i in range(nseg)]
    out_ref, keys_ref = refs[7 + 3 * nseg:]
    qi = pl.program_id(1)
    qb = qi + qb0
    lo = lax.broadcasted_iota(jnp.int32, (tq, LANE), 1) < DH_B
    kf = float(topk)

    def body(seg_rows, n_far):
        nkt = sum(r // LANE for r in seg_rows)
        ty = _tile_types(qb, nkt)

        wi = wif_ref[0]
        kt0 = 0
        for (_, _, ki_ref), rows, tr in zip(segs, seg_rows, seg_t):
            ki = (ki_ref[0, :, :rows] if tr else ki_ref[0, :rows, :]).astype(BF16)
            kw = ki.shape[0] if tr else ki.shape[1]
            qis = jnp.concatenate([_head_stack(bqi_ref[0, :, hp * LANE:(hp + 1) * LANE], lo, kw)
                                   for hp in range(H_IDX // 2)], axis=0)
            r = jnp.maximum(_dot(qis, ki) if tr else _dot_nt(qis, ki), 0.0)
            score = wi[:, 0:1] * r[0:tq]
            for h in range(1, H_IDX):
                score = score + wi[:, h:h + 1] * r[h * tq:(h + 1) * tq]
            score = score + jnp.concatenate([mslab_ref[ty[kt0 + j], :tq] for j in range(rows // LANE)], axis=1)
            bits = pltpu.bitcast(score, jnp.int32)
            keys_ref[:, kt0 * LANE:kt0 * LANE + rows] = bits ^ ((bits >> 31) & jnp.int32(0x7FFFFFFF))
            kt0 += rows // LANE

        def count(pred):
            acc = None
            for kt in range(nkt):
                ind = jnp.where(pred(keys_ref[:, kt * LANE:(kt + 1) * LANE]), 1.0, 0.0)
                acc = ind if acc is None else acc + ind
            return jnp.sum(acc, axis=1, keepdims=True)

        def search(i, t):
            cand = t + lax.shift_left(jnp.int32(1), 31 - i)
            return jnp.where(count(lambda x: x >= cand) >= kf, cand, t)

        def search4(i, t):
            d = lax.shift_left(jnp.int32(1), 30 - 2 * i)
            c1 = t + d
            c2 = t + lax.shift_left(d, 1)
            c3 = c2 + d
            accs = [None, None, None]
            for kt in range(nkt):
                tile = keys_ref[:, kt * LANE:(kt + 1) * LANE]
                for j, c in enumerate((c1, c2, c3)):
                    ind = jnp.where(tile >= c, 1.0, 0.0)
                    accs[j] = ind if accs[j] is None else accs[j] + ind
            n1, n2, n3 = (jnp.sum(a, axis=1, keepdims=True) for a in accs)
            return jnp.where(n3 >= kf, c3, jnp.where(n2 >= kf, c2, jnp.where(n1 >= kf, c1, t)))

        t0 = jnp.full((tq, 1), INT_MIN, jnp.int32)
        if nkt <= RADIX4_MAX_TILES:
            t = lax.fori_loop(0, 16, search4, t0, unroll=4)
        else:
            t = lax.fori_loop(0, 32, search, t0, unroll=SEARCH_UNROLL)

        need = kf - count(lambda x: x > t)
        run = jnp.zeros((tq, 1), F32)
        negm = []
        for kt in range(nkt):
            kt_keys = keys_ref[:, kt * LANE:(kt + 1) * LANE]
            eq = jnp.where(kt_keys == t, 1.0, 0.0)
            rank = _dot(eq.astype(BF16), tri_ref[...]) + run
            tie = jnp.where(rank <= need, eq, 0.0)
            negm.append(jnp.where(kt_keys > t, 0.0, (1.0 - tie) * NEG))
            run = run + jnp.sum(eq, axis=1, keepdims=True)
        far2 = [_stack2(negm[kt], negm[kt]) for kt in range(n_far)]

        for hp in range(H_B // 2):
            hs = slice(hp * LANE, (hp + 1) * LANE)
            qp = bq_ref[0, :, hs]
            parts, vs, kt = [], [], 0
            for (k_ref, v_ref, _), rows, tr in zip(segs, seg_rows, seg_t):
                k = (k_ref[0, :, :rows] if tr else k_ref[0, :rows, :]).astype(BF16)
                q2 = _head_stack(qp, lo, k.shape[0] if tr else k.shape[1])
                s = _dot(q2, k) if tr else _dot_nt(q2, k)
                for j in range(rows // LANE):
                    if kt < n_far:
                        nb2 = far2[kt]
                    else:
                        nb2 = _stack2(negm[kt] + slab_ref[2 * hp, ty[kt], :tq],
                                      negm[kt] + slab_ref[2 * hp + 1, ty[kt], :tq])
                    parts.append(s[:, j * LANE:(j + 1) * LANE] + nb2)
                    kt += 1
                vs.append(((v_ref[0, :, :rows] if tr else v_ref[0, :rows, :]).astype(BF16), tr))
            o2 = _softmax_pv(parts, vs)
            if o2.shape[1] == LANE:
                o = jnp.where(lo, o2[:tq], o2[tq:])
            else:
                o = jnp.concatenate([o2[:tq], o2[tq:]], axis=1)
            out_ref[0, :, hs] = (o * sbg_ref[0, :, hs].astype(F32)).astype(BF16)

    _run_variants(qi, variants, body)


def _dsa(bq, bqi, wif, sbg, slab, mslab, tri, segs, *, tq, qb0, topk):
    b, t, _ = bq.shape
    nqb = t // tq
    seg_t = tuple(tr for _, _, _, tr, _ in segs)
    seg_rows = [s[0].shape[-1] if s[3] else s[0].shape[-2] for s in segs]
    variants = _variants(nqb, qb0, seg_rows, DSA_VARIANTS)
    kern = functools.partial(_dsa_kernel, tq=tq, qb0=qb0, variants=variants, topk=topk, seg_t=seg_t)

    def qspec(w):
        return pl.BlockSpec((1, tq, w), lambda bi, qi: (bi, qi, 0))

    seg_specs, seg_args = [], []
    for k, v, ki, _, layer in segs:
        for a in (k, v, ki):
            seg_specs.append(_seg_spec(a, layer))
            seg_args.append(a)
    return pl.pallas_call(
        kern,
        grid=(b, nqb),
        in_specs=[qspec(512), qspec(256), qspec(LANE), qspec(512),
                  pl.BlockSpec(slab.shape, lambda bi, qi: (0, 0, 0, 0)),
                  pl.BlockSpec(mslab.shape, lambda bi, qi: (0, 0, 0)),
                  pl.BlockSpec((LANE, LANE), lambda bi, qi: (0, 0))] + seg_specs,
        out_specs=qspec(512),
        out_shape=jax.ShapeDtypeStruct((b, t, 512), BF16),
        scratch_shapes=[pltpu.VMEM((tq, sum(seg_rows)), jnp.int32)],
        compiler_params=pltpu.CompilerParams(dimension_semantics=("arbitrary", "arbitrary"),
                                             vmem_limit_bytes=VMEM_LIMIT),
        name="dsa",
    )(bq, bqi, wif, sbg, slab, mslab, tri, *seg_args)


def _ret_kernel(cq_ref, ck_ref, cv_ref, scg_ref, st_ref, dmat_ref, dq_ref, dk_ref, gc_ref, bd_ref, gn_ref,
                out_ref, sto_ref, *, c, nchunks):
    lane = lax.broadcasted_iota(jnp.int32, (c, LANE), 1)
    lo = lane < DK_C
    for p in range(H_C // 2):
        st = st_ref[0, p]
        for ci in range(nchunks):
            rows = slice(ci * c, (ci + 1) * c)
            q = cq_ref[0, rows, p * LANE:(p + 1) * LANE]
            k = ck_ref[0, rows, p * LANE:(p + 1) * LANE]
            v = cv_ref[0, rows, p * 2 * DV_C:(p + 1) * 2 * DV_C]
            qd = (q.astype(F32) * dq_ref[p]).astype(BF16)
            cross = _dot(qd, st.astype(BF16))
            zero = jnp.zeros_like(q)
            for j in range(2):
                h = 2 * p + j
                qm = jnp.where(lo, q, zero) if j == 0 else jnp.where(lo, zero, q)
                a = (_dot_nt(qm, k) * dmat_ref[h]).astype(BF16)
                o = cross[:, j * DV_C:(j + 1) * DV_C] + _dot(a, v[:, j * DV_C:(j + 1) * DV_C])
                o = o * lax.rsqrt(jnp.mean(o * o, axis=1, keepdims=True) + EPS) * gn_ref[...]
                gate = scg_ref[0, rows, h * DV_C:(h + 1) * DV_C].astype(F32)
                out_ref[0, rows, h * DV_C:(h + 1) * DV_C] = (o * gate).astype(BF16)
            kd = (k.astype(F32) * dk_ref[p]).astype(BF16)
            st = (gc_ref[p] * st + _dot_tn(kd, v)) * bd_ref[...]
        sto_ref[0, p] = st


def _retention(cq, ck, cv, scg, state_pairs, log_gamma, gn_g, *, c):
    b, t, _ = cq.shape
    nchunks = t // c
    n = jnp.arange(c, dtype=F32)
    diff = n[:, None] - n[None, :]
    dmat = jnp.where(diff >= 0, jnp.exp(log_gamma[:, None, None] * jnp.maximum(diff, 0.0)[None]), 0.0)
    decay_q = jnp.exp((n[:, None] + 1.0) * log_gamma[None, :])
    decay_k = jnp.exp((c - 1.0 - n)[:, None] * log_gamma[None, :])
    decay_c = jnp.exp(c * log_gamma)

    def lanes(tab):
        return jnp.transpose(jnp.repeat(tab, DK_C, axis=1).reshape(c, H_C // 2, LANE), (1, 0, 2))

    gc = jnp.broadcast_to(jnp.repeat(decay_c, DK_C).reshape(H_C // 2, LANE, 1), (H_C // 2, LANE, 2 * DV_C))
    bd = (jnp.arange(LANE)[:, None] // DK_C == jnp.arange(2 * DV_C)[None, :] // DV_C).astype(F32)

    def full(a):
        return pl.BlockSpec(a.shape, lambda bi: (0,) * a.ndim)

    def tok(w):
        return pl.BlockSpec((1, t, w), lambda bi: (bi, 0, 0))

    stspec = pl.BlockSpec((1, H_C // 2, LANE, 2 * DV_C), lambda bi: (bi, 0, 0, 0))
    dq, dk, gnv = lanes(decay_q), lanes(decay_k), gn_g[None, :]
    return pl.pallas_call(
        functools.partial(_ret_kernel, c=c, nchunks=nchunks),
        grid=(b,),
        in_specs=[tok(256), tok(256), tok(512), tok(512), stspec,
                  full(dmat), full(dq), full(dk), full(gc), full(bd), full(gnv)],
        out_specs=[tok(512), stspec],
        out_shape=[jax.ShapeDtypeStruct((b, t, 512), BF16),
                   jax.ShapeDtypeStruct((b, H_C // 2, LANE, 2 * DV_C), F32)],
        compiler_params=pltpu.CompilerParams(dimension_semantics=("arbitrary",), vmem_limit_bytes=VMEM_LIMIT),
        name="retention",
    )(cq, ck, cv, scg, state_pairs, dmat, dq, dk, gc, bd, gnv)


def _state_to_pairs(st):
    b = st.shape[0]
    s4 = st.reshape(b, H_C // 2, 2, DK_C, DV_C)
    z = jnp.zeros_like(s4[:, :, 0])
    top = jnp.concatenate([s4[:, :, 0], z], axis=-1)
    bot = jnp.concatenate([z, s4[:, :, 1]], axis=-1)
    return jnp.concatenate([top, bot], axis=-2)


def _pairs_to_state(sp):
    b = sp.shape[0]
    s0 = sp[:, :, :DK_C, :DV_C]
    s1 = sp[:, :, DK_C:, DV_C:]
    return jnp.stack([s0, s1], axis=2).reshape(b, H_C, DK_C, DV_C)


def _merge_kernel(x_ref, ba_ref, bb_ref, bc_ref, smg_ref, wb_ref, wo_ref, y_ref):
    m = None
    for n, br in enumerate((ba_ref, bb_ref, bc_ref)):
        proj = _dot(br[...], wb_ref[n])
        term = smg_ref[:, n * D_MODEL:(n + 1) * D_MODEL].astype(F32) * proj
        m = term if m is None else m + term
    y_ref[...] = x_ref[...] + _dot(m.astype(BF16), wo_ref[...])


def _merge(x2d, br_a, br_b, br_c, smg, wb, wo, tm):
    n = x2d.shape[0]

    def row(w):
        return pl.BlockSpec((tm, w), lambda i: (i, 0))

    return pl.pallas_call(
        _merge_kernel,
        grid=(n // tm,),
        in_specs=[row(D_MODEL), row(W_BR), row(W_BR), row(W_BR), row(N_BRANCH * D_MODEL),
                  pl.BlockSpec((N_BRANCH, W_BR, D_MODEL), lambda i: (0, 0, 0)),
                  pl.BlockSpec((D_MODEL, D_MODEL), lambda i: (0, 0))],
        out_specs=row(D_MODEL),
        out_shape=jax.ShapeDtypeStruct((n, D_MODEL), F32),
        compiler_params=pltpu.CompilerParams(dimension_semantics=("arbitrary",), vmem_limit_bytes=VMEM_LIMIT),
        name="merge",
    )(x2d, br_a, br_b, br_c, smg, wb, wo)


def _pad_rows(a, rows):
    return jnp.pad(a, ((0, 0), (0, rows - a.shape[1]), (0, 0)))


def kernel(x_prompt, x_sample, cache_a_k, cache_a_v, cache_b_k, cache_b_v, cache_b_kidx, state_c, rel_bias,
           norm_g, w_in, a_qk_g, a_lambda, a_subln_g, b_qk_g, c_gn_g, w_branch, w_out):
    bp, tp, _ = x_prompt.shape
    bs, ts, _ = x_sample.shape
    depth = w_in.shape[0]
    past = cache_a_k.shape[2]
    assert tp % TQ == 0 and ts == CHUNK and past % LANE == 0
    pos_p = jnp.arange(tp, dtype=jnp.int32)
    pos_s = past + jnp.arange(ts, dtype=jnp.int32)
    topk_p = min(TOPK_MAX, tp // 4)
    topk_s = min(TOPK_MAX, (past + ts) // 4)
    log_gamma = jnp.log(1.0 - 2.0 ** (-5.0 - jnp.arange(H_C, dtype=F32)))

    slab_a, mslab = _bias_slabs(rel_bias[:, :H_A])
    slab_b, _ = _bias_slabs(rel_bias[:, H_A:])
    lane_i = jnp.arange(LANE)
    g64 = (lane_i[:, None] // 64 == lane_i[None, :] // 64).astype(BF16)
    tri = (lane_i[:, None] <= lane_i[None, :]).astype(BF16)
    c_prompt = 256 if tp % 256 == 0 else CHUNK
    tm_p = 512
    tm_s = 256 if (bs * ts) % 256 == 0 else ts
    qb_s = past // LANE
    ka_t = jnp.transpose(cache_a_k, (0, 1, 3, 4, 5, 2)).reshape(depth, bs, 512, past)
    va = cache_a_v.reshape(depth, bs, past * H_A, DV_A)
    kb_t, vb_t, kib_t = (jnp.transpose(c, (0, 1, 3, 2)) for c in (cache_b_k, cache_b_v, cache_b_kidx))

    yp = x_prompt.reshape(bp * tp, D_MODEL)
    ys = x_sample.reshape(bs * ts, D_MODEL)
    P = None
    sc_p, small_p = [], {k: [] for k in ("bk", "bv", "bki")}
    outs_s = {k: [] for k in ("ak", "av", "bk", "bv", "bki", "sc")}
    for l in range(depth):
        lam_init = 0.8 - 0.6 * math.exp(-0.3 * l)
        w_head = w_in[l, :, :HEAD_COLS].astype(BF16)
        w_tail = w_in[l, :, TAIL_START:TAIL_START + TAIL_COLS].astype(BF16)
        wb = w_branch[l].astype(BF16)
        wo = w_out[l].astype(BF16)

        P = _project(yp, pos_p, tm_p, norm_g[l], w_head, w_tail, a_qk_g[l], b_qk_g[l], g64, True, l, P)
        r3 = lambda a: a.reshape(bp, tp, a.shape[-1])
        br_a = _attn_a(r3(P["aq"]), r3(P["sag"]), slab_a, a_lambda[l], a_subln_g[l],
                       [(r3(P["akb"]), r3(P["avb"]), False, None)], tq=TQ, qb0=0, lam_init=lam_init)
        br_b = _dsa(r3(P["bq"]), r3(P["bqi"]), r3(P["wif"]), r3(P["sbg"]), slab_b, mslab, tri,
                    [(r3(P["kkb"]), r3(P["vvb"]), r3(P["kib"]), False, None)], tq=TQ, qb0=0, topk=topk_p)
        st0 = jnp.zeros((bp, H_C // 2, LANE, 2 * DV_C), F32)
        br_c, stp = _retention(r3(P["cq"]), r3(P["ck"]), r3(P["cv"]), r3(P["scg"]), st0, log_gamma, c_gn_g[l],
                               c=c_prompt)
        yp = _merge(yp, br_a.reshape(bp * tp, W_BR), br_b.reshape(bp * tp, W_BR), br_c.reshape(bp * tp, W_BR),
                    P["smg"], wb, wo, tm_p)
        sc_p.append(_pairs_to_state(stp))
        for k in small_p:
            small_p[k].append(P[k])

        S = _project(ys, pos_s, tm_s, norm_g[l], w_head, w_tail, a_qk_g[l], b_qk_g[l], g64, False)
        r3s = lambda a: a.reshape(bs, ts, a.shape[-1])
        rk = lambda a: _pad_rows(r3s(a), LANE)
        br_a = _attn_a(r3s(S["aq"]), r3s(S["sag"]), slab_a, a_lambda[l], a_subln_g[l],
                       [(ka_t, va, True, l), (rk(S["akb"]), rk(S["avb"]), False, None)],
                       tq=ts, qb0=qb_s, lam_init=lam_init)
        br_b = _dsa(r3s(S["bq"]), r3s(S["bqi"]), r3s(S["wif"]), r3s(S["sbg"]), slab_b, mslab, tri,
                    [(kb_t, vb_t, kib_t, True, l), (rk(S["kkb"]), rk(S["vvb"]), rk(S["kib"]), False, None)],
                    tq=ts, qb0=qb_s, topk=topk_s)
        br_c, sts = _retention(r3s(S["cq"]), r3s(S["ck"]), r3s(S["cv"]), r3s(S["scg"]),
                               _state_to_pairs(state_c[l].astype(F32)), log_gamma, c_gn_g[l], c=ts)
        ys = _merge(ys, br_a.reshape(bs * ts, W_BR), br_b.reshape(bs * ts, W_BR), br_c.reshape(bs * ts, W_BR),
                    S["smg"], wb, wo, tm_s)
        outs_s["ak"].append(S["ak"].reshape(bs, ts, H_A, 2, DH_A))
        outs_s["av"].append(S["av"].reshape(bs, ts, H_A, DV_A))
        outs_s["bk"].append(S["bk"].reshape(bs, ts, DH_B))
        outs_s["bv"].append(S["bv"].reshape(bs, ts, DH_B))
        outs_s["bki"].append(S["bki"].reshape(bs, ts, D_IDX))
        outs_s["sc"].append(_pairs_to_state(sts).astype(state_c.dtype))

    caches_p = (jnp.transpose(P["ak"].reshape(depth, bp, H_A, 2, DH_A, tp), (0, 1, 5, 2, 3, 4)),
                P["av"].reshape(depth, bp, tp, H_A, DV_A),
                *(jnp.transpose(jnp.concatenate(small_p[k], axis=0), (0, 1, 3, 2)) for k in ("bk", "bv", "bki")),
                jnp.stack(sc_p))
    order = ("ak", "av", "bk", "bv", "bki", "sc")
    return ((yp.reshape(bp, tp, D_MODEL), ys.reshape(bs, ts, D_MODEL)) + caches_p
            + tuple(jnp.stack(outs_s[k]) for k in order))
```

```python
import functools
import math

import jax
import jax.numpy as jnp
from jax import lax
from jax.experimental import pallas as pl
from jax.experimental.pallas import tpu as pltpu

D_MODEL = 1024
CHUNK = 64
EPS = 1e-6
H_A = 4
DH_A = 64
DV_A = 128
H_B = 8
DH_B = 64
H_IDX = 4
D_IDX = 64
TOPK_MAX = 256
H_C = 4
DK_C = 64
DV_C = 128
W_BR = 512
N_BRANCH = 3
NUM_BUCKETS = 32
MAX_DISTANCE = 128
ROPE_BASE = 10000.0

LANE = 128
TQ = 128
NEG = -1e30
LOG2E = math.log2(math.e)
SEARCH_UNROLL = 16
ATTN_A_VARIANTS = 16
DSA_VARIANTS = 16
RADIX4_MAX_TILES = 4
INT_MIN = -(2 ** 31)
F32 = jnp.float32
BF16 = jnp.bfloat16
VMEM_LIMIT = 60 * 1024 * 1024

_SEC = {"aq": (0, 0, 512), "ak": (0, 512, 512), "av": (0, 1024, 512), "ag": (0, 1536, 512), "bq": (0, 2048, 512),
        "kv": (0, 2560, 128), "bqi": (0, 2688, 256), "kiw": (0, 2944, 128),
        "bg": (1, 0, 512), "cq": (1, 512, 256), "ck": (1, 768, 256), "cv": (1, 1024, 512), "cg": (1, 1536, 512),
        "mg": (1, 2048, 3072)}
HEAD_COLS = 3072
TAIL_START = 3012
TAIL_COLS = 5120


def _dot(a, b):
    return jnp.dot(a, b, preferred_element_type=F32)


def _dot_nt(a, b):
    return lax.dot_general(a, b, (((1,), (1,)), ((), ())), preferred_element_type=F32)


def _dot_tn(a, b):
    return lax.dot_general(a, b, (((0,), (0,)), ((), ())), preferred_element_type=F32)


_STACKED = ("ak", "av", "bk", "bv", "bki")
_CARRIED = ("ak", "av")


def _proj_kernel(*refs, seq_minor, layer):
    (x_ref, g_ref, wh_ref, wt_ref, cq_cos_ref, cq_sin_ref, ck_cos_ref, ck_sin_ref,
     gaq_ref, gak_ref, gbq_ref, gbk_ref, g64_ref) = refs[:13]
    n_prev = len(_CARRIED) if seq_minor and layer else 0
    prev_refs = refs[13:13 + n_prev]
    (aq_ref, ak_ref, akb_ref, av_ref, avb_ref, sag_ref, bq_ref, bqi_ref,
     bk_ref, kkb_ref, bv_ref, vvb_ref, bki_ref, kib_ref, wif_ref, sbg_ref,
     cq_ref, ck_ref, cv_ref, scg_ref, smg_ref) = refs[13 + n_prev:]
    for ref, prev in zip((ak_ref, av_ref), prev_refs):
        ref[:layer] = prev[...]
    x = x_ref[...]
    ms = jnp.mean(x * x, axis=-1, keepdims=True)
    xn = ((x * lax.rsqrt(ms + EPS)) * g_ref[...]).astype(BF16)
    w_refs = (wh_ref, wt_ref)

    def mm(name, sub=None):
        op, c0, n = _SEC[name]
        if sub is not None:
            c0, n = c0 + sub[0], sub[1]
        return _dot(xn, w_refs[op][:, c0:c0 + n])

    def group_norm64(z, gain):
        zz = (z * z).astype(BF16)
        parts = [_dot(zz[:, j * LANE:(j + 1) * LANE], g64_ref[...]) for j in range(z.shape[1] // LANE)]
        ss = parts[0] if len(parts) == 1 else jnp.concatenate(parts, axis=1)
        return (z * lax.rsqrt(ss * (1.0 / 64.0) + EPS)) * gain

    lane = lax.broadcasted_iota(jnp.int32, (x.shape[0], LANE), 1)
    lo = lane < 64

    def rotate_half(z):
        first = (lane & 63) < DK_C // 2
        cols = []
        for j in range(z.shape[1] // LANE):
            zj = z[:, j * LANE:(j + 1) * LANE]
            cols.append(jnp.where(first, -pltpu.roll(zj, LANE - DK_C // 2, 1), pltpu.roll(zj, DK_C // 2, 1)))
        return jnp.concatenate(cols, axis=1)

    aq_ref[...] = group_norm64(mm("aq"), gaq_ref[...]).astype(BF16)
    ak = group_norm64(mm("ak"), gak_ref[...])
    akb_ref[...] = ak.astype(BF16)
    av = mm("av")
    avb_ref[...] = av.astype(BF16)
    if seq_minor:
        ak_ref[layer, 0] = jnp.transpose(ak)
        for h in range(H_A):
            av_ref[layer, pl.ds(h, av.shape[0], stride=H_A), :] = av[:, h * DV_A:(h + 1) * DV_A]
    else:
        ak_ref[...] = ak
        av_ref[...] = av
    z = mm("ag")
    sag_ref[...] = (z * jax.nn.sigmoid(z)).astype(BF16)
    bq_ref[...] = group_norm64(mm("bq"), gbq_ref[...]).astype(BF16)
    bqi_ref[...] = mm("bqi").astype(BF16)
    z = mm("kv")
    kv = jnp.where(lo, group_norm64(z, gbk_ref[...]), z)
    vk = pltpu.roll(kv, 64, 1)
    if seq_minor:
        kv_t = jnp.transpose(kv)
        bk_ref[0, 0] = kv_t[:DH_B]
        bv_ref[0, 0] = kv_t[DH_B:]
    else:
        bk_ref[...] = kv[:, :DH_B]
        bv_ref[...] = vk[:, :DH_B]
    kkb_ref[...] = jnp.where(lo, kv, vk).astype(BF16)
    vvb_ref[...] = jnp.where(lo, vk, kv).astype(BF16)
    z = mm("kiw")
    zr = pltpu.roll(z, 64, 1)
    if seq_minor:
        bki_ref[0, 0] = jnp.transpose(z)[:D_IDX]
    else:
        bki_ref[...] = z[:, :D_IDX]
    kib_ref[...] = jnp.where(lo, z, zr).astype(BF16)
    wif_ref[...] = zr
    z = mm("bg")
    sbg_ref[...] = (z * jax.nn.sigmoid(z)).astype(BF16)
    z = mm("cq")
    cq_ref[...] = (z * cq_cos_ref[...] + rotate_half(z) * cq_sin_ref[...]).astype(BF16)
    z = mm("ck")
    ck_ref[...] = (z * ck_cos_ref[...] + rotate_half(z) * ck_sin_ref[...]).astype(BF16)
    cv_ref[...] = mm("cv").astype(BF16)
    z = mm("cg")
    scg_ref[...] = (z * jax.nn.sigmoid(z)).astype(BF16)
    for j in range(N_BRANCH * 2):
        z = mm("mg", (j * 512, 512))
        smg_ref[:, j * 512:(j + 1) * 512] = jax.nn.sigmoid(z).astype(BF16)


def _rope_tables(pos, rows):
    half = DK_C // 2
    inv = ROPE_BASE ** (-jnp.arange(half, dtype=F32) / half)
    ang = pos.astype(F32)[:, None] * inv[None, :]
    cos = jnp.tile(jnp.cos(ang), (1, 2 * H_C))
    sin = jnp.tile(jnp.sin(ang), (1, 2 * H_C))
    reps = max(1, rows // cos.shape[0])
    cos = jnp.tile(cos, (reps, 1))
    sin = jnp.tile(sin, (reps, 1))
    kscale = DK_C ** -0.5
    return cos, sin, cos * kscale, sin * kscale


def _project(x2d, pos, tm, norm_g, w_head, w_tail, a_qk_g, b_qk_g, g64, seq_minor, layer=0, prev=None):
    n = x2d.shape[0]
    seq = pos.shape[0]
    assert not seq_minor or seq % tm == 0
    nt = max(1, seq // tm)
    cq_cos, cq_sin, ck_cos, ck_sin = _rope_tables(pos, tm)
    ntab = cq_cos.shape[0] // tm
    gaq = jnp.tile(a_qk_g[0], 8)[None, :] * (LOG2E * DH_A ** -0.5)
    gak = jnp.tile(a_qk_g[1], 8)[None, :]
    gbq = jnp.tile(b_qk_g[0], 8)[None, :] * (LOG2E * DH_B ** -0.5)
    gbk = jnp.tile(b_qk_g[1], 2)[None, :]

    def row(width):
        return pl.BlockSpec((tm, width), lambda i: (i, 0))

    def const(shape, single=False):
        if single:
            return pl.BlockSpec(shape, lambda i: (0, 0), pipeline_mode=pl.Buffered(1))
        return pl.BlockSpec(shape, lambda i: (0, 0))

    tab = pl.BlockSpec((tm, 256), lambda i: (i % ntab, 0))
    outs = [("aq", 512, BF16), ("ak", 512, F32), ("akb", 512, BF16), ("av", 512, F32), ("avb", 512, BF16),
            ("sag", 512, BF16), ("bq", 512, BF16), ("bqi", 256, BF16), ("bk", 64, F32), ("kkb", 128, BF16),
            ("bv", 64, F32), ("vvb", 128, BF16), ("bki", 64, F32), ("kib", 128, BF16), ("wif", 128, F32),
            ("sbg", 512, BF16), ("cq", 256, BF16), ("ck", 256, BF16), ("cv", 512, BF16), ("scg", 512, BF16),
            ("smg", 3072, BF16)]
    def stacked_spec(name, w, depth, **kw):
        if name == "av":
            return pl.BlockSpec((depth, tm * H_A, DV_A), lambda i: (0, i, 0), **kw)
        return pl.BlockSpec((depth, 1, w, tm), lambda i: (0, i // nt, 0, i % nt), **kw)

    def slots(name):
        return layer + 1 if name in _CARRIED else 1

    def out_spec(name, w):
        return stacked_spec(name, w, slots(name)) if seq_minor and name in _STACKED else row(w)

    def out_shape(name, w, dt):
        if seq_minor and name == "av":
            return jax.ShapeDtypeStruct((slots(name), n * H_A, DV_A), dt)
        if seq_minor and name in _STACKED:
            return jax.ShapeDtypeStruct((slots(name), n // seq, w, seq), dt)
        return jax.ShapeDtypeStruct((n, w), dt)

    widths = {name: w for name, w, _ in outs}
    prev_args = [prev[name] for name in _CARRIED] if seq_minor and layer else []
    prev_specs = [stacked_spec(name, widths[name], layer) for name in _CARRIED] if prev_args else []

    res = pl.pallas_call(
        functools.partial(_proj_kernel, seq_minor=seq_minor, layer=layer),
        grid=(n // tm,),
        in_specs=[row(D_MODEL), const((1, D_MODEL)), const((D_MODEL, HEAD_COLS), single=True),
                  const((D_MODEL, TAIL_COLS), single=True),
                  tab, tab, tab, tab,
                  const((1, 512)), const((1, 512)), const((1, 512)), const((1, 128)), const((LANE, LANE))] + prev_specs,
        out_specs=[out_spec(name, w) for name, w, _ in outs],
        out_shape=[out_shape(name, w, dt) for name, w, dt in outs],
        compiler_params=pltpu.CompilerParams(dimension_semantics=("arbitrary",), vmem_limit_bytes=VMEM_LIMIT),
        name="proj",
    )(x2d, norm_g[None, :], w_head, w_tail, cq_cos, cq_sin, ck_cos, ck_sin, gaq, gak, gbq, gbk, g64, *prev_args)
    return {name: r for (name, _, _), r in zip(outs, res)}


def _t5_bucket(rel):
    nb = NUM_BUCKETS // 2
    max_exact = nb // 2
    ret = jnp.where(rel > 0, nb, 0)
    n = jnp.abs(rel)
    large = max_exact + (jnp.log(jnp.maximum(n, 1).astype(F32) / max_exact)
                         / math.log(MAX_DISTANCE / max_exact) * (nb - max_exact)).astype(jnp.int32)
    large = jnp.minimum(large, nb - 1)
    return ret + jnp.where(n < max_exact, n, large)


def _bias_slabs(bias_cols):
    i = jnp.arange(TQ, dtype=jnp.int32)[:, None]
    j = jnp.arange(LANE, dtype=jnp.int32)[None, :]
    vis0 = j < (i // CHUNK + 1) * CHUNK

    def lookup(rel):
        bucket = _t5_bucket(rel)
        out = jnp.zeros((bias_cols.shape[1],) + rel.shape, F32)
        for b in range(NUM_BUCKETS):
            out = jnp.where(bucket[None] == b, bias_cols[b].astype(F32)[:, None, None], out)
        return out

    far = lookup(-2 * LANE + j - i)
    c = far[:, :1, :1]
    tiles = [(far - c) * LOG2E, (lookup(-LANE + j - i) - c) * LOG2E,
             jnp.where(vis0[None], (lookup(j - i) - c) * LOG2E, NEG), jnp.full_like(far, NEG)]
    zero = jnp.zeros((TQ, LANE), F32)
    masks = [zero, zero, jnp.where(vis0, 0.0, -jnp.inf).astype(F32), jnp.full((TQ, LANE), -jnp.inf, F32)]
    return jnp.stack(tiles, axis=1), jnp.stack(masks, axis=0)


def _tile_types(qb, nkt):
    return [jnp.clip(kt - qb, -2, 1) + 2 for kt in range(nkt)]


def _variants(nqb, qb0, seg_rows, max_variants):
    if len(seg_rows) > 1 or nqb == 1:
        assert nqb == 1
        return [(0, 1, tuple(seg_rows), max(0, qb0 - 1))]
    nvar = min(max_variants, nqb)
    assert nqb % nvar == 0
    grp = nqb // nvar
    nkt_total = seg_rows[0] // LANE
    return [(v * grp, (v + 1) * grp, (min(nkt_total, qb0 + (v + 1) * grp) * LANE,), max(0, qb0 + v * grp - 1))
            for v in range(nvar)]


def _run_variants(qi, variants, body):
    if len(variants) == 1:
        body(variants[0][2], variants[0][3])
    else:
        for lo_q, hi_q, rows, n_far in variants:
            pl.when((qi >= lo_q) & (qi < hi_q))(functools.partial(body, rows, n_far))


def _head_stack(qp, lo, kw):
    if kw == LANE:
        zero = jnp.zeros_like(qp)
        return jnp.concatenate([jnp.where(lo, qp, zero), jnp.where(lo, zero, qp)], axis=0)
    return jnp.concatenate([qp[:, :kw], qp[:, kw:]], axis=0)


def _stack2(a, b):
    return jnp.concatenate([a, b], axis=0)


def _softmax_pv(parts, v_blocks):
    s2 = parts[0] if len(parts) == 1 else jnp.concatenate(parts, axis=1)
    m = jnp.max(s2, axis=1, keepdims=True)
    p = jnp.exp2(s2 - m)
    l = jnp.sum(p, axis=1, keepdims=True)
    pb = p.astype(BF16)
    dv = min(v.shape[0] if t else v.shape[1] for v, t in v_blocks)
    o, c0 = None, 0
    for v, t in v_blocks:
        rows = v.shape[1] if t else v.shape[0]
        pc = pb[:, c0:c0 + rows]
        term = (_dot_nt(pc, v) if t else _dot(pc, v))[:, :dv]
        o = term if o is None else o + term
        c0 += rows
    return o * (1.0 / l)


def _attn_a_kernel(*refs, tq, qb0, variants, lam_init, seg_kt):
    aq_ref, sag_ref, slab_ref, alam_ref, subg_ref = refs[:5]
    nseg = len(seg_kt)
    segs = [refs[5 + 2 * i:7 + 2 * i] for i in range(nseg)]
    out_ref = refs[5 + 2 * nseg]
    qi = pl.program_id(1)
    qb = qi + qb0
    lp = alam_ref[...]
    lam = (jnp.exp(jnp.sum(lp[0:1] * lp[1:2], axis=1, keepdims=True))
           - jnp.exp(jnp.sum(lp[2:3] * lp[3:4], axis=1, keepdims=True)) + lam_init)
    lo = lax.broadcasted_iota(jnp.int32, (tq, LANE), 1) < DH_A

    def body(seg_rows, n_far):
        nkt = sum(r // LANE for r in seg_rows)
        ty = _tile_types(qb, nkt)
        for h in range(H_A):
            hs = slice(h * LANE, (h + 1) * LANE)
            q2 = _head_stack(aq_ref[0, :, hs], lo, LANE)
            parts, kt = [], 0
            for (k_ref, _), rows, kt_major in zip(segs, seg_rows, seg_kt):
                s = (_dot(q2, k_ref[0, hs, :rows].astype(BF16)) if kt_major
                     else _dot_nt(q2, k_ref[0, :rows, hs]))
                n_plain = min(max(n_far - kt, 0), rows // LANE)
                if n_plain:
                    parts.append(s[:, :n_plain * LANE])
                for j in range(n_plain, rows // LANE):
                    b = slab_ref[h, ty[kt + j], :tq]
                    parts.append(s[:, j * LANE:(j + 1) * LANE] + _stack2(b, b))
                kt += rows // LANE
            vs = [(v_ref[0, pl.ds(h, rows, stride=H_A), :].astype(BF16) if kt_major else v_ref[0, :rows, hs], False)
                  for (_, v_ref), rows, kt_major in zip(segs, seg_rows, seg_kt)]
            o2 = _softmax_pv(parts, vs)
            o = o2[:tq] - lam * o2[tq:]
            o = o * lax.rsqrt(jnp.mean(o * o, axis=1, keepdims=True) + EPS) * subg_ref[...] * (1.0 - lam_init)
            out_ref[0, :, hs] = (o * sag_ref[0, :, hs].astype(F32)).astype(BF16)

    _run_variants(qi, variants, body)


def _seg_spec(a, layer):
    if layer is None:
        return pl.BlockSpec((1,) + a.shape[1:], lambda bi, qi: (bi, 0, 0))
    return pl.BlockSpec((None, 1) + a.shape[2:], lambda bi, qi, layer=layer: (layer, bi, 0, 0))


def _attn_a(aq, sag, slab, a_lambda, subln_g, segs, *, tq, qb0, lam_init):
    b, t, _ = aq.shape
    nqb = t // tq
    seg_kt = tuple(kt for _, _, kt, _ in segs)
    seg_rows = [k.shape[-1] if kt else k.shape[-2] for k, _, kt, _ in segs]
    variants = _variants(nqb, qb0, seg_rows, ATTN_A_VARIANTS)
    kern = functools.partial(_attn_a_kernel, tq=tq, qb0=qb0, variants=variants, lam_init=lam_init, seg_kt=seg_kt)
    qspec = pl.BlockSpec((1, tq, 512), lambda bi, qi: (bi, qi, 0))
    seg_specs, seg_args = [], []
    for k, v, _, layer in segs:
        seg_specs += [_seg_spec(k, layer), _seg_spec(v, layer)]
        seg_args += [k, v]
    return pl.pallas_call(
        kern,
        grid=(b, nqb),
        in_specs=[qspec, qspec,
                  pl.BlockSpec(slab.shape, lambda bi, qi: (0, 0, 0, 0)),
                  pl.BlockSpec((4, DH_A), lambda bi, qi: (0, 0)),
                  pl.BlockSpec((1, DV_A), lambda bi, qi: (0, 0))] + seg_specs,
        out_specs=qspec,
        out_shape=jax.ShapeDtypeStruct((b, t, 512), BF16),
        compiler_params=pltpu.CompilerParams(dimension_semantics=("arbitrary", "arbitrary"),
                                             vmem_limit_bytes=VMEM_LIMIT),
        name="attn_a",
    )(aq, sag, slab, a_lambda, subln_g[None, :], *seg_args)


def _dsa_kernel(*refs, tq, qb0, variants, topk, seg_t):
    bq_ref, bqi_ref, wif_ref, sbg_ref, slab_ref, mslab_ref, tri_ref = refs[:7]
    nseg = len(seg_t)
    segs = [refs[7 + 3 * i:10 + 3 * i] for i in range(nseg)]
    out_ref, keys_ref = refs[7 + 3 * nseg:]
    qi = pl.program_id(1)
    qb = qi + qb0
    lo = lax.broadcasted_iota(jnp.int32, (tq, LANE), 1) < DH_B
    kf = float(topk)

    def body(seg_rows, n_far):
        nkt = sum(r // LANE for r in seg_rows)
        ty = _tile_types(qb, nkt)

        wi = wif_ref[0]
        kt0 = 0
        for (_, _, ki_ref), rows, tr in zip(segs, seg_rows, seg_t):
            ki = (ki_ref[0, :, :rows] if tr else ki_ref[0, :rows, :]).astype(BF16)
            kw = ki.shape[0] if tr else ki.shape[1]
            qis = jnp.concatenate([_head_stack(bqi_ref[0, :, hp * LANE:(hp + 1) * LANE], lo, kw)
                                   for hp in range(H_IDX // 2)], axis=0)
            r = jnp.maximum(_dot(qis, ki) if tr else _dot_nt(qis, ki), 0.0)
            score = wi[:, 0:1] * r[0:tq]
            for h in range(1, H_IDX):
                score = score + wi[:, h:h + 1] * r[h * tq:(h + 1) * tq]
            score = score + jnp.concatenate([mslab_ref[ty[kt0 + j], :tq] for j in range(rows // LANE)], axis=1)
            bits = pltpu.bitcast(score, jnp.int32)
            keys_ref[:, kt0 * LANE:kt0 * LANE + rows] = bits ^ ((bits >> 31) & jnp.int32(0x7FFFFFFF))
            kt0 += rows // LANE

        def count(pred):
            acc = None
            for kt in range(nkt):
                ind = jnp.where(pred(keys_ref[:, kt * LANE:(kt + 1) * LANE]), 1.0, 0.0)
                acc = ind if acc is None else acc + ind
            return jnp.sum(acc, axis=1, keepdims=True)

        def search(i, t):
            cand = t + lax.shift_left(jnp.int32(1), 31 - i)
            return jnp.where(count(lambda x: x >= cand) >= kf, cand, t)

        def search4(i, t):
            d = lax.shift_left(jnp.int32(1), 30 - 2 * i)
            c1 = t + d
            c2 = t + lax.shift_left(d, 1)
            c3 = c2 + d
            accs = [None, None, None]
            for kt in range(nkt):
                tile = keys_ref[:, kt * LANE:(kt + 1) * LANE]
                for j, c in enumerate((c1, c2, c3)):
                    ind = jnp.where(tile >= c, 1.0, 0.0)
                    accs[j] = ind if accs[j] is None else accs[j] + ind
            n1, n2, n3 = (jnp.sum(a, axis=1, keepdims=True) for a in accs)
            return jnp.where(n3 >= kf, c3, jnp.where(n2 >= kf, c2, jnp.where(n1 >= kf, c1, t)))

        t0 = jnp.full((tq, 1), INT_MIN, jnp.int32)
        if nkt <= RADIX4_MAX_TILES:
            t = lax.fori_loop(0, 16, search4, t0, unroll=4)
        else:
            t = lax.fori_loop(0, 32, search, t0, unroll=SEARCH_UNROLL)

        need = kf - count(lambda x: x > t)
        run = jnp.zeros((tq, 1), F32)
        negm = []
        for kt in range(nkt):
            kt_keys = keys_ref[:, kt * LANE:(kt + 1) * LANE]
            eq = jnp.where(kt_keys == t, 1.0, 0.0)
            rank = _dot(eq.astype(BF16), tri_ref[...]) + run
            tie = jnp.where(rank <= need, eq, 0.0)
            negm.append(jnp.where(kt_keys > t, 0.0, (1.0 - tie) * NEG))
            run = run + jnp.sum(eq, axis=1, keepdims=True)
        far2 = [_stack2(negm[kt], negm[kt]) for kt in range(n_far)]

        for hp in range(H_B // 2):
            hs = slice(hp * LANE, (hp + 1) * LANE)
            qp = bq_ref[0, :, hs]
            parts, vs, kt = [], [], 0
            for (k_ref, v_ref, _), rows, tr in zip(segs, seg_rows, seg_t):
                k = (k_ref[0, :, :rows] if tr else k_ref[0, :rows, :]).astype(BF16)
                q2 = _head_stack(qp, lo, k.shape[0] if tr else k.shape[1])
                s = _dot(q2, k) if tr else _dot_nt(q2, k)
                for j in range(rows // LANE):
                    if kt < n_far:
                        nb2 = far2[kt]
                    else:
                        nb2 = _stack2(negm[kt] + slab_ref[2 * hp, ty[kt], :tq],
                                      negm[kt] + slab_ref[2 * hp + 1, ty[kt], :tq])
                    parts.append(s[:, j * LANE:(j + 1) * LANE] + nb2)
                    kt += 1
                vs.append(((v_ref[0, :, :rows] if tr else v_ref[0, :rows, :]).astype(BF16), tr))
            o2 = _softmax_pv(parts, vs)
            if o2.shape[1] == LANE:
                o = jnp.where(lo, o2[:tq], o2[tq:])
            else:
                o = jnp.concatenate([o2[:tq], o2[tq:]], axis=1)
            out_ref[0, :, hs] = (o * sbg_ref[0, :, hs].astype(F32)).astype(BF16)

    _run_variants(qi, variants, body)


def _dsa(bq, bqi, wif, sbg, slab, mslab, tri, segs, *, tq, qb0, topk):
    b, t, _ = bq.shape
    nqb = t // tq
    seg_t = tuple(tr for _, _, _, tr, _ in segs)
    seg_rows = [s[0].shape[-1] if s[3] else s[0].shape[-2] for s in segs]
    variants = _variants(nqb, qb0, seg_rows, DSA_VARIANTS)
    kern = functools.partial(_dsa_kernel, tq=tq, qb0=qb0, variants=variants, topk=topk, seg_t=seg_t)

    def qspec(w):
        return pl.BlockSpec((1, tq, w), lambda bi, qi: (bi, qi, 0))

    seg_specs, seg_args = [], []
    for k, v, ki, _, layer in segs:
        for a in (k, v, ki):
            seg_specs.append(_seg_spec(a, layer))
            seg_args.append(a)
    return pl.pallas_call(
        kern,
        grid=(b, nqb),
        in_specs=[qspec(512), qspec(256), qspec(LANE), qspec(512),
                  pl.BlockSpec(slab.shape, lambda bi, qi: (0, 0, 0, 0)),
                  pl.BlockSpec(mslab.shape, lambda bi, qi: (0, 0, 0)),
                  pl.BlockSpec((LANE, LANE), lambda bi, qi: (0, 0))] + seg_specs,
        out_specs=qspec(512),
        out_shape=jax.ShapeDtypeStruct((b, t, 512), BF16),
        scratch_shapes=[pltpu.VMEM((tq, sum(seg_rows)), jnp.int32)],
        compiler_params=pltpu.CompilerParams(dimension_semantics=("arbitrary", "arbitrary"),
                                             vmem_limit_bytes=VMEM_LIMIT),
        name="dsa",
    )(bq, bqi, wif, sbg, slab, mslab, tri, *seg_args)


def _ret_kernel(cq_ref, ck_ref, cv_ref, scg_ref, st_ref, dmat_ref, dq_ref, dk_ref, gc_ref, bd_ref, gn_ref,
                out_ref, sto_ref, *, c, nchunks):
    lane = lax.broadcasted_iota(jnp.int32, (c, LANE), 1)
    lo = lane < DK_C
    for p in range(H_C // 2):
        st = st_ref[0, p]
        for ci in range(nchunks):
            rows = slice(ci * c, (ci + 1) * c)
            q = cq_ref[0, rows, p * LANE:(p + 1) * LANE]
            k = ck_ref[0, rows, p * LANE:(p + 1) * LANE]
            v = cv_ref[0, rows, p * 2 * DV_C:(p + 1) * 2 * DV_C]
            qd = (q.astype(F32) * dq_ref[p]).astype(BF16)
            cross = _dot(qd, st.astype(BF16))
            zero = jnp.zeros_like(q)
            for j in range(2):
                h = 2 * p + j
                qm = jnp.where(lo, q, zero) if j == 0 else jnp.where(lo, zero, q)
                a = (_dot_nt(qm, k) * dmat_ref[h]).astype(BF16)
                o = cross[:, j * DV_C:(j + 1) * DV_C] + _dot(a, v[:, j * DV_C:(j + 1) * DV_C])
                o = o * lax.rsqrt(jnp.mean(o * o, axis=1, keepdims=True) + EPS) * gn_ref[...]
                gate = scg_ref[0, rows, h * DV_C:(h + 1) * DV_C].astype(F32)
                out_ref[0, rows, h * DV_C:(h + 1) * DV_C] = (o * gate).astype(BF16)
            kd = (k.astype(F32) * dk_ref[p]).astype(BF16)
            st = (gc_ref[p] * st + _dot_tn(kd, v)) * bd_ref[...]
        sto_ref[0, p] = st


def _retention(cq, ck, cv, scg, state_pairs, log_gamma, gn_g, *, c):
    b, t, _ = cq.shape
    nchunks = t // c
    n = jnp.arange(c, dtype=F32)
    diff = n[:, None] - n[None, :]
    dmat = jnp.where(diff >= 0, jnp.exp(log_gamma[:, None, None] * jnp.maximum(diff, 0.0)[None]), 0.0)
    decay_q = jnp.exp((n[:, None] + 1.0) * log_gamma[None, :])
    decay_k = jnp.exp((c - 1.0 - n)[:, None] * log_gamma[None, :])
    decay_c = jnp.exp(c * log_gamma)

    def lanes(tab):
        return jnp.transpose(jnp.repeat(tab, DK_C, axis=1).reshape(c, H_C // 2, LANE), (1, 0, 2))

    gc = jnp.broadcast_to(jnp.repeat(decay_c, DK_C).reshape(H_C // 2, LANE, 1), (H_C // 2, LANE, 2 * DV_C))
    bd = (jnp.arange(LANE)[:, None] // DK_C == jnp.arange(2 * DV_C)[None, :] // DV_C).astype(F32)

    def full(a):
        return pl.BlockSpec(a.shape, lambda bi: (0,) * a.ndim)

    def tok(w):
        return pl.BlockSpec((1, t, w), lambda bi: (bi, 0, 0))

    stspec = pl.BlockSpec((1, H_C // 2, LANE, 2 * DV_C), lambda bi: (bi, 0, 0, 0))
    dq, dk, gnv = lanes(decay_q), lanes(decay_k), gn_g[None, :]
    return pl.pallas_call(
        functools.partial(_ret_kernel, c=c, nchunks=nchunks),
        grid=(b,),
        in_specs=[tok(256), tok(256), tok(512), tok(512), stspec,
                  full(dmat), full(dq), full(dk), full(gc), full(bd), full(gnv)],
        out_specs=[tok(512), stspec],
        out_shape=[jax.ShapeDtypeStruct((b, t, 512), BF16),
                   jax.ShapeDtypeStruct((b, H_C // 2, LANE, 2 * DV_C), F32)],
        compiler_params=pltpu.CompilerParams(dimension_semantics=("arbitrary",), vmem_limit_bytes=VMEM_LIMIT),
        name="retention",
    )(cq, ck, cv, scg, state_pairs, dmat, dq, dk, gc, bd, gnv)


def _state_to_pairs(st):
    b = st.shape[0]
    s4 = st.reshape(b, H_C // 2, 2, DK_C, DV_C)
    z = jnp.zeros_like(s4[:, :, 0])
    top = jnp.concatenate([s4[:, :, 0], z], axis=-1)
    bot = jnp.concatenate([z, s4[:, :, 1]], axis=-1)
    return jnp.concatenate([top, bot], axis=-2)


def _pairs_to_state(sp):
    b = sp.shape[0]
    s0 = sp[:, :, :DK_C, :DV_C]
    s1 = sp[:, :, DK_C:, DV_C:]
    return jnp.stack([s0, s1], axis=2).reshape(b, H_C, DK_C, DV_C)


def _merge_kernel(x_ref, ba_ref, bb_ref, bc_ref, smg_ref, wb_ref, wo_ref, y_ref):
    m = None
    for n, br in enumerate((ba_ref, bb_ref, bc_ref)):
        proj = _dot(br[...], wb_ref[n])
        term = smg_ref[:, n * D_MODEL:(n + 1) * D_MODEL].astype(F32) * proj
        m = term if m is None else m + term
    y_ref[...] = x_ref[...] + _dot(m.astype(BF16), wo_ref[...])


def _merge(x2d, br_a, br_b, br_c, smg, wb, wo, tm):
    n = x2d.shape[0]

    def row(w):
        return pl.BlockSpec((tm, w), lambda i: (i, 0))

    return pl.pallas_call(
        _merge_kernel,
        grid=(n // tm,),
        in_specs=[row(D_MODEL), row(W_BR), row(W_BR), row(W_BR), row(N_BRANCH * D_MODEL),
                  pl.BlockSpec((N_BRANCH, W_BR, D_MODEL), lambda i: (0, 0, 0)),
                  pl.BlockSpec((D_MODEL, D_MODEL), lambda i: (0, 0))],
        out_specs=row(D_MODEL),
        out_shape=jax.ShapeDtypeStruct((n, D_MODEL), F32),
        compiler_params=pltpu.CompilerParams(dimension_semantics=("arbitrary",), vmem_limit_bytes=VMEM_LIMIT),
        name="merge",
    )(x2d, br_a, br_b, br_c, smg, wb, wo)


def _pad_rows(a, rows):
    return jnp.pad(a, ((0, 0), (0, rows - a.shape[1]), (0, 0)))


def kernel(x_prompt, x_sample, cache_a_k, cache_a_v, cache_b_k, cache_b_v, cache_b_kidx, state_c, rel_bias,
           norm_g, w_in, a_qk_g, a_lambda, a_subln_g, b_qk_g, c_gn_g, w_branch, w_out):
    bp, tp, _ = x_prompt.shape
    bs, ts, _ = x_sample.shape
    depth = w_in.shape[0]
    past = cache_a_k.shape[2]
    assert tp % TQ == 0 and ts == CHUNK and past % LANE == 0
    pos_p = jnp.arange(tp, dtype=jnp.int32)
    pos_s = past + jnp.arange(ts, dtype=jnp.int32)
    topk_p = min(TOPK_MAX, tp // 4)
    topk_s = min(TOPK_MAX, (past + ts) // 4)
    log_gamma = jnp.log(1.0 - 2.0 ** (-5.0 - jnp.arange(H_C, dtype=F32)))

    slab_a, mslab = _bias_slabs(rel_bias[:, :H_A])
    slab_b, _ = _bias_slabs(rel_bias[:, H_A:])
    lane_i = jnp.arange(LANE)
    g64 = (lane_i[:, None] // 64 == lane_i[None, :] // 64).astype(BF16)
    tri = (lane_i[:, None] <= lane_i[None, :]).astype(BF16)
    c_prompt = 256 if tp % 256 == 0 else CHUNK
    tm_p = 512
    tm_s = 256 if (bs * ts) % 256 == 0 else ts
    qb_s = past // LANE
    ka_t = jnp.transpose(cache_a_k, (0, 1, 3, 4, 5, 2)).reshape(depth, bs, 512, past)
    va = cache_a_v.reshape(depth, bs, past * H_A, DV_A)
    kb_t, vb_t, kib_t = (jnp.transpose(c, (0, 1, 3, 2)) for c in (cache_b_k, cache_b_v, cache_b_kidx))

    yp = x_prompt.reshape(bp * tp, D_MODEL)
    ys = x_sample.reshape(bs * ts, D_MODEL)
    P = None
    sc_p, small_p = [], {k: [] for k in ("bk", "bv", "bki")}
    outs_s = {k: [] for k in ("ak", "av", "bk", "bv", "bki", "sc")}
    for l in range(depth):
        lam_init = 0.8 - 0.6 * math.exp(-0.3 * l)
        w_head = w_in[l, :, :HEAD_COLS].astype(BF16)
        w_tail = w_in[l, :, TAIL_START:TAIL_START + TAIL_COLS].astype(BF16)
        wb = w_branch[l].astype(BF16)
        wo = w_out[l].astype(BF16)

        P = _project(yp, pos_p, tm_p, norm_g[l], w_head, w_tail, a_qk_g[l], b_qk_g[l], g64, True, l, P)
        r3 = lambda a: a.reshape(bp, tp, a.shape[-1])
        br_a = _attn_a(r3(P["aq"]), r3(P["sag"]), slab_a, a_lambda[l], a_subln_g[l],
                       [(r3(P["akb"]), r3(P["avb"]), False, None)], tq=TQ, qb0=0, lam_init=lam_init)
        br_b = _dsa(r3(P["bq"]), r3(P["bqi"]), r3(P["wif"]), r3(P["sbg"]), slab_b, mslab, tri,
                    [(r3(P["kkb"]), r3(P["vvb"]), r3(P["kib"]), False, None)], tq=TQ, qb0=0, topk=topk_p)
        st0 = jnp.zeros((bp, H_C // 2, LANE, 2 * DV_C), F32)
        br_c, stp = _retention(r3(P["cq"]), r3(P["ck"]), r3(P["cv"]), r3(P["scg"]), st0, log_gamma, c_gn_g[l],
                               c=c_prompt)
        yp = _merge(yp, br_a.reshape(bp * tp, W_BR), br_b.reshape(bp * tp, W_BR), br_c.reshape(bp * tp, W_BR),
                    P["smg"], wb, wo, tm_p)
        sc_p.append(_pairs_to_state(stp))
        for k in small_p:
            small_p[k].append(P[k])

        S = _project(ys, pos_s, tm_s, norm_g[l], w_head, w_tail, a_qk_g[l], b_qk_g[l], g64, False)
        r3s = lambda a: a.reshape(bs, ts, a.shape[-1])
        rk = lambda a: _pad_rows(r3s(a), LANE)
        br_a = _attn_a(r3s(S["aq"]), r3s(S["sag"]), slab_a, a_lambda[l], a_subln_g[l],
                       [(ka_t, va, True, l), (rk(S["akb"]), rk(S["avb"]), False, None)],
                       tq=ts, qb0=qb_s, lam_init=lam_init)
        br_b = _dsa(r3s(S["bq"]), r3s(S["bqi"]), r3s(S["wif"]), r3s(S["sbg"]), slab_b, mslab, tri,
                    [(kb_t, vb_t, kib_t, True, l), (rk(S["kkb"]), rk(S["vvb"]), rk(S["kib"]), False, None)],
                    tq=ts, qb0=qb_s, topk=topk_s)
        br_c, sts = _retention(r3s(S["cq"]), r3s(S["ck"]), r3s(S["cv"]), r3s(S["scg"]),
                               _state_to_pairs(state_c[l].astype(F32)), log_gamma, c_gn_g[l], c=ts)
        ys = _merge(ys, br_a.reshape(bs * ts, W_BR), br_b.reshape(bs * ts, W_BR), br_c.reshape(bs * ts, W_BR),
                    S["smg"], wb, wo, tm_s)
        outs_s["ak"].append(S["ak"].reshape(bs, ts, H_A, 2, DH_A))
        outs_s["av"].append(S["av"].reshape(bs, ts, H_A, DV_A))
        outs_s["bk"].append(S["bk"].reshape(bs, ts, DH_B))
        outs_s["bv"].append(S["bv"].reshape(bs, ts, DH_B))
        outs_s["bki"].append(S["bki"].reshape(bs, ts, D_IDX))
        outs_s["sc"].append(_pairs_to_state(sts).astype(state_c.dtype))

    caches_p = (jnp.transpose(P["ak"].reshape(depth, bp, H_A, 2, DH_A, tp), (0, 1, 5, 2, 3, 4)),
                P["av"].reshape(depth, bp, tp, H_A, DV_A),
                *(jnp.transpose(jnp.concatenate(small_p[k], axis=0), (0, 1, 3, 2)) for k in ("bk", "bv", "bki")),
                jnp.stack(sc_p))
    order = ("ak", "av", "bk", "bv", "bki", "sc")
    return ((yp.reshape(bp, tp, D_MODEL), ys.reshape(bs, ts, D_MODEL)) + caches_p
            + tuple(jnp.stack(outs_s[k]) for k in order))
```

```python
import functools
import math

import jax
import jax.numpy as jnp
from jax import lax
from jax.experimental import pallas as pl
from jax.experimental.pallas import tpu as pltpu

D_MODEL = 1024
CHUNK = 64
EPS = 1e-6
H_A = 4
DH_A = 64
DV_A = 128
H_B = 8
DH_B = 64
H_IDX = 4
D_IDX = 64
TOPK_MAX = 256
H_C = 4
DK_C = 64
DV_C = 128
W_BR = 512
N_BRANCH = 3
NUM_BUCKETS = 32
MAX_DISTANCE = 128
ROPE_BASE = 10000.0

LANE = 128
TQ = 128
NEG = -1e30
LOG2E = math.log2(math.e)
SEARCH_UNROLL = 16
ATTN_A_VARIANTS = 8
DSA_VARIANTS = 8
RADIX4_MAX_TILES = 4
INT_MIN = -(2 ** 31)
F32 = jnp.float32
BF16 = jnp.bfloat16
VMEM_LIMIT = 60 * 1024 * 1024

_SEC = {"aq": (0, 0, 512), "ak": (0, 512, 512), "av": (0, 1024, 512), "ag": (0, 1536, 512), "bq": (0, 2048, 512),
        "kv": (0, 2560, 128), "bqi": (0, 2688, 256), "kiw": (0, 2944, 128),
        "bg": (1, 0, 512), "cq": (1, 512, 256), "ck": (1, 768, 256), "cv": (1, 1024, 512), "cg": (1, 1536, 512),
        "mg": (1, 2048, 3072)}
HEAD_COLS = 3072
TAIL_START = 3012
TAIL_COLS = 5120


def _dot(a, b):
    return jnp.dot(a, b, preferred_element_type=F32)


def _dot_nt(a, b):
    return lax.dot_general(a, b, (((1,), (1,)), ((), ())), preferred_element_type=F32)


def _dot_tn(a, b):
    return lax.dot_general(a, b, (((0,), (0,)), ((), ())), preferred_element_type=F32)


_STACKED = ("ak", "av", "bk", "bv", "bki")
_CARRIED = ("ak", "av")


def _proj_kernel(*refs, seq_minor, layer):
    (x_ref, g_ref, wh_ref, wt_ref, cq_cos_ref, cq_sin_ref, ck_cos_ref, ck_sin_ref,
     gaq_ref, gak_ref, gbq_ref, gbk_ref, g64_ref) = refs[:13]
    n_prev = len(_CARRIED) if seq_minor and layer else 0
    prev_refs = refs[13:13 + n_prev]
    (aq_ref, ak_ref, akb_ref, av_ref, avb_ref, sag_ref, bq_ref, bqi_ref,
     bk_ref, kkb_ref, bv_ref, vvb_ref, bki_ref, kib_ref, wif_ref, sbg_ref,
     cq_ref, ck_ref, cv_ref, scg_ref, smg_ref) = refs[13 + n_prev:]
    for ref, prev in zip((ak_ref, av_ref), prev_refs):
        ref[:layer] = prev[...]
    x = x_ref[...]
    ms = jnp.mean(x * x, axis=-1, keepdims=True)
    xn = ((x * lax.rsqrt(ms + EPS)) * g_ref[...]).astype(BF16)
    w_refs = (wh_ref, wt_ref)

    def mm(name, sub=None):
        op, c0, n = _SEC[name]
        if sub is not None:
            c0, n = c0 + sub[0], sub[1]
        return _dot(xn, w_refs[op][:, c0:c0 + n])

    def group_norm64(z, gain):
        zz = (z * z).astype(BF16)
        parts = [_dot(zz[:, j * LANE:(j + 1) * LANE], g64_ref[...]) for j in range(z.shape[1] // LANE)]
        ss = parts[0] if len(parts) == 1 else jnp.concatenate(parts, axis=1)
        return (z * lax.rsqrt(ss * (1.0 / 64.0) + EPS)) * gain

    lane = lax.broadcasted_iota(jnp.int32, (x.shape[0], LANE), 1)
    lo = lane < 64

    def rotate_half(z):
        first = (lane & 63) < DK_C // 2
        cols = []
        for j in range(z.shape[1] // LANE):
            zj = z[:, j * LANE:(j + 1) * LANE]
            cols.append(jnp.where(first, -pltpu.roll(zj, LANE - DK_C // 2, 1), pltpu.roll(zj, DK_C // 2, 1)))
        return jnp.concatenate(cols, axis=1)

    aq_ref[...] = group_norm64(mm("aq"), gaq_ref[...]).astype(BF16)
    ak = group_norm64(mm("ak"), gak_ref[...])
    akb_ref[...] = ak.astype(BF16)
    av = mm("av")
    avb_ref[...] = av.astype(BF16)
    if seq_minor:
        ak_ref[layer, 0] = jnp.transpose(ak)
        for h in range(H_A):
            av_ref[layer, pl.ds(h, av.shape[0], stride=H_A), :] = av[:, h * DV_A:(h + 1) * DV_A]
    else:
        ak_ref[...] = ak
        av_ref[...] = av
    z = mm("ag")
    sag_ref[...] = (z * jax.nn.sigmoid(z)).astype(BF16)
    bq_ref[...] = group_norm64(mm("bq"), gbq_ref[...]).astype(BF16)
    bqi_ref[...] = mm("bqi").astype(BF16)
    z = mm("kv")
    kv = jnp.where(lo, group_norm64(z, gbk_ref[...]), z)
    vk = pltpu.roll(kv, 64, 1)
    if seq_minor:
        kv_t = jnp.transpose(kv)
        bk_ref[0, 0] = kv_t[:DH_B]
        bv_ref[0, 0] = kv_t[DH_B:]
    else:
        bk_ref[...] = kv[:, :DH_B]
        bv_ref[...] = vk[:, :DH_B]
    kkb_ref[...] = jnp.where(lo, kv, vk).astype(BF16)
    vvb_ref[...] = jnp.where(lo, vk, kv).astype(BF16)
    z = mm("kiw")
    zr = pltpu.roll(z, 64, 1)
    if seq_minor:
        bki_ref[0, 0] = jnp.transpose(z)[:D_IDX]
    else:
        bki_ref[...] = z[:, :D_IDX]
    kib_ref[...] = jnp.where(lo, z, zr).astype(BF16)
    wif_ref[...] = zr
    z = mm("bg")
    sbg_ref[...] = (z * jax.nn.sigmoid(z)).astype(BF16)
    z = mm("cq")
    cq_ref[...] = (z * cq_cos_ref[...] + rotate_half(z) * cq_sin_ref[...]).astype(BF16)
    z = mm("ck")
    ck_ref[...] = (z * ck_cos_ref[...] + rotate_half(z) * ck_sin_ref[...]).astype(BF16)
    cv_ref[...] = mm("cv").astype(BF16)
    z = mm("cg")
    scg_ref[...] = (z * jax.nn.sigmoid(z)).astype(BF16)
    for j in range(N_BRANCH * 2):
        z = mm("mg", (j * 512, 512))
        smg_ref[:, j * 512:(j + 1) * 512] = jax.nn.sigmoid(z).astype(BF16)


def _rope_tables(pos, rows):
    half = DK_C // 2
    inv = ROPE_BASE ** (-jnp.arange(half, dtype=F32) / half)
    ang = pos.astype(F32)[:, None] * inv[None, :]
    cos = jnp.tile(jnp.cos(ang), (1, 2 * H_C))
    sin = jnp.tile(jnp.sin(ang), (1, 2 * H_C))
    reps = max(1, rows // cos.shape[0])
    cos = jnp.tile(cos, (reps, 1))
    sin = jnp.tile(sin, (reps, 1))
    kscale = DK_C ** -0.5
    return cos, sin, cos * kscale, sin * kscale


def _project(x2d, pos, tm, norm_g, w_head, w_tail, a_qk_g, b_qk_g, g64, seq_minor, layer=0, prev=None):
    n = x2d.shape[0]
    seq = pos.shape[0]
    assert not seq_minor or seq % tm == 0
    nt = max(1, seq // tm)
    cq_cos, cq_sin, ck_cos, ck_sin = _rope_tables(pos, tm)
    ntab = cq_cos.shape[0] // tm
    gaq = jnp.tile(a_qk_g[0], 8)[None, :] * (LOG2E * DH_A ** -0.5)
    gak = jnp.tile(a_qk_g[1], 8)[None, :]
    gbq = jnp.tile(b_qk_g[0], 8)[None, :] * (LOG2E * DH_B ** -0.5)
    gbk = jnp.tile(b_qk_g[1], 2)[None, :]

    def row(width):
        return pl.BlockSpec((tm, width), lambda i: (i, 0))

    def const(shape, single=False):
        if single:
            return pl.BlockSpec(shape, lambda i: (0, 0), pipeline_mode=pl.Buffered(1))
        return pl.BlockSpec(shape, lambda i: (0, 0))

    tab = pl.BlockSpec((tm, 256), lambda i: (i % ntab, 0))
    outs = [("aq", 512, BF16), ("ak", 512, F32), ("akb", 512, BF16), ("av", 512, F32), ("avb", 512, BF16),
            ("sag", 512, BF16), ("bq", 512, BF16), ("bqi", 256, BF16), ("bk", 64, F32), ("kkb", 128, BF16),
            ("bv", 64, F32), ("vvb", 128, BF16), ("bki", 64, F32), ("kib", 128, BF16), ("wif", 128, F32),
            ("sbg", 512, BF16), ("cq", 256, BF16), ("ck", 256, BF16), ("cv", 512, BF16), ("scg", 512, BF16),
            ("smg", 3072, BF16)]
    def stacked_spec(name, w, depth, **kw):
        if name == "av":
            return pl.BlockSpec((depth, tm * H_A, DV_A), lambda i: (0, i, 0), **kw)
        return pl.BlockSpec((depth, 1, w, tm), lambda i: (0, i // nt, 0, i % nt), **kw)

    def slots(name):
        return layer + 1 if name in _CARRIED else 1

    def out_spec(name, w):
        return stacked_spec(name, w, slots(name)) if seq_minor and name in _STACKED else row(w)

    def out_shape(name, w, dt):
        if seq_minor and name == "av":
            return jax.ShapeDtypeStruct((slots(name), n * H_A, DV_A), dt)
        if seq_minor and name in _STACKED:
            return jax.ShapeDtypeStruct((slots(name), n // seq, w, seq), dt)
        return jax.ShapeDtypeStruct((n, w), dt)

    widths = {name: w for name, w, _ in outs}
    prev_args = [prev[name] for name in _CARRIED] if seq_minor and layer else []
    prev_specs = [stacked_spec(name, widths[name], layer) for name in _CARRIED] if prev_args else []

    res = pl.pallas_call(
        functools.partial(_proj_kernel, seq_minor=seq_minor, layer=layer),
        grid=(n // tm,),
        in_specs=[row(D_MODEL), const((1, D_MODEL)), const((D_MODEL, HEAD_COLS), single=True),
                  const((D_MODEL, TAIL_COLS), single=True),
                  tab, tab, tab, tab,
                  const((1, 512)), const((1, 512)), const((1, 512)), const((1, 128)), const((LANE, LANE))] + prev_specs,
        out_specs=[out_spec(name, w) for name, w, _ in outs],
        out_shape=[out_shape(name, w, dt) for name, w, dt in outs],
        compiler_params=pltpu.CompilerParams(dimension_semantics=("arbitrary",), vmem_limit_bytes=VMEM_LIMIT),
        name="proj",
    )(x2d, norm_g[None, :], w_head, w_tail, cq_cos, cq_sin, ck_cos, ck_sin, gaq, gak, gbq, gbk, g64, *prev_args)
    return {name: r for (name, _, _), r in zip(outs, res)}


def _t5_bucket(rel):
    nb = NUM_BUCKETS // 2
    max_exact = nb // 2
    ret = jnp.where(rel > 0, nb, 0)
    n = jnp.abs(rel)
    large = max_exact + (jnp.log(jnp.maximum(n, 1).astype(F32) / max_exact)
                         / math.log(MAX_DISTANCE / max_exact) * (nb - max_exact)).astype(jnp.int32)
    large = jnp.minimum(large, nb - 1)
    return ret + jnp.where(n < max_exact, n, large)


def _bias_slabs(bias_cols):
    i = jnp.arange(TQ, dtype=jnp.int32)[:, None]
    j = jnp.arange(LANE, dtype=jnp.int32)[None, :]
    vis0 = j < (i // CHUNK + 1) * CHUNK

    def lookup(rel):
        bucket = _t5_bucket(rel)
        out = jnp.zeros((bias_cols.shape[1],) + rel.shape, F32)
        for b in range(NUM_BUCKETS):
            out = jnp.where(bucket[None] == b, bias_cols[b].astype(F32)[:, None, None], out)
        return out

    far = lookup(-2 * LANE + j - i)
    c = far[:, :1, :1]
    tiles = [(far - c) * LOG2E, (lookup(-LANE + j - i) - c) * LOG2E,
             jnp.where(vis0[None], (lookup(j - i) - c) * LOG2E, NEG), jnp.full_like(far, NEG)]
    zero = jnp.zeros((TQ, LANE), F32)
    masks = [zero, zero, jnp.where(vis0, 0.0, -jnp.inf).astype(F32), jnp.full((TQ, LANE), -jnp.inf, F32)]
    return jnp.stack(tiles, axis=1), jnp.stack(masks, axis=0)


def _tile_types(qb, nkt):
    return [jnp.clip(kt - qb, -2, 1) + 2 for kt in range(nkt)]


def _variants(nqb, qb0, seg_rows, max_variants):
    if len(seg_rows) > 1 or nqb == 1:
        assert nqb == 1
        return [(0, 1, tuple(seg_rows), max(0, qb0 - 1))]
    nvar = min(max_variants, nqb)
    assert nqb % nvar == 0
    grp = nqb // nvar
    nkt_total = seg_rows[0] // LANE
    return [(v * grp, (v + 1) * grp, (min(nkt_total, qb0 + (v + 1) * grp) * LANE,), max(0, qb0 + v * grp - 1))
            for v in range(nvar)]


def _run_variants(qi, variants, body):
    if len(variants) == 1:
        body(variants[0][2], variants[0][3])
    else:
        for lo_q, hi_q, rows, n_far in variants:
            pl.when((qi >= lo_q) & (qi < hi_q))(functools.partial(body, rows, n_far))


def _head_stack(qp, lo, kw):
    if kw == LANE:
        zero = jnp.zeros_like(qp)
        return jnp.concatenate([jnp.where(lo, qp, zero), jnp.where(lo, zero, qp)], axis=0)
    return jnp.concatenate([qp[:, :kw], qp[:, kw:]], axis=0)


def _stack2(a, b):
    return jnp.concatenate([a, b], axis=0)


def _softmax_pv(parts, v_blocks):
    s2 = parts[0] if len(parts) == 1 else jnp.concatenate(parts, axis=1)
    m = jnp.max(s2, axis=1, keepdims=True)
    p = jnp.exp2(s2 - m)
    l = jnp.sum(p, axis=1, keepdims=True)
    pb = p.astype(BF16)
    dv = min(v.shape[0] if t else v.shape[1] for v, t in v_blocks)
    o, c0 = None, 0
    for v, t in v_blocks:
        rows = v.shape[1] if t else v.shape[0]
        pc = pb[:, c0:c0 + rows]
        term = (_dot_nt(pc, v) if t else _dot(pc, v))[:, :dv]
        o = term if o is None else o + term
        c0 += rows
    return o * (1.0 / l)


def _attn_a_kernel(*refs, tq, qb0, variants, lam_init, seg_kt):
    aq_ref, sag_ref, slab_ref, alam_ref, subg_ref = refs[:5]
    nseg = len(seg_kt)
    segs = [refs[5 + 2 * i:7 + 2 * i] for i in range(nseg)]
    out_ref = refs[5 + 2 * nseg]
    qi = pl.program_id(1)
    qb = qi + qb0
    lp = alam_ref[...]
    lam = (jnp.exp(jnp.sum(lp[0:1] * lp[1:2], axis=1, keepdims=True))
           - jnp.exp(jnp.sum(lp[2:3] * lp[3:4], axis=1, keepdims=True)) + lam_init)
    lo = lax.broadcasted_iota(jnp.int32, (tq, LANE), 1) < DH_A

    def body(seg_rows, n_far):
        nkt = sum(r // LANE for r in seg_rows)
        ty = _tile_types(qb, nkt)
        for h in range(H_A):
            hs = slice(h * LANE, (h + 1) * LANE)
            q2 = _head_stack(aq_ref[0, :, hs], lo, LANE)
            parts, kt = [], 0
            for (k_ref, _), rows, kt_major in zip(segs, seg_rows, seg_kt):
                s = (_dot(q2, k_ref[0, hs, :rows].astype(BF16)) if kt_major
                     else _dot_nt(q2, k_ref[0, :rows, hs]))
                n_plain = min(max(n_far - kt, 0), rows // LANE)
                if n_plain:
                    parts.append(s[:, :n_plain * LANE])
                for j in range(n_plain, rows // LANE):
                    b = slab_ref[h, ty[kt + j], :tq]
                    parts.append(s[:, j * LANE:(j + 1) * LANE] + _stack2(b, b))
                kt += rows // LANE
            vs = [(v_ref[0, pl.ds(h, rows, stride=H_A), :].astype(BF16) if kt_major else v_ref[0, :rows, hs], False)
                  for (_, v_ref), rows, kt_major in zip(segs, seg_rows, seg_kt)]
            o2 = _softmax_pv(parts, vs)
            o = o2[:tq] - lam * o2[tq:]
            o = o * lax.rsqrt(jnp.mean(o * o, axis=1, keepdims=True) + EPS) * subg_ref[...] * (1.0 - lam_init)
            out_ref[0, :, hs] = (o * sag_ref[0, :, hs].astype(F32)).astype(BF16)

    _run_variants(qi, variants, body)


def _seg_spec(a, layer):
    if layer is None:
        return pl.BlockSpec((1,) + a.shape[1:], lambda bi, qi: (bi, 0, 0))
    return pl.BlockSpec((None, 1) + a.shape[2:], lambda bi, qi, layer=layer: (layer, bi, 0, 0))


def _attn_a(aq, sag, slab, a_lambda, subln_g, segs, *, tq, qb0, lam_init):
    b, t, _ = aq.shape
    nqb = t // tq
    seg_kt = tuple(kt for _, _, kt, _ in segs)
    seg_rows = [k.shape[-1] if kt else k.shape[-2] for k, _, kt, _ in segs]
    variants = _variants(nqb, qb0, seg_rows, ATTN_A_VARIANTS)
    kern = functools.partial(_attn_a_kernel, tq=tq, qb0=qb0, variants=variants, lam_init=lam_init, seg_kt=seg_kt)
    qspec = pl.BlockSpec((1, tq, 512), lambda bi, qi: (bi, qi, 0))
    seg_specs, seg_args = [], []
    for k, v, _, layer in segs:
        seg_specs += [_seg_spec(k, layer), _seg_spec(v, layer)]
        seg_args += [k, v]
    return pl.pallas_call(
        kern,
        grid=(b, nqb),
        in_specs=[qspec, qspec,
                  pl.BlockSpec(slab.shape, lambda bi, qi: (0, 0, 0, 0)),
                  pl.BlockSpec((4, DH_A), lambda bi, qi: (0, 0)),
                  pl.BlockSpec((1, DV_A), lambda bi, qi: (0, 0))] + seg_specs,
        out_specs=qspec,
        out_shape=jax.ShapeDtypeStruct((b, t, 512), BF16),
        compiler_params=pltpu.CompilerParams(dimension_semantics=("arbitrary", "arbitrary"),
                                             vmem_limit_bytes=VMEM_LIMIT),
        name="attn_a",
    )(aq, sag, slab, a_lambda, subln_g[None, :], *seg_args)


def _dsa_kernel(*refs, tq, qb0, variants, topk, seg_t):
    bq_ref, bqi_ref, wif_ref, sbg_ref, slab_ref, mslab_ref, tri_ref = refs[:7]
    nseg = len(seg_t)
    segs = [refs[7 + 3 * i:10 + 3 * i] for i in range(nseg)]
    out_ref, keys_ref = refs[7 + 3 * nseg:]
    qi = pl.program_id(1)
    qb = qi + qb0
    lo = lax.broadcasted_iota(jnp.int32, (tq, LANE), 1) < DH_B
    kf = float(topk)

    def body(seg_rows, n_far):
        nkt = sum(r // LANE for r in seg_rows)
        ty = _tile_types(qb, nkt)

        wi = wif_ref[0]
        kt0 = 0
        for (_, _, ki_ref), rows, tr in zip(segs, seg_rows, seg_t):
            ki = (ki_ref[0, :, :rows] if tr else ki_ref[0, :rows, :]).astype(BF16)
            kw = ki.shape[0] if tr else ki.shape[1]
            qis = jnp.concatenate([_head_stack(bqi_ref[0, :, hp * LANE:(hp + 1) * LANE], lo, kw)
                                   for hp in range(H_IDX // 2)], axis=0)
            r = jnp.maximum(_dot(qis, ki) if tr else _dot_nt(qis, ki), 0.0)
            score = wi[:, 0:1] * r[0:tq]
            for h in range(1, H_IDX):
                score = score + wi[:, h:h + 1] * r[h * tq:(h + 1) * tq]
            score = score + jnp.concatenate([mslab_ref[ty[kt0 + j], :tq] for j in range(rows // LANE)], axis=1)
            bits = pltpu.bitcast(score, jnp.int32)
            keys_ref[:, kt0 * LANE:kt0 * LANE + rows] = bits ^ ((bits >> 31) & jnp.int32(0x7FFFFFFF))
            kt0 += rows // LANE

        def count(pred):
            acc = None
            for kt in range(nkt):
                ind = jnp.where(pred(keys_ref[:, kt * LANE:(kt + 1) * LANE]), 1.0, 0.0)
                acc = ind if acc is None else acc + ind
            return jnp.sum(acc, axis=1, keepdims=True)

        def search(i, t):
            cand = t + lax.shift_left(jnp.int32(1), 31 - i)
            return jnp.where(count(lambda x: x >= cand) >= kf, cand, t)

        def search4(i, t):
            d = lax.shift_left(jnp.int32(1), 30 - 2 * i)
            c1 = t + d
            c2 = t + lax.shift_left(d, 1)
            c3 = c2 + d
            accs = [None, None, None]
            for kt in range(nkt):
                tile = keys_ref[:, kt * LANE:(kt + 1) * LANE]
                for j, c in enumerate((c1, c2, c3)):
                    ind = jnp.where(tile >= c, 1.0, 0.0)
                    accs[j] = ind if accs[j] is None else accs[j] + ind
            n1, n2, n3 = (jnp.sum(a, axis=1, keepdims=True) for a in accs)
            return jnp.where(n3 >= kf, c3, jnp.where(n2 >= kf, c2, jnp.where(n1 >= kf, c1, t)))

        t0 = jnp.full((tq, 1), INT_MIN, jnp.int32)
        if nkt <= RADIX4_MAX_TILES:
            t = lax.fori_loop(0, 16, search4, t0, unroll=4)
        else:
            t = lax.fori_loop(0, 32, search, t0, unroll=SEARCH_UNROLL)

        need = kf - count(lambda x: x > t)
        run = jnp.zeros((tq, 1), F32)
        negm = []
        for kt in range(nkt):
            kt_keys = keys_ref[:, kt * LANE:(kt + 1) * LANE]
            eq = jnp.where(kt_keys == t, 1.0, 0.0)
            rank = _dot(eq.astype(BF16), tri_ref[...]) + run
            tie = jnp.where(rank <= need, eq, 0.0)
            negm.append(jnp.where(kt_keys > t, 0.0, (1.0 - tie) * NEG))
            run = run + jnp.sum(eq, axis=1, keepdims=True)

        for hp in range(H_B // 2):
            hs = slice(hp * LANE, (hp + 1) * LANE)
            qp = bq_ref[0, :, hs]
            parts, vs, kt = [], [], 0
            for (k_ref, v_ref, _), rows, tr in zip(segs, seg_rows, seg_t):
                k = (k_ref[0, :, :rows] if tr else k_ref[0, :rows, :]).astype(BF16)
                q2 = _head_stack(qp, lo, k.shape[0] if tr else k.shape[1])
                s = _dot(q2, k) if tr else _dot_nt(q2, k)
                for j in range(rows // LANE):
                    if kt < n_far:
                        nb2 = _stack2(negm[kt], negm[kt])
                    else:
                        nb2 = _stack2(negm[kt] + slab_ref[2 * hp, ty[kt], :tq],
                                      negm[kt] + slab_ref[2 * hp + 1, ty[kt], :tq])
                    parts.append(s[:, j * LANE:(j + 1) * LANE] + nb2)
                    kt += 1
                vs.append(((v_ref[0, :, :rows] if tr else v_ref[0, :rows, :]).astype(BF16), tr))
            o2 = _softmax_pv(parts, vs)
            if o2.shape[1] == LANE:
                o = jnp.where(lo, o2[:tq], o2[tq:])
            else:
                o = jnp.concatenate([o2[:tq], o2[tq:]], axis=1)
            out_ref[0, :, hs] = (o * sbg_ref[0, :, hs].astype(F32)).astype(BF16)

    _run_variants(qi, variants, body)


def _dsa(bq, bqi, wif, sbg, slab, mslab, tri, segs, *, tq, qb0, topk):
    b, t, _ = bq.shape
    nqb = t // tq
    seg_t = tuple(tr for _, _, _, tr, _ in segs)
    seg_rows = [s[0].shape[-1] if s[3] else s[0].shape[-2] for s in segs]
    variants = _variants(nqb, qb0, seg_rows, DSA_VARIANTS)
    kern = functools.partial(_dsa_kernel, tq=tq, qb0=qb0, variants=variants, topk=topk, seg_t=seg_t)

    def qspec(w):
        return pl.BlockSpec((1, tq, w), lambda bi, qi: (bi, qi, 0))

    seg_specs, seg_args = [], []
    for k, v, ki, _, layer in segs:
        for a in (k, v, ki):
            seg_specs.append(_seg_spec(a, layer))
            seg_args.append(a)
    return pl.pallas_call(
        kern,
        grid=(b, nqb),
        in_specs=[qspec(512), qspec(256), qspec(LANE), qspec(512),
                  pl.BlockSpec(slab.shape, lambda bi, qi: (0, 0, 0, 0)),
                  pl.BlockSpec(mslab.shape, lambda bi, qi: (0, 0, 0)),
                  pl.BlockSpec((LANE, LANE), lambda bi, qi: (0, 0))] + seg_specs,
        out_specs=qspec(512),
        out_shape=jax.ShapeDtypeStruct((b, t, 512), BF16),
        scratch_shapes=[pltpu.VMEM((tq, sum(seg_rows)), jnp.int32)],
        compiler_params=pltpu.CompilerParams(dimension_semantics=("arbitrary", "arbitrary"),
                                             vmem_limit_bytes=VMEM_LIMIT),
        name="dsa",
    )(bq, bqi, wif, sbg, slab, mslab, tri, *seg_args)


def _ret_kernel(cq_ref, ck_ref, cv_ref, scg_ref, st_ref, dmat_ref, dq_ref, dk_ref, gc_ref, bd_ref, gn_ref,
                out_ref, sto_ref, *, c, nchunks):
    lane = lax.broadcasted_iota(jnp.int32, (c, LANE), 1)
    lo = lane < DK_C
    for p in range(H_C // 2):
        st = st_ref[0, p]
        for ci in range(nchunks):
            rows = slice(ci * c, (ci + 1) * c)
            q = cq_ref[0, rows, p * LANE:(p + 1) * LANE]
            k = ck_ref[0, rows, p * LANE:(p + 1) * LANE]
            v = cv_ref[0, rows, p * 2 * DV_C:(p + 1) * 2 * DV_C]
            qd = (q.astype(F32) * dq_ref[p]).astype(BF16)
            cross = _dot(qd, st.astype(BF16))
            zero = jnp.zeros_like(q)
            for j in range(2):
                h = 2 * p + j
                qm = jnp.where(lo, q, zero) if j == 0 else jnp.where(lo, zero, q)
                a = (_dot_nt(qm, k) * dmat_ref[h]).astype(BF16)
                o = cross[:, j * DV_C:(j + 1) * DV_C] + _dot(a, v[:, j * DV_C:(j + 1) * DV_C])
                o = o * lax.rsqrt(jnp.mean(o * o, axis=1, keepdims=True) + EPS) * gn_ref[...]
                gate = scg_ref[0, rows, h * DV_C:(h + 1) * DV_C].astype(F32)
                out_ref[0, rows, h * DV_C:(h + 1) * DV_C] = (o * gate).astype(BF16)
            kd = (k.astype(F32) * dk_ref[p]).astype(BF16)
            st = (gc_ref[p] * st + _dot_tn(kd, v)) * bd_ref[...]
        sto_ref[0, p] = st


def _retention(cq, ck, cv, scg, state_pairs, log_gamma, gn_g, *, c):
    b, t, _ = cq.shape
    nchunks = t // c
    n = jnp.arange(c, dtype=F32)
    diff = n[:, None] - n[None, :]
    dmat = jnp.where(diff >= 0, jnp.exp(log_gamma[:, None, None] * jnp.maximum(diff, 0.0)[None]), 0.0)
    decay_q = jnp.exp((n[:, None] + 1.0) * log_gamma[None, :])
    decay_k = jnp.exp((c - 1.0 - n)[:, None] * log_gamma[None, :])
    decay_c = jnp.exp(c * log_gamma)

    def lanes(tab):
        return jnp.transpose(jnp.repeat(tab, DK_C, axis=1).reshape(c, H_C // 2, LANE), (1, 0, 2))

    gc = jnp.broadcast_to(jnp.repeat(decay_c, DK_C).reshape(H_C // 2, LANE, 1), (H_C // 2, LANE, 2 * DV_C))
    bd = (jnp.arange(LANE)[:, None] // DK_C == jnp.arange(2 * DV_C)[None, :] // DV_C).astype(F32)

    def full(a):
        return pl.BlockSpec(a.shape, lambda bi: (0,) * a.ndim)

    def tok(w):
        return pl.BlockSpec((1, t, w), lambda bi: (bi, 0, 0))

    stspec = pl.BlockSpec((1, H_C // 2, LANE, 2 * DV_C), lambda bi: (bi, 0, 0, 0))
    dq, dk, gnv = lanes(decay_q), lanes(decay_k), gn_g[None, :]
    return pl.pallas_call(
        functools.partial(_ret_kernel, c=c, nchunks=nchunks),
        grid=(b,),
        in_specs=[tok(256), tok(256), tok(512), tok(512), stspec,
                  full(dmat), full(dq), full(dk), full(gc), full(bd), full(gnv)],
        out_specs=[tok(512), stspec],
        out_shape=[jax.ShapeDtypeStruct((b, t, 512), BF16),
                   jax.ShapeDtypeStruct((b, H_C // 2, LANE, 2 * DV_C), F32)],
        compiler_params=pltpu.CompilerParams(dimension_semantics=("arbitrary",), vmem_limit_bytes=VMEM_LIMIT),
        name="retention",
    )(cq, ck, cv, scg, state_pairs, dmat, dq, dk, gc, bd, gnv)


def _state_to_pairs(st):
    b = st.shape[0]
    s4 = st.reshape(b, H_C // 2, 2, DK_C, DV_C)
    z = jnp.zeros_like(s4[:, :, 0])
    top = jnp.concatenate([s4[:, :, 0], z], axis=-1)
    bot = jnp.concatenate([z, s4[:, :, 1]], axis=-1)
    return jnp.concatenate([top, bot], axis=-2)


def _pairs_to_state(sp):
    b = sp.shape[0]
    s0 = sp[:, :, :DK_C, :DV_C]
    s1 = sp[:, :, DK_C:, DV_C:]
    return jnp.stack([s0, s1], axis=2).reshape(b, H_C, DK_C, DV_C)


def _merge_kernel(x_ref, ba_ref, bb_ref, bc_ref, smg_ref, wb_ref, wo_ref, y_ref):
    m = None
    for n, br in enumerate((ba_ref, bb_ref, bc_ref)):
        proj = _dot(br[...], wb_ref[n])
        term = smg_ref[:, n * D_MODEL:(n + 1) * D_MODEL].astype(F32) * proj
        m = term if m is None else m + term
    y_ref[...] = x_ref[...] + _dot(m.astype(BF16), wo_ref[...])


def _merge(x2d, br_a, br_b, br_c, smg, wb, wo, tm):
    n = x2d.shape[0]

    def row(w):
        return pl.BlockSpec((tm, w), lambda i: (i, 0))

    return pl.pallas_call(
        _merge_kernel,
        grid=(n // tm,),
        in_specs=[row(D_MODEL), row(W_BR), row(W_BR), row(W_BR), row(N_BRANCH * D_MODEL),
                  pl.BlockSpec((N_BRANCH, W_BR, D_MODEL), lambda i: (0, 0, 0)),
                  pl.BlockSpec((D_MODEL, D_MODEL), lambda i: (0, 0))],
        out_specs=row(D_MODEL),
        out_shape=jax.ShapeDtypeStruct((n, D_MODEL), F32),
        compiler_params=pltpu.CompilerParams(dimension_semantics=("arbitrary",), vmem_limit_bytes=VMEM_LIMIT),
        name="merge",
    )(x2d, br_a, br_b, br_c, smg, wb, wo)


def _pad_rows(a, rows):
    return jnp.pad(a, ((0, 0), (0, rows - a.shape[1]), (0, 0)))


def kernel(x_prompt, x_sample, cache_a_k, cache_a_v, cache_b_k, cache_b_v, cache_b_kidx, state_c, rel_bias,
           norm_g, w_in, a_qk_g, a_lambda, a_subln_g, b_qk_g, c_gn_g, w_branch, w_out):
    bp, tp, _ = x_prompt.shape
    bs, ts, _ = x_sample.shape
    depth = w_in.shape[0]
    past = cache_a_k.shape[2]
    assert tp % TQ == 0 and ts == CHUNK and past % LANE == 0
    pos_p = jnp.arange(tp, dtype=jnp.int32)
    pos_s = past + jnp.arange(ts, dtype=jnp.int32)
    topk_p = min(TOPK_MAX, tp // 4)
    topk_s = min(TOPK_MAX, (past + ts) // 4)
    log_gamma = jnp.log(1.0 - 2.0 ** (-5.0 - jnp.arange(H_C, dtype=F32)))

    slab_a, mslab = _bias_slabs(rel_bias[:, :H_A])
    slab_b, _ = _bias_slabs(rel_bias[:, H_A:])
    lane_i = jnp.arange(LANE)
    g64 = (lane_i[:, None] // 64 == lane_i[None, :] // 64).astype(BF16)
    tri = (lane_i[:, None] <= lane_i[None, :]).astype(BF16)
    c_prompt = 256 if tp % 256 == 0 else CHUNK
    tm_p = 512
    tm_s = 256 if (bs * ts) % 256 == 0 else ts
    qb_s = past // LANE
    ka_t = jnp.transpose(cache_a_k, (0, 1, 3, 4, 5, 2)).reshape(depth, bs, 512, past)
    va = cache_a_v.reshape(depth, bs, past * H_A, DV_A)
    kb_t, vb_t, kib_t = (jnp.transpose(c, (0, 1, 3, 2)) for c in (cache_b_k, cache_b_v, cache_b_kidx))

    yp = x_prompt.reshape(bp * tp, D_MODEL)
    ys = x_sample.reshape(bs * ts, D_MODEL)
    P = None
    sc_p, small_p = [], {k: [] for k in ("bk", "bv", "bki")}
    outs_s = {k: [] for k in ("ak", "av", "bk", "bv", "bki", "sc")}
    for l in range(depth):
        lam_init = 0.8 - 0.6 * math.exp(-0.3 * l)
        w_head = w_in[l, :, :HEAD_COLS].astype(BF16)
        w_tail = w_in[l, :, TAIL_START:TAIL_START + TAIL_COLS].astype(BF16)
        wb = w_branch[l].astype(BF16)
        wo = w_out[l].astype(BF16)

        P = _project(yp, pos_p, tm_p, norm_g[l], w_head, w_tail, a_qk_g[l], b_qk_g[l], g64, True, l, P)
        r3 = lambda a: a.reshape(bp, tp, a.shape[-1])
        br_a = _attn_a(r3(P["aq"]), r3(P["sag"]), slab_a, a_lambda[l], a_subln_g[l],
                       [(r3(P["akb"]), r3(P["avb"]), False, None)], tq=TQ, qb0=0, lam_init=lam_init)
        br_b = _dsa(r3(P["bq"]), r3(P["bqi"]), r3(P["wif"]), r3(P["sbg"]), slab_b, mslab, tri,
                    [(r3(P["kkb"]), r3(P["vvb"]), r3(P["kib"]), False, None)], tq=TQ, qb0=0, topk=topk_p)
        st0 = jnp.zeros((bp, H_C // 2, LANE, 2 * DV_C), F32)
        br_c, stp = _retention(r3(P["cq"]), r3(P["ck"]), r3(P["cv"]), r3(P["scg"]), st0, log_gamma, c_gn_g[l],
                               c=c_prompt)
        yp = _merge(yp, br_a.reshape(bp * tp, W_BR), br_b.reshape(bp * tp, W_BR), br_c.reshape(bp * tp, W_BR),
                    P["smg"], wb, wo, tm_p)
        sc_p.append(_pairs_to_state(stp))
        for k in small_p:
            small_p[k].append(P[k])

        S = _project(ys, pos_s, tm_s, norm_g[l], w_head, w_tail, a_qk_g[l], b_qk_g[l], g64, False)
        r3s = lambda a: a.reshape(bs, ts, a.shape[-1])
        rk = lambda a: _pad_rows(r3s(a), LANE)
        br_a = _attn_a(r3s(S["aq"]), r3s(S["sag"]), slab_a, a_lambda[l], a_subln_g[l],
                       [(ka_t, va, True, l), (rk(S["akb"]), rk(S["avb"]), False, None)],
                       tq=ts, qb0=qb_s, lam_init=lam_init)
        br_b = _dsa(r3s(S["bq"]), r3s(S["bqi"]), r3s(S["wif"]), r3s(S["sbg"]), slab_b, mslab, tri,
                    [(kb_t, vb_t, kib_t, True, l), (rk(S["kkb"]), rk(S["vvb"]), rk(S["kib"]), False, None)],
                    tq=ts, qb0=qb_s, topk=topk_s)
        br_c, sts = _retention(r3s(S["cq"]), r3s(S["ck"]), r3s(S["cv"]), r3s(S["scg"]),
                               _state_to_pairs(state_c[l].astype(F32)), log_gamma, c_gn_g[l], c=ts)
        ys = _merge(ys, br_a.reshape(bs * ts, W_BR), br_b.reshape(bs * ts, W_BR), br_c.reshape(bs * ts, W_BR),
                    S["smg"], wb, wo, tm_s)
        outs_s["ak"].append(S["ak"].reshape(bs, ts, H_A, 2, DH_A))
        outs_s["av"].append(S["av"].reshape(bs, ts, H_A, DV_A))
        outs_s["bk"].append(S["bk"].reshape(bs, ts, DH_B))
        outs_s["bv"].append(S["bv"].reshape(bs, ts, DH_B))
        outs_s["bki"].append(S["bki"].reshape(bs, ts, D_IDX))
        outs_s["sc"].append(_pairs_to_state(sts).astype(state_c.dtype))

    caches_p = (jnp.transpose(P["ak"].reshape(depth, bp, H_A, 2, DH_A, tp), (0, 1, 5, 2, 3, 4)),
                P["av"].reshape(depth, bp, tp, H_A, DV_A),
                *(jnp.transpose(jnp.concatenate(small_p[k], axis=0), (0, 1, 3, 2)) for k in ("bk", "bv", "bki")),
                jnp.stack(sc_p))
    order = ("ak", "av", "bk", "bv", "bki", "sc")
    return ((yp.reshape(bp, tp, D_MODEL), ys.reshape(bs, ts, D_MODEL)) + caches_p
            + tuple(jnp.stack(outs_s[k]) for k in order))
```
